```python
import jax
import jax.numpy as jnp
from jax import lax
import numpy as np

D_MODEL = 1024
BATCH = 1
SEQ = 16384
DEPTH = 4

GRID_W = 64
CTX_LEN = 256
N_MIXERS = 4
EPS = 1e-6

CONV_W = 3
GLA_HEADS = 4
GLA_DK = D_MODEL // 2
GLA_DV = D_MODEL
GLA_HK = GLA_DK // GLA_HEADS
GLA_HV = GLA_DV // GLA_HEADS
GLA_RANK = 16
GLA_NORMALIZER = 16.0
GLA_CHUNK = 64
HEAD_DIM = 64
SWA_HEADS = D_MODEL // HEAD_DIM
SWA_KV_HEADS = SWA_HEADS // 4
SWA_WINDOW = 128
SWA_BLOCK = 128
ROPE_BASE = 10000.0
NA_HEADS = D_MODEL // HEAD_DIM
NA_KH = 8
NA_KW = 16
N_EXPERTS = 16
N_GROUPS = 4
EXPERTS_PER_GROUP = N_EXPERTS // N_GROUPS
TOP_K = 2
D_EXPERT = D_MODEL // 2

kernel_name = 'hybrid_diffusion_interleaved_moe'


def _n_layers_of(m):
    return len(range(m, DEPTH, N_MIXERS))


def rms_norm(x, g):
    x32 = x.astype(jnp.float32)
    y = x32 * lax.rsqrt(jnp.mean(x32 * x32, axis=-1, keepdims=True) + EPS)
    return (y * g.astype(jnp.float32)).astype(x.dtype)


def ada_params(cond, w, b):
    return jnp.split(jax.nn.silu(cond) @ w + b, 6, axis=-1)


def centred_conv(u, w):
    L = u.shape[1]
    p = CONV_W // 2
    up = jnp.pad(u, ((0, 0), (p, p), (0, 0)))
    return sum(up[:, i:i + L] * w[i] for i in range(CONV_W))


def gated_short_conv(u, w_in, w_conv, w_out):
    b_gate, c_gate, v = jnp.split(u @ w_in, 3, axis=-1)
    return (b_gate * centred_conv(c_gate * v, w_conv)) @ w_out


def gla_chunked(q, k, v, gk, s0):
    B, L, H, _ = q.shape
    dv = v.shape[-1]
    n = L // GLA_CHUNK

    def chunks(a):
        return a.reshape(B, n, GLA_CHUNK, H, a.shape[-1]).transpose(1, 0, 3, 2, 4)

    qc, kc, vc, gc = chunks(q), chunks(k), chunks(v), chunks(gk)
    b = jnp.cumsum(gc, axis=-2)
    b_last = b[..., -1:, :]
    q_dec = qc * jnp.exp(b)
    k_inv = kc * jnp.exp(-b)
    k_end = kc * jnp.exp(b_last - b)
    prefix = jnp.tril(jnp.ones((GLA_CHUNK, GLA_CHUNK), bool))
    att = jnp.where(prefix, jnp.einsum('nbhid,nbhjd->nbhij', q_dec, k_inv), 0.0)
    o_intra = jnp.einsum('nbhij,nbhjv->nbhiv', att, vc)

    def step(s, xs):
        q_i, k_i, v_i, dec_i = xs
        o_i = jnp.einsum('bhid,bhdv->bhiv', q_i, s)
        s = s * dec_i[..., None] + jnp.einsum('bhid,bhiv->bhdv', k_i, v_i)
        return s, o_i

    s_fin, o_inter = lax.scan(step, s0, (q_dec, k_end, vc, jnp.exp(b_last[..., 0, :])))
    o = (o_intra + o_inter).transpose(1, 0, 3, 2, 4).reshape(B, L, H, dv)
    return o, s_fin


def gla_mixer(h, hc, w_proj, w_gk1, w_gk2, b_gk, g_norm, w_out, with_ctx):
    def project(u):
        B, L, _ = u.shape
        q, k, v, g = jnp.split(u @ w_proj, [GLA_DK, 2 * GLA_DK, 2 * GLA_DK + GLA_DV], axis=-1)
        heads = lambda a: a.reshape(B, L, GLA_HEADS, -1).astype(jnp.float32)
        gates = [heads(jax.nn.log_sigmoid(((u @ w_gk1[d]) @ w_gk2[d] + b_gk[d]).astype(jnp.float32)) / GLA_NORMALIZER)
                 for d in range(2)]
        return heads(q) * GLA_HK ** -0.5, heads(k), heads(v), gates[0], gates[1], g

    def readout(o, g, dtype):
        on = o * lax.rsqrt(jnp.mean(o * o, axis=-1, keepdims=True) + EPS) * g_norm.astype(jnp.float32)
        gate = jax.nn.silu(g.astype(jnp.float32)).reshape(o.shape)
        return (on * gate).reshape(o.shape[0], o.shape[1], -1).astype(dtype) @ w_out

    flip = lambda a: jnp.flip(a, axis=1)
    qx, kx, vx, gfx, gbx, gx = project(h)
    qc, kc, vc, gfc, gbc, gc = project(hc)
    s0 = jnp.zeros((h.shape[0], GLA_HEADS, GLA_HK, GLA_HV), jnp.float32)
    oc_f, sc_f = gla_chunked(qc, kc, vc, gfc, s0)
    oc_b, sc_b = gla_chunked(flip(qc), flip(kc), flip(vc), flip(gbc), s0)
    ox_f, _ = gla_chunked(qx, kx, vx, gfx, sc_f)
    ox_b, _ = gla_chunked(flip(qx), flip(kx), flip(vx), flip(gbx), sc_b)
    y = readout(ox_f + flip(ox_b), gx, h.dtype)
    yc = readout(oc_f + flip(oc_b), gc, hc.dtype) if with_ctx else None
    return y, yc


def axial_rope_tables(row, col):
    half = HEAD_DIM // 2
    inv_freq = ROPE_BASE ** (-jnp.arange(0, half, 2, dtype=jnp.float32) / half)

    def tab(pos):
        ang = pos.astype(jnp.float32)[:, None] * inv_freq
        ang = jnp.concatenate([ang, ang], axis=-1)
        return jnp.cos(ang), jnp.sin(ang)

    cr, sr = tab(row)
    cc, sc = tab(col)
    return jnp.concatenate([cr, cc], axis=-1), jnp.concatenate([sr, sc], axis=-1)


def apply_axial_rope(x, cos, sin):
    x1, x2, x3, x4 = jnp.split(x, 4, axis=-1)
    rot = jnp.concatenate([-x2, x1, -x4, x3], axis=-1)
    return x * cos[:, None, :] + rot * sin[:, None, :]


def swa_mixer(h, hc, w_qkv, sinks, w_out, cos, sin, with_ctx):
    B, L, _ = h.shape
    Lc = hc.shape[1]
    G = SWA_HEADS // SWA_KV_HEADS
    nb = L // SWA_BLOCK
    M = 3 * SWA_BLOCK

    def project(u):
        Lu = u.shape[1]
        q, k, v = jnp.split(u @ w_qkv, [SWA_HEADS * HEAD_DIM, (SWA_HEADS + SWA_KV_HEADS) * HEAD_DIM], axis=-1)
        return (q.reshape(B, Lu, SWA_HEADS, HEAD_DIM).astype(jnp.float32) * HEAD_DIM ** -0.5,
                k.reshape(B, Lu, SWA_KV_HEADS, HEAD_DIM).astype(jnp.float32),
                v.reshape(B, Lu, SWA_KV_HEADS, HEAD_DIM).astype(jnp.float32))

    q, k, v = project(h)
    q, k = apply_axial_rope(q, cos, sin), apply_axial_rope(k, cos, sin)
    qc, kc, vc = project(hc)
    sink = sinks.astype(jnp.float32).reshape(SWA_KV_HEADS, G)

    def banded(a):
        ap = jnp.pad(a, ((0, 0), (SWA_BLOCK, SWA_BLOCK), (0, 0), (0, 0)))
        ap = ap.reshape(B, nb + 2, SWA_BLOCK, SWA_KV_HEADS, HEAD_DIM)
        return jnp.concatenate([ap[:, :-2], ap[:, 1:-1], ap[:, 2:]], axis=2)

    kw, vw = banded(k), banded(v)
    qi = jnp.arange(SWA_BLOCK)[:, None]
    kj = jnp.arange(M)[None, :] - SWA_BLOCK
    kpos = jnp.arange(nb)[:, None, None] * SWA_BLOCK + kj[None]
    mask = (jnp.abs(kj - qi) <= SWA_WINDOW)[None] & (kpos >= 0) & (kpos < L)
    qb = q.reshape(B, nb, SWA_BLOCK, SWA_KV_HEADS, G, HEAD_DIM)
    s_loc = jnp.einsum('bnqkgd,bnmkd->bnkgqm', qb, kw)
    s_loc = jnp.where(mask[None, :, None, None], s_loc, -jnp.inf)
    s_ctx = jnp.einsum('bnqkgd,bckd->bnkgqc', qb, kc)
    s_sink = jnp.broadcast_to(sink[None, None, :, :, None, None], s_loc.shape[:-1] + (1,))
    p = jax.nn.softmax(jnp.concatenate([s_loc, s_ctx, s_sink], axis=-1), axis=-1)
    o = (jnp.einsum('bnkgqm,bnmkd->bnqkgd', p[..., :M], vw)
         + jnp.einsum('bnkgqc,bckd->bnqkgd', p[..., M:M + Lc], vc))
    y = o.reshape(B, L, -1).astype(h.dtype) @ w_out
    if not with_ctx:
        return y, None
    qcg = qc.reshape(B, Lc, SWA_KV_HEADS, G, HEAD_DIM)
    sc_ = jnp.einsum('bqkgd,bckd->bkgqc', qcg, kc)
    sk = jnp.broadcast_to(sink[None, :, :, None, None], sc_.shape[:-1] + (1,))
    pc = jax.nn.softmax(jnp.concatenate([sc_, sk], axis=-1), axis=-1)
    oc = jnp.einsum('bkgqc,bckd->bqkgd', pc[..., :Lc], vc)
    yc = oc.reshape(B, Lc, -1).astype(hc.dtype) @ w_out
    return y, yc


def na_mixer(h, hc, w_qkv, rpb, w_out, with_ctx):
    B, L, _ = h.shape
    Lc = hc.shape[1]
    rows = L // GRID_W
    kh = min(NA_KH, rows)
    n_nb = kh * NA_KW

    def project(u):
        q, k, v = jnp.split(u @ w_qkv, 3, axis=-1)
        sh = (B, u.shape[1], NA_HEADS, HEAD_DIM)
        return (q.reshape(sh).astype(jnp.float32) * HEAD_DIM ** -0.5,
                k.reshape(sh).astype(jnp.float32), v.reshape(sh).astype(jnp.float32))

    q, k, v = project(h)
    qc, kc, vc = project(hc)
    grid = lambda a: a.reshape(B, rows, GRID_W, NA_HEADS, HEAD_DIM)
    kg, vg = grid(k), grid(v)
    row_ids = jnp.arange(rows)
    row_start = jnp.clip(row_ids - kh // 2, 0, rows - kh)
    col_ids = jnp.arange(GRID_W)
    cols = jnp.clip(col_ids - NA_KW // 2, 0, GRID_W - NA_KW)[:, None] + jnp.arange(NA_KW)
    col_off = cols - col_ids[:, None] + NA_KW - 1
    rpb32 = rpb.astype(jnp.float32)

    def row_block(args):
        q_r, r, r0 = args
        k_nb = lax.dynamic_slice_in_dim(kg, r0, kh, axis=1)[:, :, cols]
        v_nb = lax.dynamic_slice_in_dim(vg, r0, kh, axis=1)[:, :, cols]
        row_off = r0 + jnp.arange(kh) - r + NA_KH - 1
        bias = rpb32[:, row_off[None, :, None], col_off[:, None, :]]
        s_nb = jnp.einsum('bchd,bicjhd->bhcij', q_r, k_nb) + bias
        s_ctx = jnp.einsum('bchd,bnhd->bhcn', q_r, kc)
        p = jax.nn.softmax(jnp.concatenate([s_nb.reshape(B, NA_HEADS, GRID_W, n_nb), s_ctx], axis=-1), axis=-1)
        p_nb = p[..., :n_nb].reshape(B, NA_HEADS, GRID_W, kh, NA_KW)
        return (jnp.einsum('bhcij,bicjhd->bchd', p_nb, v_nb)
                + jnp.einsum('bhcn,bnhd->bchd', p[..., n_nb:], vc))

    o = lax.map(row_block, (jnp.moveaxis(grid(q), 1, 0), row_ids, row_start))
    y = jnp.moveaxis(o, 0, 1).reshape(B, L, -1).astype(h.dtype) @ w_out
    if not with_ctx:
        return y, None
    pc = jax.nn.softmax(jnp.einsum('bqhd,bnhd->bhqn', qc, kc), axis=-1)
    yc = jnp.einsum('bhqn,bnhd->bqhd', pc, vc).reshape(B, Lc, -1).astype(hc.dtype) @ w_out
    return y, yc


def route(h, w_router, b_router):
    B, L, _ = h.shape
    scores = jax.nn.sigmoid(h.astype(jnp.float32) @ w_router.astype(jnp.float32))
    biased = (scores + b_router.astype(jnp.float32)).reshape(B, L, N_GROUPS, EXPERTS_PER_GROUP)
    group_score = jnp.sum(lax.top_k(biased, 2)[0], axis=-1)
    best = jnp.argmax(group_score, axis=-1)
    in_group = (best[..., None] == jnp.arange(N_GROUPS))[..., None]
    masked = jnp.where(in_group, biased, -jnp.inf).reshape(B, L, N_EXPERTS)
    _, idx = lax.top_k(masked, TOP_K)
    w = jnp.take_along_axis(scores, idx, axis=-1)
    w = w / jnp.sum(w, axis=-1, keepdims=True)
    return jnp.sum(jax.nn.one_hot(idx, N_EXPERTS, dtype=jnp.float32) * w[..., None], axis=-2)


def moe_ffn(h, w_router, b_router, w_gate, w_up, w_down):
    gates = route(h, w_router, b_router)
    y = jnp.zeros(h.shape, jnp.float32)
    for e in range(N_EXPERTS):
        a = jax.nn.silu(h @ w_gate[e]) * (h @ w_up[e])
        y = y + gates[..., e:e + 1] * (a @ w_down[e]).astype(jnp.float32)
    return y.astype(h.dtype)


def setup_inputs(seed: int = 0) -> dict:
    key = jax.random.key(seed)
    ks = iter(jax.random.split(key, 48))
    D = D_MODEL
    nrm = lambda shape, scale: jax.random.normal(next(ks), shape, jnp.float32) * scale
    gain = lambda shape: 1.0 + 0.02 * jax.random.normal(next(ks), shape, jnp.float32)
    nA, nB, nC, nD = (_n_layers_of(m) for m in range(N_MIXERS))
    return {
        'x': nrm((BATCH, SEQ, D), 1.0),
        'c': nrm((BATCH, D), 1.0),
        'ctx': nrm((BATCH, CTX_LEN, D), 1.0),
        'c_ctx': nrm((D,), 1.0),
        'w_mod': nrm((DEPTH, D, 6 * D), 0.5 * D ** -0.5),
        'b_mod': nrm((DEPTH, 6 * D), 0.02),
        'g_mix': gain((DEPTH, D)),
        'g_ffn': gain((DEPTH, D)),
        'g_final': gain((D,)),
        'conv_w_in': nrm((nA, D, 3 * D), D ** -0.5),
        'conv_w': nrm((nA, CONV_W, D), CONV_W ** -0.5),
        'conv_w_out': nrm((nA, D, D), D ** -0.5),
        'gla_w_proj': nrm((nB, D, 2 * GLA_DK + 2 * GLA_DV), D ** -0.5),
        'gla_w_gk1': nrm((nB, 2, D, GLA_RANK), D ** -0.5),
        'gla_w_gk2': nrm((nB, 2, GLA_RANK, GLA_DK), GLA_RANK ** -0.5),
        'gla_b_gk': nrm((nB, 2, GLA_DK), 0.1),
        'gla_g_norm': gain((nB, GLA_HV)),
        'gla_w_out': nrm((nB, GLA_DV, D), GLA_DV ** -0.5),
        'swa_w_qkv': nrm((nC, D, (SWA_HEADS + 2 * SWA_KV_HEADS) * HEAD_DIM), D ** -0.5),
        'swa_sinks': nrm((nC, SWA_HEADS), 0.5),
        'swa_w_out': nrm((nC, SWA_HEADS * HEAD_DIM, D), (SWA_HEADS * HEAD_DIM) ** -0.5),
        'na_w_qkv': nrm((nD, D, 3 * NA_HEADS * HEAD_DIM), D ** -0.5),
        'na_rpb': nrm((nD, NA_HEADS, 2 * NA_KH - 1, 2 * NA_KW - 1), 0.1),
        'na_w_out': nrm((nD, NA_HEADS * HEAD_DIM, D), (NA_HEADS * HEAD_DIM) ** -0.5),
        'router_w': nrm((D, N_EXPERTS), D ** -0.5),
        'router_b': nrm((N_EXPERTS,), 0.01),
        'moe_w_gate': nrm((DEPTH, N_EXPERTS, D, D_EXPERT), D ** -0.5),
        'moe_w_up': nrm((DEPTH, N_EXPERTS, D, D_EXPERT), D ** -0.5),
        'moe_w_down': nrm((DEPTH, N_EXPERTS, D_EXPERT, D), D_EXPERT ** -0.5),
    }


def reference(x, c, ctx, c_ctx, w_mod, b_mod, g_mix, g_ffn, g_final,
              conv_w_in, conv_w, conv_w_out,
              gla_w_proj, gla_w_gk1, gla_w_gk2, gla_b_gk, gla_g_norm, gla_w_out,
              swa_w_qkv, swa_sinks, swa_w_out,
              na_w_qkv, na_rpb, na_w_out,
              router_w, router_b, moe_w_gate, moe_w_up, moe_w_down):
    L = x.shape[1]
    t = jnp.arange(L)
    rope_cos, rope_sin = axial_rope_tables(t // GRID_W, t % GRID_W)
    xc = ctx
    for i in range(DEPTH):
        last = i == DEPTH - 1
        kind, j = i % N_MIXERS, i // N_MIXERS
        sh1, sc1, g1, sh2, sc2, g2 = [m[:, None, :] for m in ada_params(c, w_mod[i], b_mod[i])]
        csh1, csc1, cg1, csh2, csc2, cg2 = ada_params(c_ctx, w_mod[i], b_mod[i])
        h = rms_norm(x, g_mix[i]) * (1 + sc1) + sh1
        hc = rms_norm(xc, g_mix[i]) * (1 + csc1) + csh1
        if kind == 0:
            y = gated_short_conv(h, conv_w_in[j], conv_w[j], conv_w_out[j])
            yc = None if last else gated_short_conv(hc, conv_w_in[j], conv_w[j], conv_w_out[j])
        elif kind == 1:
            y, yc = gla_mixer(h, hc, gla_w_proj[j], gla_w_gk1[j], gla_w_gk2[j], gla_b_gk[j],
                              gla_g_norm[j], gla_w_out[j], not last)
        elif kind == 2:
            y, yc = swa_mixer(h, hc, swa_w_qkv[j], swa_sinks[j], swa_w_out[j], rope_cos, rope_sin, not last)
        else:
            y, yc = na_mixer(h, hc, na_w_qkv[j], na_rpb[j], na_w_out[j], not last)
        x = x + g1 * y
        h = rms_norm(x, g_ffn[i]) * (1 + sc2) + sh2
        x = x + g2 * moe_ffn(h, router_w, router_b, moe_w_gate[i], moe_w_up[i], moe_w_down[i])
        if not last:
            xc = xc + cg1 * yc
            hc = rms_norm(xc, g_ffn[i]) * (1 + csc2) + csh2
            xc = xc + cg2 * moe_ffn(hc, router_w, router_b, moe_w_gate[i], moe_w_up[i], moe_w_down[i])
    return rms_norm(x, g_final)
```

```python
import functools

import jax
import jax.numpy as jnp
from jax import lax
from jax.experimental import pallas as pl
from jax.experimental.pallas import tpu as pltpu

F32 = jnp.float32
BF16 = jnp.bfloat16
HIGHEST = lax.Precision.HIGHEST

D_MODEL = 1024
DEPTH = 4
GRID_W = 64
EPS = 1e-6
CONV_W = 3
GLA_HEADS = 4
GLA_DK = D_MODEL // 2
GLA_DV = D_MODEL
GLA_HK = GLA_DK // GLA_HEADS
GLA_HV = GLA_DV // GLA_HEADS
GLA_RANK = 16
GLA_NORMALIZER = 16.0
GLA_CHUNK = 64
HEAD_DIM = 64
SWA_HEADS = D_MODEL // HEAD_DIM
SWA_KV_HEADS = SWA_HEADS // 4
SWA_WINDOW = 128
ROPE_BASE = 10000.0
NA_HEADS = D_MODEL // HEAD_DIM
NA_KH = 8
NA_KW = 16
N_EXPERTS = 16
N_GROUPS = 4
EXPERTS_PER_GROUP = N_EXPERTS // N_GROUPS
D_EXPERT = D_MODEL // 2

TM = 256
LANES = 128
NEG = -1e30
VMEM_LIMIT = 56 * 1024 * 1024

_NT = (((1,), (1,)), ((), ()))
_TN = (((0,), (0,)), ((), ()))


def _params(*sem):
    return pltpu.CompilerParams(dimension_semantics=sem, vmem_limit_bytes=VMEM_LIMIT)


def _norm_mod(x, g, scale, shift):
    ms = jnp.mean(x * x, axis=-1, keepdims=True)
    return (x * lax.rsqrt(ms + EPS) * g) * (1.0 + scale) + shift


def _silu(x):
    return x * jax.nn.sigmoid(x)


def _ada_kernel(cond_ref, w_ref, b_ref, o_ref):
    s = _silu(cond_ref[...])
    o_ref[0] = jnp.dot(s, w_ref[0], precision=HIGHEST, preferred_element_type=F32) + b_ref[0]


def _ada_params(c, c_ctx, w_mod, b_mod):
    depth, d, n = w_mod.shape
    nc = n // 4
    cond = jnp.zeros((8, d), F32).at[0].set(c[0]).at[1].set(c_ctx)
    out = pl.pallas_call(
        _ada_kernel,
        out_shape=jax.ShapeDtypeStruct((depth, 8, n), F32),
        grid=(depth, n // nc),
        in_specs=[pl.BlockSpec((8, d), lambda l, j: (0, 0)),
                  pl.BlockSpec((1, d, nc), lambda l, j: (l, 0, j)),
                  pl.BlockSpec((1, 1, nc), lambda l, j: (l, 0, j))],
        out_specs=pl.BlockSpec((1, 8, nc), lambda l, j: (l, 0, j)),
        compiler_params=_params("parallel", "parallel"),
        name="ada_params",
    )(cond, w_mod, b_mod.reshape(depth, 1, n))
    return out[:, :2].reshape(depth, 2, 6, d)


def _proj_kernel(x_ref, g_ref, mod_ref, w_ref, *rest, n_chunk, q_cols, q_scale, rope_cols, n_main):
    if rope_cols:
        cos_ref, sa_ref, sb_ref = rest[:3]
        rest = rest[3:]
    o_ref = rest[0]
    h = _norm_mod(x_ref[...], g_ref[...], mod_ref[0, 1:2, :], mod_ref[0, 0:1, :]).astype(BF16)
    for c0 in range(0, n_main, n_chunk):
        c1 = min(c0 + n_chunk, n_main)
        y = jnp.dot(h, w_ref[:, c0:c1], preferred_element_type=F32)
        if c0 < q_cols:
            y = y * q_scale
        if c0 >= rope_cols:
            o_ref[:, c0:c1] = y.astype(o_ref.dtype)
            continue
        for s0 in range(c0, c1, LANES):
            ys = y[:, s0 - c0:s0 - c0 + LANES]
            if s0 < rope_cols:
                ys = (ys * cos_ref[...] + pltpu.roll(ys, LANES - 16, 1) * sa_ref[...]
                      + pltpu.roll(ys, 16, 1) * sb_ref[...])
            o_ref[:, s0:s0 + LANES] = ys.astype(o_ref.dtype)
    if n_main < w_ref.shape[1]:
        rest[1][...] = jnp.dot(h, w_ref[:, n_main:], preferred_element_type=F32)


def _project(x, g, mods, w, *, n_x_tiles, n_chunk=512, q_cols=0, q_scale=1.0, rope=None, rope_cols=0,
             n_extra=0):
    r, d = x.shape
    n_total = w.shape[1]
    n_main = n_total - n_extra
    nt = r // TM
    in_specs = [pl.BlockSpec((TM, d), lambda i: (i, 0)),
                pl.BlockSpec((1, d), lambda i: (0, 0)),
                pl.BlockSpec((1, 6, d), lambda i: (i // n_x_tiles, 0, 0)),
                pl.BlockSpec((d, n_total), lambda i: (0, 0))]
    args = [x, g.reshape(1, d), mods, w]
    if rope_cols:
        in_specs += [pl.BlockSpec((TM, LANES), lambda i: (i, 0))] * 3
        args += list(rope)
    out_shape = [jax.ShapeDtypeStruct((r, n_main), BF16)]
    out_specs = [pl.BlockSpec((TM, n_main), lambda i: (i, 0))]
    if n_extra:
        out_shape.append(jax.ShapeDtypeStruct((r, n_extra), F32))
        out_specs.append(pl.BlockSpec((TM, n_extra), lambda i: (i, 0)))
    out = pl.pallas_call(
        functools.partial(_proj_kernel, n_chunk=n_chunk, q_cols=q_cols, q_scale=q_scale,
                          rope_cols=rope_cols, n_main=n_main),
        out_shape=out_shape, grid=(nt,), in_specs=in_specs, out_specs=out_specs,
        compiler_params=_params("parallel"), name="norm_proj",
    )(*args)
    return out if n_extra else out[0]


def _out_proj_kernel(x_ref, y_ref, mod_ref, w_ref, o_ref):
    y = jnp.dot(y_ref[...], w_ref[...], preferred_element_type=F32)
    o_ref[...] = x_ref[...] + mod_ref[0, 2:3, :] * y


def _out_proj(x, ypre, mods, w, *, n_x_tiles):
    r, d = x.shape
    return pl.pallas_call(
        _out_proj_kernel,
        out_shape=jax.ShapeDtypeStruct((r, d), F32),
        grid=(r // TM,),
        in_specs=[pl.BlockSpec((TM, d), lambda i: (i, 0)),
                  pl.BlockSpec((TM, d), lambda i: (i, 0)),
                  pl.BlockSpec((1, 6, d), lambda i: (i // n_x_tiles, 0, 0)),
                  pl.BlockSpec((d, d), lambda i: (0, 0))],
        out_specs=pl.BlockSpec((TM, d), lambda i: (i, 0)),
        compiler_params=_params("parallel"), name="out_proj",
    )(x, ypre, mods, w)


def _conv_kernel(x_ref, bg_ref, cg_ref, v_ref, cgp_ref, vp_ref, cgn_ref, vn_ref, cw_ref, mod_ref, w_ref,
                 o_ref, *, n_x_tiles):
    i = pl.program_id(0)
    has_prev = jnp.logical_and(i != 0, i != n_x_tiles).astype(F32)
    has_next = jnp.logical_and(i != n_x_tiles - 1, i != n_x_tiles).astype(F32)
    z = cg_ref[...].astype(F32) * v_ref[...].astype(F32)
    z_prev = cgp_ref[7:8, :].astype(F32) * vp_ref[7:8, :].astype(F32) * has_prev
    z_next = cgn_ref[0:1, :].astype(F32) * vn_ref[0:1, :].astype(F32) * has_next
    row = lax.broadcasted_iota(jnp.int32, z.shape, 0)
    z_dn = jnp.where(row == 0, z_prev, pltpu.roll(z, 1, 0))
    z_up = jnp.where(row == TM - 1, z_next, pltpu.roll(z, TM - 1, 0))
    conv = z_dn * cw_ref[0:1, :] + z * cw_ref[1:2, :] + z_up * cw_ref[2:3, :]
    ypre = (bg_ref[...].astype(F32) * conv).astype(BF16)
    y = jnp.dot(ypre, w_ref[...], preferred_element_type=F32)
    o_ref[...] = x_ref[...] + mod_ref[0, 2:3, :] * y


def _conv_mixer(x, u, conv_w, mods, w_out, *, n_x_tiles):
    r, d = x.shape
    n8 = r // 8
    t8 = TM // 8
    tile = lambda c: pl.BlockSpec((TM, d), lambda i: (i, c))
    prev = lambda c: pl.BlockSpec((8, d), lambda i: (jnp.maximum(i * t8 - 1, 0), c))
    nxt = lambda c: pl.BlockSpec((8, d), lambda i: (jnp.minimum((i + 1) * t8, n8 - 1), c))
    return pl.pallas_call(
        functools.partial(_conv_kernel, n_x_tiles=n_x_tiles),
        out_shape=jax.ShapeDtypeStruct((r, d), F32),
        grid=(r // TM,),
        in_specs=[pl.BlockSpec((TM, d), lambda i: (i, 0)), tile(0), tile(1), tile(2),
                  prev(1), prev(2), nxt(1), nxt(2),
                  pl.BlockSpec((CONV_W, d), lambda i: (0, 0)),
                  pl.BlockSpec((1, 6, d), lambda i: (i // n_x_tiles, 0, 0)),
                  pl.BlockSpec((d, d), lambda i: (0, 0))],
        out_specs=pl.BlockSpec((TM, d), lambda i: (i, 0)),
        compiler_params=_params("parallel"), name="conv_mixer",
    )(x, u, u, u, u, u, u, u, conv_w, mods, w_out)


def _log_sigmoid(z):
    return jnp.minimum(z, 0.0) - jnp.log1p(jnp.exp(-jnp.abs(z)))


def _gla_kernel(q_ref, k_ref, v_ref, r_ref, w2_ref, b_ref, *rest, reverse, final):
    if final:
        of_ref, g_ref, gn_ref, o_ref, st_ref = rest
    else:
        o_ref, st_ref = rest
    C = GLA_CHUNK

    @pl.when(pl.program_id(0) == 0)
    def _():
        st_ref[...] = jnp.zeros_like(st_ref)

    z = jnp.dot(r_ref[...], w2_ref[...], precision=HIGHEST, preferred_element_type=F32) + b_ref[...]
    gk = _log_sigmoid(z) * (1.0 / GLA_NORMALIZER)
    ii = lax.broadcasted_iota(jnp.int32, (C, C), 0)
    jj = lax.broadcasted_iota(jnp.int32, (C, C), 1)
    causal = (jj >= ii) if reverse else (jj <= ii)
    tri = causal.astype(F32)
    chunks = range(TM // C)
    for c in (reversed(chunks) if reverse else chunks):
        rows = slice(c * C, (c + 1) * C)
        for h in range(GLA_HEADS):
            kcols = slice(h * GLA_HK, (h + 1) * GLA_HK)
            vcols = slice(h * GLA_HV, (h + 1) * GLA_HV)
            bcum = jnp.dot(tri, gk[rows, kcols], precision=HIGHEST, preferred_element_type=F32)
            total = bcum[0:1, :] if reverse else bcum[C - 1:C, :]
            qh = q_ref[rows, kcols].astype(F32) * (GLA_HK ** -0.5)
            kh = k_ref[rows, kcols].astype(F32)
            vh = v_ref[rows, vcols]
            q_dec = (qh * jnp.exp(bcum)).astype(BF16)
            k_inv = (kh * jnp.exp(-bcum)).astype(BF16)
            k_end = (kh * jnp.exp(total - bcum)).astype(BF16)
            att = lax.dot_general(q_dec, k_inv, _NT, preferred_element_type=F32)
            att = jnp.where(causal, att, 0.0).astype(BF16)
            st = st_ref[h]
            o = (jnp.dot(att, vh, preferred_element_type=F32)
                 + lax.dot_general(q_dec, st.astype(BF16), _NT, preferred_element_type=F32))
            st_ref[h] = st * jnp.exp(total) + lax.dot_general(vh, k_end, _TN, preferred_element_type=F32)
            if final:
                o = o + of_ref[rows, vcols].astype(F32)
                on = o * lax.rsqrt(jnp.mean(o * o, axis=-1, keepdims=True) + EPS) * gn_ref[...]
                o = on * _silu(g_ref[rows, vcols].astype(F32))
            o_ref[rows, vcols] = o.astype(o_ref.dtype)


def _gla_pass(u, r_low, w2pad, b_gk, *, reverse, o_fwd=None, g_norm=None):
    rws = u.shape[0]
    nt = rws // TM
    final = o_fwd is not None
    if reverse:
        order = lambda s: nt - 1 - s
    else:
        order = lambda s: (s + nt - 1) % nt
    in_specs = [pl.BlockSpec((TM, GLA_DK), lambda s: (order(s), 0)),
                pl.BlockSpec((TM, GLA_DK), lambda s: (order(s), 1)),
                pl.BlockSpec((TM, GLA_DV), lambda s: (order(s), 1)),
                pl.BlockSpec((TM, LANES), lambda s: (order(s), 0)),
                pl.BlockSpec((LANES, GLA_DK), lambda s: (0, 0)),
                pl.BlockSpec((1, GLA_DK), lambda s: (0, 0))]
    args = [u, u, u, r_low, w2pad, b_gk.reshape(1, GLA_DK)]
    if final:
        in_specs += [pl.BlockSpec((TM, GLA_DV), lambda s: (order(s), 0)),
                     pl.BlockSpec((TM, GLA_DV), lambda s: (order(s), 2)),
                     pl.BlockSpec((1, GLA_HV), lambda s: (0, 0))]
        args += [o_fwd, u, g_norm.reshape(1, GLA_HV)]
    return pl.pallas_call(
        functools.partial(_gla_kernel, reverse=reverse, final=final),
        out_shape=jax.ShapeDtypeStruct((rws, GLA_DV), BF16),
        grid=(nt,), in_specs=in_specs,
        out_specs=pl.BlockSpec((TM, GLA_DV), lambda s: (order(s), 0)),
        scratch_shapes=[pltpu.VMEM((GLA_HEADS, GLA_HV, GLA_HK), F32)],
        compiler_params=_params("arbitrary"), name="gla_bwd" if reverse else "gla_fwd",
    )(*args)


def _swa_kernel(sink_ref, q_ref, kvp_ref, kvo_ref, kvn_ref, kvc_ref, o_ref, kv_buf, *, seq_len):
    i = pl.program_id(0)
    half = TM // 2
    kv_buf[0:half] = kvp_ref[...]
    kv_buf[half:half + TM] = kvo_ref[...]
    kv_buf[half + TM:2 * TM] = kvn_ref[...]
    kv_buf[2 * TM:3 * TM] = kvc_ref[...]
    nk = 3 * TM
    qpos = i * TM + lax.broadcasted_iota(jnp.int32, (TM, nk), 0)
    col = lax.broadcasted_iota(jnp.int32, (TM, nk), 1)
    kpos = i * TM - half + col
    local = ((jnp.abs(kpos - qpos) <= SWA_WINDOW) & (kpos >= 0) & (kpos < seq_len) & (qpos < seq_len))
    valid = local | (col >= 2 * TM)
    kv_w = SWA_KV_HEADS * HEAD_DIM
    group = SWA_HEADS // SWA_KV_HEADS
    for h in range(SWA_HEADS):
        kv = h // group
        qh = q_ref[:, h * HEAD_DIM:(h + 1) * HEAD_DIM]
        kh = kv_buf[:, kv * HEAD_DIM:(kv + 1) * HEAD_DIM]
        vh = kv_buf[:, kv_w + kv * HEAD_DIM:kv_w + (kv + 1) * HEAD_DIM]
        s = lax.dot_general(qh, kh, _NT, preferred_element_type=F32)
        s = jnp.where(valid, s, NEG)
        sink = sink_ref[h]
        m = jnp.maximum(jnp.max(s, axis=-1, keepdims=True), sink)
        p = jnp.exp(s - m)
        l = jnp.sum(p, axis=-1, keepdims=True) + jnp.exp(sink - m)
        o = jnp.dot(p.astype(BF16), vh, preferred_element_type=F32) / l
        o_ref[:, h * HEAD_DIM:(h + 1) * HEAD_DIM] = o.astype(o_ref.dtype)


def _swa_attention(u, sinks, *, seq_len):
    rws = u.shape[0]
    nt = rws // TM
    half = TM // 2
    n_half = rws // half
    kv_w = 2 * SWA_KV_HEADS * HEAD_DIM
    qw = SWA_HEADS * HEAD_DIM
    kvc = qw // kv_w
    return pl.pallas_call(
        functools.partial(_swa_kernel, seq_len=seq_len),
        out_shape=jax.ShapeDtypeStruct((rws, qw), BF16),
        grid=(nt,),
        in_specs=[pl.BlockSpec(memory_space=pltpu.SMEM),
                  pl.BlockSpec((TM, qw), lambda i: (i, 0)),
                  pl.BlockSpec((half, kv_w), lambda i: (jnp.maximum(2 * i - 1, 0), kvc)),
                  pl.BlockSpec((TM, kv_w), lambda i: (i, kvc)),
                  pl.BlockSpec((half, kv_w), lambda i: (jnp.minimum(2 * i + 2, n_half - 1), kvc)),
                  pl.BlockSpec((TM, kv_w), lambda i: (nt - 1, kvc))],
        out_specs=pl.BlockSpec((TM, qw), lambda i: (i, 0)),
        scratch_shapes=[pltpu.VMEM((3 * TM, kv_w), BF16)],
        compiler_params=_params("parallel"), name="swa_attention",
    )(sinks, u, u, u, u, u)


def _na_kernel(q_ref, kp_ref, ko_ref, kn_ref, kc_ref, vp_ref, vo_ref, vn_ref, vc_ref, bias_ref, o_ref,
               k_buf, v_buf, *, n_rows):
    i = pl.program_id(0)
    for j, (kr, vr) in enumerate(((kp_ref, vp_ref), (ko_ref, vo_ref), (kn_ref, vn_ref), (kc_ref, vc_ref))):
        k_buf[j * TM:(j + 1) * TM] = kr[...]
        v_buf[j * TM:(j + 1) * TM] = vr[...]
    n_loc = 3 * TM
    rpt = TM // GRID_W
    qi = lax.broadcasted_iota(jnp.int32, (TM, n_loc), 0)
    ki = lax.broadcasted_iota(jnp.int32, (TM, n_loc), 1)
    r = i * rpt + qi // GRID_W
    c = qi % GRID_W
    krow = (i - 1) * rpt + ki // GRID_W
    kcol = ki % GRID_W
    r0 = jnp.clip(r - NA_KH // 2, 0, n_rows - NA_KH)
    c0 = jnp.clip(c - NA_KW // 2, 0, GRID_W - NA_KW)
    valid = ((krow >= r0) & (krow < r0 + NA_KH) & (kcol >= c0) & (kcol < c0 + NA_KW) & (r < n_rows))
    for h in range(NA_HEADS):
        cols = slice(h * HEAD_DIM, (h + 1) * HEAD_DIM)
        qh = q_ref[:, cols]
        s = lax.dot_general(qh, k_buf[:, cols], _NT, preferred_element_type=F32)
        s_loc = jnp.where(valid, s[:, :n_loc] + bias_ref[h], NEG)
        s_ctx = s[:, n_loc:]
        m = jnp.maximum(jnp.max(s_loc, axis=-1, keepdims=True), jnp.max(s_ctx, axis=-1, keepdims=True))
        p_loc = jnp.exp(s_loc - m)
        p_ctx = jnp.exp(s_ctx - m)
        l = jnp.sum(p_loc, axis=-1, keepdims=True) + jnp.sum(p_ctx, axis=-1, keepdims=True)
        o = (jnp.dot(p_loc.astype(BF16), v_buf[0:n_loc, cols], preferred_element_type=F32)
             + jnp.dot(p_ctx.astype(BF16), v_buf[n_loc:, cols], preferred_element_type=F32)) / l
        o_ref[:, cols] = o.astype(o_ref.dtype)


def _na_bias_table(rpb):
    rpt = TM // GRID_W
    a = jnp.arange(rpt)
    b = jnp.arange(3 * rpt)
    c = jnp.arange(GRID_W)
    ri = jnp.clip(b[None, :] - rpt - a[:, None] + NA_KH - 1, 0, 2 * NA_KH - 2)
    ci = jnp.clip(c[None, :] - c[:, None] + NA_KW - 1, 0, 2 * NA_KW - 2)
    t = rpb.astype(F32)[:, ri[:, None, :, None], ci[None, :, None, :]]
    return t.reshape(rpb.shape[0], TM, 3 * TM)


def _na_attention(u, bias, *, n_rows):
    rws = u.shape[0]
    nt = rws // TM
    d = NA_HEADS * HEAD_DIM
    blk = lambda f, c: pl.BlockSpec((TM, d), lambda i: (f(i), c))
    prev = lambda i: jnp.maximum(i - 1, 0)
    own = lambda i: i
    nxt = lambda i: jnp.minimum(i + 1, nt - 1)
    ctx = lambda i: nt - 1
    return pl.pallas_call(
        functools.partial(_na_kernel, n_rows=n_rows),
        out_shape=jax.ShapeDtypeStruct((rws, d), BF16),
        grid=(nt,),
        in_specs=[blk(own, 0), blk(prev, 1), blk(own, 1), blk(nxt, 1), blk(ctx, 1),
                  blk(prev, 2), blk(own, 2), blk(nxt, 2), blk(ctx, 2),
                  pl.BlockSpec((NA_HEADS, TM, 3 * TM), lambda i: (0, 0, 0))],
        out_specs=pl.BlockSpec((TM, d), lambda i: (i, 0)),
        scratch_shapes=[pltpu.VMEM((4 * TM, d), BF16), pltpu.VMEM((4 * TM, d), BF16)],
        compiler_params=_params("parallel"), name="na_attention",
    )(u, u, u, u, u, u, u, u, u, bias)


def _router_kernel(x_ref, g_ref, mod_ref, wr_ref, br_ref, o_ref):
    h = _norm_mod(x_ref[...], g_ref[...], mod_ref[0, 4:5, :], mod_ref[0, 3:4, :])
    logits = lax.dot_general(wr_ref[...], h, _NT, precision=HIGHEST, preferred_element_type=F32)
    scores = jax.nn.sigmoid(logits)
    biased = scores + br_ref[...]
    row = lambda a, e: a[e:e + 1, :]
    best_g = jnp.zeros((1, TM), jnp.int32)
    best_s = None
    for g in range(N_GROUPS):
        v = [row(biased, g * EXPERTS_PER_GROUP + j) for j in range(EXPERTS_PER_GROUP)]
        gs = None
        for a in range(EXPERTS_PER_GROUP):
            for b in range(a + 1, EXPERTS_PER_GROUP):
                pair = v[a] + v[b]
                gs = pair if gs is None else jnp.maximum(gs, pair)
        if best_s is None:
            best_s = gs
        else:
            better = gs > best_s
            best_g = jnp.where(better, g, best_g)
            best_s = jnp.where(better, gs, best_s)
    picks = []
    for _ in range(2):
        top_v = jnp.full((1, TM), -jnp.inf, F32)
        top_i = jnp.full((1, TM), -1, jnp.int32)
        for e in range(N_EXPERTS):
            ok = best_g == (e // EXPERTS_PER_GROUP)
            for p in picks:
                ok = jnp.logical_and(ok, p != e)
            cand = jnp.where(ok, row(biased, e), -jnp.inf)
            better = cand > top_v
            top_i = jnp.where(better, e, top_i)
            top_v = jnp.where(better, cand, top_v)
        picks.append(top_i)
    e_iota = lax.broadcasted_iota(jnp.int32, (N_EXPERTS, TM), 0)
    sel0 = e_iota == picks[0]
    sel1 = e_iota == picks[1]
    w0 = jnp.sum(jnp.where(sel0, scores, 0.0), axis=0, keepdims=True)
    w1 = jnp.sum(jnp.where(sel1, scores, 0.0), axis=0, keepdims=True)
    tot = w0 + w1
    o_ref[...] = jnp.where(sel0, w0 / tot, 0.0) + jnp.where(sel1, w1 / tot, 0.0)


def _router(x, g, mods, w_router_t, b_router, *, n_x_tiles):
    r, d = x.shape
    return pl.pallas_call(
        _router_kernel,
        out_shape=jax.ShapeDtypeStruct((N_EXPERTS, r), F32),
        grid=(r // TM,),
        in_specs=[pl.BlockSpec((TM, d), lambda i: (i, 0)),
                  pl.BlockSpec((1, d), lambda i: (0, 0)),
                  pl.BlockSpec((1, 6, d), lambda i: (i // n_x_tiles, 0, 0)),
                  pl.BlockSpec((N_EXPERTS, d), lambda i: (0, 0)),
                  pl.BlockSpec((N_EXPERTS, 1), lambda i: (0, 0))],
        out_specs=pl.BlockSpec((N_EXPERTS, TM), lambda i: (0, i)),
        compiler_params=_params("parallel"), name="router",
    )(x, g.reshape(1, d), mods, w_router_t, b_router.reshape(N_EXPERTS, 1))


def _moe_dense_kernel(x_ref, g_ref, mod_ref, gate_ref, wg_ref, wu_ref, wd_ref, o_ref, h_scr, acc, *,
                      tm, seq_len):
    i = pl.program_id(0)
    e = pl.program_id(1)
    is_ctx = (i * tm + lax.broadcasted_iota(jnp.int32, (tm, 1), 0)) >= seq_len
    pick = lambda j: jnp.where(is_ctx, mod_ref[1, j:j + 1, :], mod_ref[0, j:j + 1, :])

    @pl.when(e == 0)
    def _():
        h_scr[...] = _norm_mod(x_ref[...], g_ref[...], pick(4), pick(3)).astype(BF16)
        acc[...] = jnp.zeros_like(acc)

    h = h_scr[...]
    a = _silu(jnp.dot(h, wg_ref[0], preferred_element_type=F32)) * jnp.dot(h, wu_ref[0],
                                                                          preferred_element_type=F32)
    y = jnp.dot(a.astype(BF16), wd_ref[0], preferred_element_type=F32)
    acc[...] += gate_ref[0] * y

    @pl.when(e == N_EXPERTS - 1)
    def _():
        o_ref[...] = x_ref[...] + pick(5) * acc[...]


def _moe_dense(x, g, mods, gates_t, wg, wu, wd, *, seq_len):
    r, d = x.shape
    tm = next(t for t in (1280, 1024, 768, 512, 256) if r % t == 0)
    de = wg.shape[-1]
    return pl.pallas_call(
        functools.partial(_moe_dense_kernel, tm=tm, seq_len=seq_len),
        out_shape=jax.ShapeDtypeStruct((r, d), F32),
        grid=(r // tm, N_EXPERTS),
        in_specs=[pl.BlockSpec((tm, d), lambda i, e: (i, 0)),
                  pl.BlockSpec((1, d), lambda i, e: (0, 0)),
                  pl.BlockSpec((2, 6, d), lambda i, e: (0, 0, 0)),
                  pl.BlockSpec((1, tm, 1), lambda i, e: (e, i, 0)),
                  pl.BlockSpec((1, d, de), lambda i, e: (e, 0, 0)),
                  pl.BlockSpec((1, d, de), lambda i, e: (e, 0, 0)),
                  pl.BlockSpec((1, de, d), lambda i, e: (e, 0, 0))],
        out_specs=pl.BlockSpec((tm, d), lambda i, e: (i, 0)),
        scratch_shapes=[pltpu.VMEM((tm, d), BF16), pltpu.VMEM((tm, d), F32)],
        compiler_params=_params("parallel", "arbitrary"), name="moe_dense",
    )(x, g.reshape(1, d), mods, gates_t.reshape(N_EXPERTS, r, 1), wg, wu, wd)


def _final_norm_kernel(x_ref, g_ref, o_ref):
    x = x_ref[...]
    ms = jnp.mean(x * x, axis=-1, keepdims=True)
    o_ref[...] = x * lax.rsqrt(ms + EPS) * g_ref[...]


def _final_norm(x, g, *, seq_len):
    d = x.shape[1]
    return pl.pallas_call(
        _final_norm_kernel,
        out_shape=jax.ShapeDtypeStruct((seq_len, d), F32),
        grid=(seq_len // TM,),
        in_specs=[pl.BlockSpec((TM, d), lambda i: (i, 0)), pl.BlockSpec((1, d), lambda i: (0, 0))],
        out_specs=pl.BlockSpec((TM, d), lambda i: (i, 0)),
        compiler_params=_params("parallel"), name="final_norm",
    )(x, g.reshape(1, d))


def _rope_tables(seq_len, n_rows_total):
    half = HEAD_DIM // 2
    t = jnp.arange(seq_len)
    inv_freq = ROPE_BASE ** (-jnp.arange(0, half, 2, dtype=F32) / half)

    def tab(pos):
        ang = pos.astype(F32)[:, None] * inv_freq
        ang = jnp.concatenate([ang, ang], axis=-1)
        return jnp.cos(ang), jnp.sin(ang)

    cr, sr = tab(t // GRID_W)
    cc, sc = tab(t % GRID_W)
    cos = jnp.concatenate([cr, cc], axis=-1)
    sin = jnp.concatenate([sr, sc], axis=-1)
    first = (jnp.arange(HEAD_DIM) % half) < (half // 2)
    sa = jnp.where(first, -sin, 0.0)
    sb = jnp.where(first, 0.0, sin)
    pad = n_rows_total - seq_len
    wide = lambda a, fill: jnp.tile(jnp.pad(a, ((0, pad), (0, 0)), constant_values=fill), (1, LANES // HEAD_DIM))
    return wide(cos, 1.0), wide(sa, 0.0), wide(sb, 0.0)


def kernel(x, c, ctx, c_ctx, w_mod, b_mod, g_mix, g_ffn, g_final, conv_w_in, conv_w, conv_w_out,
           gla_w_proj, gla_w_gk1, gla_w_gk2, gla_b_gk, gla_g_norm, gla_w_out, swa_w_qkv, swa_sinks,
           swa_w_out, na_w_qkv, na_rpb, na_w_out, router_w, router_b, moe_w_gate, moe_w_up, moe_w_down):
    seq_len, d = x.shape[1], x.shape[2]
    ctx_len = ctx.shape[1]
    assert x.shape[0] == 1 and ctx_len == TM and seq_len % TM == 0 and d == D_MODEL
    assert seq_len % GRID_W == 0 and seq_len // GRID_W >= NA_KH
    n_x_tiles = seq_len // TM
    rows = seq_len + ctx_len
    xs = jnp.concatenate([x[0], ctx[0]], axis=0)
    mods_all = _ada_params(c, c_ctx, w_mod, b_mod)
    w_router_t = router_w.T.astype(F32)
    n_mixers = 4

    for i in range(DEPTH):
        kind, j = i % n_mixers, i // n_mixers
        mods = mods_all[i]
        if kind == 0:
            u = _project(xs, g_mix[i], mods, conv_w_in[j].astype(BF16), n_x_tiles=n_x_tiles)
            xs = _conv_mixer(xs, u, conv_w[j], mods, conv_w_out[j].astype(BF16), n_x_tiles=n_x_tiles)
        elif kind == 1:
            n_extra = LANES
            w_ext = jnp.concatenate(
                [gla_w_proj[j], gla_w_gk1[j, 0], gla_w_gk1[j, 1],
                 jnp.zeros((d, n_extra - 2 * GLA_RANK), F32)], axis=1).astype(BF16)
            u, r_low = _project(xs, g_mix[i], mods, w_ext, n_x_tiles=n_x_tiles, n_extra=n_extra)
            w2pad = [jnp.zeros((n_extra, GLA_DK), F32).at[k * GLA_RANK:(k + 1) * GLA_RANK].set(gla_w_gk2[j, k])
                     for k in range(2)]
            o_f = _gla_pass(u, r_low, w2pad[0], gla_b_gk[j, 0], reverse=False)
            ypre = _gla_pass(u, r_low, w2pad[1], gla_b_gk[j, 1], reverse=True, o_fwd=o_f,
                             g_norm=gla_g_norm[j])
            xs = _out_proj(xs, ypre, mods, gla_w_out[j].astype(BF16), n_x_tiles=n_x_tiles)
        elif kind == 2:
            rope = _rope_tables(seq_len, rows)
            u = _project(xs, g_mix[i], mods, swa_w_qkv[j].astype(BF16), n_x_tiles=n_x_tiles,
                         q_cols=SWA_HEADS * HEAD_DIM, q_scale=HEAD_DIM ** -0.5, rope=rope,
                         rope_cols=(SWA_HEADS + SWA_KV_HEADS) * HEAD_DIM)
            ypre = _swa_attention(u, swa_sinks[j], seq_len=seq_len)
            xs = _out_proj(xs, ypre, mods, swa_w_out[j].astype(BF16), n_x_tiles=n_x_tiles)
        else:
            u = _project(xs, g_mix[i], mods, na_w_qkv[j].astype(BF16), n_x_tiles=n_x_tiles,
                         q_cols=NA_HEADS * HEAD_DIM, q_scale=HEAD_DIM ** -0.5)
            ypre = _na_attention(u, _na_bias_table(na_rpb[j]), n_rows=seq_len // GRID_W)
            xs = _out_proj(xs, ypre, mods, na_w_out[j].astype(BF16), n_x_tiles=n_x_tiles)
        gates_t = _router(xs, g_ffn[i], mods, w_router_t, router_b, n_x_tiles=n_x_tiles)
        xs = _moe_dense(xs, g_ffn[i], mods, gates_t, moe_w_gate[i].astype(BF16), moe_w_up[i].astype(BF16),
                        moe_w_down[i].astype(BF16), seq_len=seq_len)
    return _final_norm(xs, g_final, seq_len=seq_len)[None]
```

```python
import functools

import jax
import jax.numpy as jnp
from jax import lax
from jax.experimental import pallas as pl
from jax.experimental.pallas import tpu as pltpu

F32 = jnp.float32
BF16 = jnp.bfloat16
HIGHEST = lax.Precision.HIGHEST

D_MODEL = 1024
DEPTH = 4
GRID_W = 64
EPS = 1e-6
CONV_W = 3
GLA_HEADS = 4
GLA_DK = D_MODEL // 2
GLA_DV = D_MODEL
GLA_HK = GLA_DK // GLA_HEADS
GLA_HV = GLA_DV // GLA_HEADS
GLA_RANK = 16
GLA_NORMALIZER = 16.0
GLA_CHUNK = 64
HEAD_DIM = 64
SWA_HEADS = D_MODEL // HEAD_DIM
SWA_KV_HEADS = SWA_HEADS // 4
SWA_WINDOW = 128
ROPE_BASE = 10000.0
NA_HEADS = D_MODEL // HEAD_DIM
NA_KH = 8
NA_KW = 16
N_EXPERTS = 16
N_GROUPS = 4
EXPERTS_PER_GROUP = N_EXPERTS // N_GROUPS
D_EXPERT = D_MODEL // 2

TM = 256
MOE_BLOCK = 512
LANES = 128
NEG = -1e30
VMEM_LIMIT = 56 * 1024 * 1024

_NT = (((1,), (1,)), ((), ()))
_TN = (((0,), (0,)), ((), ()))


def _params(*sem):
    return pltpu.CompilerParams(dimension_semantics=sem, vmem_limit_bytes=VMEM_LIMIT)


def _norm_mod(x, g, scale, shift):
    ms = jnp.mean(x * x, axis=-1, keepdims=True)
    return (x * lax.rsqrt(ms + EPS) * g) * (1.0 + scale) + shift


def _silu(x):
    return x * jax.nn.sigmoid(x)


def _ada_kernel(cond_ref, w_ref, b_ref, o_ref):
    s = _silu(cond_ref[...])
    o_ref[0] = jnp.dot(s, w_ref[0], precision=HIGHEST, preferred_element_type=F32) + b_ref[0]


def _ada_params(c, c_ctx, w_mod, b_mod):
    depth, d, n = w_mod.shape
    nc = n // 4
    cond = jnp.zeros((8, d), F32).at[0].set(c[0]).at[1].set(c_ctx)
    out = pl.pallas_call(
        _ada_kernel,
        out_shape=jax.ShapeDtypeStruct((depth, 8, n), F32),
        grid=(depth, n // nc),
        in_specs=[pl.BlockSpec((8, d), lambda l, j: (0, 0)),
                  pl.BlockSpec((1, d, nc), lambda l, j: (l, 0, j)),
                  pl.BlockSpec((1, 1, nc), lambda l, j: (l, 0, j))],
        out_specs=pl.BlockSpec((1, 8, nc), lambda l, j: (l, 0, j)),
        compiler_params=_params("parallel", "parallel"),
        name="ada_params",
    )(cond, w_mod, b_mod.reshape(depth, 1, n))
    return out[:, :2].reshape(depth, 2, 6, d)


def _proj_kernel(x_ref, g_ref, mod_ref, w_ref, *rest, n_chunk, q_cols, q_scale, rope_cols, n_main):
    if rope_cols:
        cos_ref, sa_ref, sb_ref = rest[:3]
        rest = rest[3:]
    o_ref = rest[0]
    h = _norm_mod(x_ref[...], g_ref[...], mod_ref[0, 1:2, :], mod_ref[0, 0:1, :]).astype(BF16)
    for c0 in range(0, n_main, n_chunk):
        c1 = min(c0 + n_chunk, n_main)
        y = jnp.dot(h, w_ref[:, c0:c1], preferred_element_type=F32)
        if c0 < q_cols:
            y = y * q_scale
        if c0 >= rope_cols:
            o_ref[:, c0:c1] = y.astype(o_ref.dtype)
            continue
        for s0 in range(c0, c1, LANES):
            ys = y[:, s0 - c0:s0 - c0 + LANES]
            if s0 < rope_cols:
                ys = (ys * cos_ref[...] + pltpu.roll(ys, LANES - 16, 1) * sa_ref[...]
                      + pltpu.roll(ys, 16, 1) * sb_ref[...])
            o_ref[:, s0:s0 + LANES] = ys.astype(o_ref.dtype)
    if n_main < w_ref.shape[1]:
        rest[1][...] = jnp.dot(h, w_ref[:, n_main:], preferred_element_type=F32)


def _project(x, g, mods, w, *, n_x_tiles, n_chunk=512, q_cols=0, q_scale=1.0, rope=None, rope_cols=0,
             n_extra=0):
    r, d = x.shape
    n_total = w.shape[1]
    n_main = n_total - n_extra
    nt = r // TM
    in_specs = [pl.BlockSpec((TM, d), lambda i: (i, 0)),
                pl.BlockSpec((1, d), lambda i: (0, 0)),
                pl.BlockSpec((1, 6, d), lambda i: (i // n_x_tiles, 0, 0)),
                pl.BlockSpec((d, n_total), lambda i: (0, 0))]
    args = [x, g.reshape(1, d), mods, w]
    if rope_cols:
        in_specs += [pl.BlockSpec((TM, LANES), lambda i: (i, 0))] * 3
        args += list(rope)
    out_shape = [jax.ShapeDtypeStruct((r, n_main), BF16)]
    out_specs = [pl.BlockSpec((TM, n_main), lambda i: (i, 0))]
    if n_extra:
        out_shape.append(jax.ShapeDtypeStruct((r, n_extra), F32))
        out_specs.append(pl.BlockSpec((TM, n_extra), lambda i: (i, 0)))
    out = pl.pallas_call(
        functools.partial(_proj_kernel, n_chunk=n_chunk, q_cols=q_cols, q_scale=q_scale,
                          rope_cols=rope_cols, n_main=n_main),
        out_shape=out_shape, grid=(nt,), in_specs=in_specs, out_specs=out_specs,
        compiler_params=_params("parallel"), name="norm_proj",
    )(*args)
    return out if n_extra else out[0]


def _out_proj_kernel(x_ref, y_ref, mod_ref, w_ref, o_ref):
    y = jnp.dot(y_ref[...], w_ref[...], preferred_element_type=F32)
    o_ref[...] = x_ref[...] + mod_ref[0, 2:3, :] * y


def _out_proj(x, ypre, mods, w, *, n_x_tiles):
    r, d = x.shape
    return pl.pallas_call(
        _out_proj_kernel,
        out_shape=jax.ShapeDtypeStruct((r, d), F32),
        grid=(r // TM,),
        in_specs=[pl.BlockSpec((TM, d), lambda i: (i, 0)),
                  pl.BlockSpec((TM, d), lambda i: (i, 0)),
                  pl.BlockSpec((1, 6, d), lambda i: (i // n_x_tiles, 0, 0)),
                  pl.BlockSpec((d, d), lambda i: (0, 0))],
        out_specs=pl.BlockSpec((TM, d), lambda i: (i, 0)),
        compiler_params=_params("parallel"), name="out_proj",
    )(x, ypre, mods, w)


def _conv_kernel(x_ref, bg_ref, cg_ref, v_ref, cgp_ref, vp_ref, cgn_ref, vn_ref, cw_ref, mod_ref, w_ref,
                 o_ref, *, n_x_tiles):
    i = pl.program_id(0)
    has_prev = jnp.logical_and(i != 0, i != n_x_tiles).astype(F32)
    has_next = jnp.logical_and(i != n_x_tiles - 1, i != n_x_tiles).astype(F32)
    z = cg_ref[...].astype(F32) * v_ref[...].astype(F32)
    z_prev = cgp_ref[7:8, :].astype(F32) * vp_ref[7:8, :].astype(F32) * has_prev
    z_next = cgn_ref[0:1, :].astype(F32) * vn_ref[0:1, :].astype(F32) * has_next
    row = lax.broadcasted_iota(jnp.int32, z.shape, 0)
    z_dn = jnp.where(row == 0, z_prev, pltpu.roll(z, 1, 0))
    z_up = jnp.where(row == TM - 1, z_next, pltpu.roll(z, TM - 1, 0))
    conv = z_dn * cw_ref[0:1, :] + z * cw_ref[1:2, :] + z_up * cw_ref[2:3, :]
    ypre = (bg_ref[...].astype(F32) * conv).astype(BF16)
    y = jnp.dot(ypre, w_ref[...], preferred_element_type=F32)
    o_ref[...] = x_ref[...] + mod_ref[0, 2:3, :] * y


def _conv_mixer(x, u, conv_w, mods, w_out, *, n_x_tiles):
    r, d = x.shape
    n8 = r // 8
    t8 = TM // 8
    tile = lambda c: pl.BlockSpec((TM, d), lambda i: (i, c))
    prev = lambda c: pl.BlockSpec((8, d), lambda i: (jnp.maximum(i * t8 - 1, 0), c))
    nxt = lambda c: pl.BlockSpec((8, d), lambda i: (jnp.minimum((i + 1) * t8, n8 - 1), c))
    return pl.pallas_call(
        functools.partial(_conv_kernel, n_x_tiles=n_x_tiles),
        out_shape=jax.ShapeDtypeStruct((r, d), F32),
        grid=(r // TM,),
        in_specs=[pl.BlockSpec((TM, d), lambda i: (i, 0)), tile(0), tile(1), tile(2),
                  prev(1), prev(2), nxt(1), nxt(2),
                  pl.BlockSpec((CONV_W, d), lambda i: (0, 0)),
                  pl.BlockSpec((1, 6, d), lambda i: (i // n_x_tiles, 0, 0)),
                  pl.BlockSpec((d, d), lambda i: (0, 0))],
        out_specs=pl.BlockSpec((TM, d), lambda i: (i, 0)),
        compiler_params=_params("parallel"), name="conv_mixer",
    )(x, u, u, u, u, u, u, u, conv_w, mods, w_out)


def _log_sigmoid(z):
    return jnp.minimum(z, 0.0) - jnp.log1p(jnp.exp(-jnp.abs(z)))


def _gla_kernel(q_ref, k_ref, v_ref, r_ref, w2_ref, b_ref, *rest, reverse, final):
    if final:
        of_ref, g_ref, gn_ref, o_ref, st_ref = rest
    else:
        o_ref, st_ref = rest
    C = GLA_CHUNK

    @pl.when(pl.program_id(0) == 0)
    def _():
        st_ref[...] = jnp.zeros_like(st_ref)

    z = jnp.dot(r_ref[...], w2_ref[...], precision=HIGHEST, preferred_element_type=F32) + b_ref[...]
    gk = _log_sigmoid(z) * (1.0 / GLA_NORMALIZER)
    ii = lax.broadcasted_iota(jnp.int32, (C, C), 0)
    jj = lax.broadcasted_iota(jnp.int32, (C, C), 1)
    causal = (jj >= ii) if reverse else (jj <= ii)
    tri = causal.astype(F32)
    chunks = range(TM // C)
    for c in (reversed(chunks) if reverse else chunks):
        rows = slice(c * C, (c + 1) * C)
        for h in range(GLA_HEADS):
            kcols = slice(h * GLA_HK, (h + 1) * GLA_HK)
            vcols = slice(h * GLA_HV, (h + 1) * GLA_HV)
            bcum = jnp.dot(tri, gk[rows, kcols], precision=HIGHEST, preferred_element_type=F32)
            total = bcum[0:1, :] if reverse else bcum[C - 1:C, :]
            qh = q_ref[rows, kcols].astype(F32) * (GLA_HK ** -0.5)
            kh = k_ref[rows, kcols].astype(F32)
            vh = v_ref[rows, vcols]
            q_dec = (qh * jnp.exp(bcum)).astype(BF16)
            k_inv = (kh * jnp.exp(-bcum)).astype(BF16)
            k_end = (kh * jnp.exp(total - bcum)).astype(BF16)
            att = lax.dot_general(q_dec, k_inv, _NT, preferred_element_type=F32)
            att = jnp.where(causal, att, 0.0).astype(BF16)
            st = st_ref[h]
            o = (jnp.dot(att, vh, preferred_element_type=F32)
                 + lax.dot_general(q_dec, st.astype(BF16), _NT, preferred_element_type=F32))
            st_ref[h] = st * jnp.exp(total) + lax.dot_general(vh, k_end, _TN, preferred_element_type=F32)
            if final:
                o = o + of_ref[rows, vcols].astype(F32)
                on = o * lax.rsqrt(jnp.mean(o * o, axis=-1, keepdims=True) + EPS) * gn_ref[...]
                o = on * _silu(g_ref[rows, vcols].astype(F32))
            o_ref[rows, vcols] = o.astype(o_ref.dtype)


def _gla_pass(u, r_low, w2pad, b_gk, *, reverse, o_fwd=None, g_norm=None):
    rws = u.shape[0]
    nt = rws // TM
    final = o_fwd is not None
    if reverse:
        order = lambda s: nt - 1 - s
    else:
        order = lambda s: (s + nt - 1) % nt
    in_specs = [pl.BlockSpec((TM, GLA_DK), lambda s: (order(s), 0)),
                pl.BlockSpec((TM, GLA_DK), lambda s: (order(s), 1)),
                pl.BlockSpec((TM, GLA_DV), lambda s: (order(s), 1)),
                pl.BlockSpec((TM, LANES), lambda s: (order(s), 0)),
                pl.BlockSpec((LANES, GLA_DK), lambda s: (0, 0)),
                pl.BlockSpec((1, GLA_DK), lambda s: (0, 0))]
    args = [u, u, u, r_low, w2pad, b_gk.reshape(1, GLA_DK)]
    if final:
        in_specs += [pl.BlockSpec((TM, GLA_DV), lambda s: (order(s), 0)),
                     pl.BlockSpec((TM, GLA_DV), lambda s: (order(s), 2)),
                     pl.BlockSpec((1, GLA_HV), lambda s: (0, 0))]
        args += [o_fwd, u, g_norm.reshape(1, GLA_HV)]
    return pl.pallas_call(
        functools.partial(_gla_kernel, reverse=reverse, final=final),
        out_shape=jax.ShapeDtypeStruct((rws, GLA_DV), BF16),
        grid=(nt,), in_specs=in_specs,
        out_specs=pl.BlockSpec((TM, GLA_DV), lambda s: (order(s), 0)),
        scratch_shapes=[pltpu.VMEM((GLA_HEADS, GLA_HV, GLA_HK), F32)],
        compiler_params=_params("arbitrary"), name="gla_bwd" if reverse else "gla_fwd",
    )(*args)


def _swa_kernel(sink_ref, q_ref, kvp_ref, kvo_ref, kvn_ref, kvc_ref, o_ref, kv_buf, *, seq_len):
    i = pl.program_id(0)
    half = TM // 2
    kv_buf[0:half] = kvp_ref[...]
    kv_buf[half:half + TM] = kvo_ref[...]
    kv_buf[half + TM:2 * TM] = kvn_ref[...]
    kv_buf[2 * TM:3 * TM] = kvc_ref[...]
    nk = 3 * TM
    qpos = i * TM + lax.broadcasted_iota(jnp.int32, (TM, nk), 0)
    col = lax.broadcasted_iota(jnp.int32, (TM, nk), 1)
    kpos = i * TM - half + col
    local = ((jnp.abs(kpos - qpos) <= SWA_WINDOW) & (kpos >= 0) & (kpos < seq_len) & (qpos < seq_len))
    valid = local | (col >= 2 * TM)
    kv_w = SWA_KV_HEADS * HEAD_DIM
    group = SWA_HEADS // SWA_KV_HEADS
    for h in range(SWA_HEADS):
        kv = h // group
        qh = q_ref[:, h * HEAD_DIM:(h + 1) * HEAD_DIM]
        kh = kv_buf[:, kv * HEAD_DIM:(kv + 1) * HEAD_DIM]
        vh = kv_buf[:, kv_w + kv * HEAD_DIM:kv_w + (kv + 1) * HEAD_DIM]
        s = lax.dot_general(qh, kh, _NT, preferred_element_type=F32)
        s = jnp.where(valid, s, NEG)
        sink = sink_ref[h]
        m = jnp.maximum(jnp.max(s, axis=-1, keepdims=True), sink)
        p = jnp.exp(s - m)
        l = jnp.sum(p, axis=-1, keepdims=True) + jnp.exp(sink - m)
        o = jnp.dot(p.astype(BF16), vh, preferred_element_type=F32) / l
        o_ref[:, h * HEAD_DIM:(h + 1) * HEAD_DIM] = o.astype(o_ref.dtype)


def _swa_attention(u, sinks, *, seq_len):
    rws = u.shape[0]
    nt = rws // TM
    half = TM // 2
    n_half = rws // half
    kv_w = 2 * SWA_KV_HEADS * HEAD_DIM
    qw = SWA_HEADS * HEAD_DIM
    kvc = qw // kv_w
    return pl.pallas_call(
        functools.partial(_swa_kernel, seq_len=seq_len),
        out_shape=jax.ShapeDtypeStruct((rws, qw), BF16),
        grid=(nt,),
        in_specs=[pl.BlockSpec(memory_space=pltpu.SMEM),
                  pl.BlockSpec((TM, qw), lambda i: (i, 0)),
                  pl.BlockSpec((half, kv_w), lambda i: (jnp.maximum(2 * i - 1, 0), kvc)),
                  pl.BlockSpec((TM, kv_w), lambda i: (i, kvc)),
                  pl.BlockSpec((half, kv_w), lambda i: (jnp.minimum(2 * i + 2, n_half - 1), kvc)),
                  pl.BlockSpec((TM, kv_w), lambda i: (nt - 1, kvc))],
        out_specs=pl.BlockSpec((TM, qw), lambda i: (i, 0)),
        scratch_shapes=[pltpu.VMEM((3 * TM, kv_w), BF16)],
        compiler_params=_params("parallel"), name="swa_attention",
    )(sinks, u, u, u, u, u)


def _na_kernel(q_ref, kp_ref, ko_ref, kn_ref, kc_ref, vp_ref, vo_ref, vn_ref, vc_ref, bias_ref, o_ref,
               k_buf, v_buf, *, n_rows):
    i = pl.program_id(0)
    for j, (kr, vr) in enumerate(((kp_ref, vp_ref), (ko_ref, vo_ref), (kn_ref, vn_ref), (kc_ref, vc_ref))):
        k_buf[j * TM:(j + 1) * TM] = kr[...]
        v_buf[j * TM:(j + 1) * TM] = vr[...]
    n_loc = 3 * TM
    rpt = TM // GRID_W
    qi = lax.broadcasted_iota(jnp.int32, (TM, n_loc), 0)
    ki = lax.broadcasted_iota(jnp.int32, (TM, n_loc), 1)
    r = i * rpt + qi // GRID_W
    c = qi % GRID_W
    krow = (i - 1) * rpt + ki // GRID_W
    kcol = ki % GRID_W
    r0 = jnp.clip(r - NA_KH // 2, 0, n_rows - NA_KH)
    c0 = jnp.clip(c - NA_KW // 2, 0, GRID_W - NA_KW)
    valid = ((krow >= r0) & (krow < r0 + NA_KH) & (kcol >= c0) & (kcol < c0 + NA_KW) & (r < n_rows))
    for h in range(NA_HEADS):
        cols = slice(h * HEAD_DIM, (h + 1) * HEAD_DIM)
        qh = q_ref[:, cols]
        s = lax.dot_general(qh, k_buf[:, cols], _NT, preferred_element_type=F32)
        s_loc = jnp.where(valid, s[:, :n_loc] + bias_ref[h], NEG)
        s_ctx = s[:, n_loc:]
        m = jnp.maximum(jnp.max(s_loc, axis=-1, keepdims=True), jnp.max(s_ctx, axis=-1, keepdims=True))
        p_loc = jnp.exp(s_loc - m)
        p_ctx = jnp.exp(s_ctx - m)
        l = jnp.sum(p_loc, axis=-1, keepdims=True) + jnp.sum(p_ctx, axis=-1, keepdims=True)
        o = (jnp.dot(p_loc.astype(BF16), v_buf[0:n_loc, cols], preferred_element_type=F32)
             + jnp.dot(p_ctx.astype(BF16), v_buf[n_loc:, cols], preferred_element_type=F32)) / l
        o_ref[:, cols] = o.astype(o_ref.dtype)


def _na_bias_table(rpb):
    rpt = TM // GRID_W
    lo = NA_KH - 1 - rpt
    assert lo - (rpt - 1) >= 0 and lo + 3 * rpt <= 2 * NA_KH - 1
    c = jnp.arange(GRID_W)
    col_off = c[None, :] - c[:, None] + NA_KW - 1
    onehot = (col_off[None] == jnp.arange(2 * NA_KW - 1)[:, None, None]).astype(F32)
    tcol = jnp.einsum('hrj,jck->hrck', rpb.astype(F32), onehot, precision=HIGHEST)
    t = jnp.stack([lax.slice_in_dim(tcol, lo - a, lo - a + 3 * rpt, axis=1) for a in range(rpt)], axis=1)
    return t.transpose(0, 1, 3, 2, 4).reshape(rpb.shape[0], TM, 3 * TM)


def _na_attention(u, bias, *, n_rows):
    rws = u.shape[0]
    nt = rws // TM
    d = NA_HEADS * HEAD_DIM
    blk = lambda f, c: pl.BlockSpec((TM, d), lambda i: (f(i), c))
    prev = lambda i: jnp.maximum(i - 1, 0)
    own = lambda i: i
    nxt = lambda i: jnp.minimum(i + 1, nt - 1)
    ctx = lambda i: nt - 1
    return pl.pallas_call(
        functools.partial(_na_kernel, n_rows=n_rows),
        out_shape=jax.ShapeDtypeStruct((rws, d), BF16),
        grid=(nt,),
        in_specs=[blk(own, 0), blk(prev, 1), blk(own, 1), blk(nxt, 1), blk(ctx, 1),
                  blk(prev, 2), blk(own, 2), blk(nxt, 2), blk(ctx, 2),
                  pl.BlockSpec((NA_HEADS, TM, 3 * TM), lambda i: (0, 0, 0))],
        out_specs=pl.BlockSpec((TM, d), lambda i: (i, 0)),
        scratch_shapes=[pltpu.VMEM((4 * TM, d), BF16), pltpu.VMEM((4 * TM, d), BF16)],
        compiler_params=_params("parallel"), name="na_attention",
    )(u, u, u, u, u, u, u, u, u, bias)


def _router_kernel(x_ref, g_ref, mod_ref, wr_ref, br_ref, e_ref, w_ref, rank_ref, cnt_ref, carry):
    @pl.when(pl.program_id(0) == 0)
    def _():
        carry[...] = jnp.zeros_like(carry)

    h = _norm_mod(x_ref[...], g_ref[...], mod_ref[0, 4:5, :], mod_ref[0, 3:4, :])
    logits = lax.dot_general(wr_ref[...], h, _NT, precision=HIGHEST, preferred_element_type=F32)
    scores = jax.nn.sigmoid(logits)
    biased = scores + br_ref[...]
    row = lambda a, e: a[e:e + 1, :]
    best_g = jnp.zeros((1, TM), jnp.int32)
    best_s = None
    for g in range(N_GROUPS):
        v = [row(biased, g * EXPERTS_PER_GROUP + j) for j in range(EXPERTS_PER_GROUP)]
        gs = None
        for a in range(EXPERTS_PER_GROUP):
            for b in range(a + 1, EXPERTS_PER_GROUP):
                pair = v[a] + v[b]
                gs = pair if gs is None else jnp.maximum(gs, pair)
        if best_s is None:
            best_s = gs
        else:
            better = gs > best_s
            best_g = jnp.where(better, g, best_g)
            best_s = jnp.where(better, gs, best_s)
    picks = []
    for _ in range(2):
        top_v = jnp.full((1, TM), -jnp.inf, F32)
        top_i = jnp.full((1, TM), -1, jnp.int32)
        for e in range(N_EXPERTS):
            ok = best_g == (e // EXPERTS_PER_GROUP)
            for p in picks:
                ok = jnp.logical_and(ok, p != e)
            cand = jnp.where(ok, row(biased, e), -jnp.inf)
            better = cand > top_v
            top_i = jnp.where(better, e, top_i)
            top_v = jnp.where(better, cand, top_v)
        picks.append(top_i)
    e_iota = lax.broadcasted_iota(jnp.int32, (N_EXPERTS, TM), 0)
    sel0 = e_iota == picks[0]
    sel1 = e_iota == picks[1]
    w0 = jnp.sum(jnp.where(sel0, scores, 0.0), axis=0, keepdims=True)
    w1 = jnp.sum(jnp.where(sel1, scores, 0.0), axis=0, keepdims=True)
    tot = w0 + w1
    e_ref[0:1, :] = picks[0]
    e_ref[1:2, :] = picks[1]
    w_ref[0:1, :] = w0 / tot
    w_ref[1:2, :] = w1 / tot
    member = jnp.where(jnp.logical_or(sel0, sel1), 1.0, 0.0)
    earlier = (lax.broadcasted_iota(jnp.int32, (TM, TM), 0)
               < lax.broadcasted_iota(jnp.int32, (TM, TM), 1)).astype(BF16)
    before = carry[:, 0:1] + jnp.dot(member.astype(BF16), earlier, preferred_element_type=F32)
    rank_ref[0:1, :] = jnp.sum(jnp.where(sel0, before, 0.0), axis=0, keepdims=True).astype(jnp.int32)
    rank_ref[1:2, :] = jnp.sum(jnp.where(sel1, before, 0.0), axis=0, keepdims=True).astype(jnp.int32)
    carry[...] = carry[...] + jnp.sum(member, axis=1, keepdims=True)
    cnt_ref[...] = carry[...].astype(jnp.int32)


def _router(x, g, mods, w_router_t, b_router, *, n_x_tiles):
    r, d = x.shape
    pair = pl.BlockSpec((2, TM), lambda i: (0, i))
    return pl.pallas_call(
        _router_kernel,
        out_shape=[jax.ShapeDtypeStruct((2, r), jnp.int32), jax.ShapeDtypeStruct((2, r), F32),
                   jax.ShapeDtypeStruct((2, r), jnp.int32), jax.ShapeDtypeStruct((N_EXPERTS, LANES), jnp.int32)],
        grid=(r // TM,),
        in_specs=[pl.BlockSpec((TM, d), lambda i: (i, 0)),
                  pl.BlockSpec((1, d), lambda i: (0, 0)),
                  pl.BlockSpec((1, 6, d), lambda i: (i // n_x_tiles, 0, 0)),
                  pl.BlockSpec((N_EXPERTS, d), lambda i: (0, 0)),
                  pl.BlockSpec((N_EXPERTS, 1), lambda i: (0, 0))],
        out_specs=[pair, pair, pair, pl.BlockSpec((N_EXPERTS, LANES), lambda i: (0, 0))],
        scratch_shapes=[pltpu.VMEM((N_EXPERTS, LANES), F32)],
        compiler_params=_params("arbitrary"), name="router",
    )(x, g.reshape(1, d), mods, w_router_t, b_router.reshape(N_EXPERTS, 1))


def _dispatch_plan(e01, rank01, counts, n_blocks):
    cnt = counts[:, 0]
    nblk = (cnt + MOE_BLOCK - 1) // MOE_BLOCK
    blk_end = jnp.cumsum(nblk)
    off = (blk_end - nblk) * MOE_BLOCK
    eidx = jnp.arange(N_EXPERTS, dtype=jnp.int32)[:, None, None]
    pos = rank01 + jnp.sum(jnp.where(e01[None] == eidx, off[:, None, None], 0), axis=0)
    blk_e = jnp.sum(jnp.arange(n_blocks, dtype=jnp.int32)[:, None] >= blk_end[None, :], axis=1)
    blk_e = jnp.minimum(blk_e, N_EXPERTS - 1).astype(jnp.int32)
    return pos.reshape(-1).astype(jnp.int32), blk_e, blk_end[-1:].astype(jnp.int32)


def _dispatch_kernel(pos_ref, x_ref, g_ref, mod_ref, xs0_ref, xs_ref, h_scr, sem, *, rows, nt):
    del xs0_ref
    i = pl.program_id(0)
    slot = i % 2

    def wait_slot(s):
        for _ in range(2):
            pltpu.make_async_copy(h_scr.at[s], xs_ref.at[pl.ds(0, TM)], sem.at[s]).wait()

    @pl.when(i >= 2)
    def _():
        wait_slot(slot)

    h_scr[slot] = _norm_mod(x_ref[...], g_ref[...], mod_ref[0, 4:5, :], mod_ref[0, 3:4, :])
    base = i * TM

    def body(t, carry):
        src = h_scr.at[slot, pl.ds(t, 1)]
        pltpu.make_async_copy(src, xs_ref.at[pl.ds(pos_ref[base + t], 1)], sem.at[slot]).start()
        pltpu.make_async_copy(src, xs_ref.at[pl.ds(pos_ref[rows + base + t], 1)], sem.at[slot]).start()
        return carry

    lax.fori_loop(0, TM, body, 0, unroll=8)

    @pl.when(i == nt - 1)
    def _():
        wait_slot(slot)
        if nt > 1:
            wait_slot(1 - slot)


def _dispatch(pos, x, g, mods, n_blocks, *, n_x_tiles):
    r, d = x.shape
    nt = r // TM
    zeros = jnp.zeros((n_blocks * MOE_BLOCK, d), F32)
    return pl.pallas_call(
        functools.partial(_dispatch_kernel, rows=r, nt=nt),
        out_shape=jax.ShapeDtypeStruct(zeros.shape, F32),
        grid_spec=pltpu.PrefetchScalarGridSpec(
            num_scalar_prefetch=1, grid=(nt,),
            in_specs=[pl.BlockSpec((TM, d), lambda i, p: (i, 0)),
                      pl.BlockSpec((1, d), lambda i, p: (0, 0)),
                      pl.BlockSpec((1, 6, d), lambda i, p: (i // n_x_tiles, 0, 0)),
                      pl.BlockSpec(memory_space=pl.ANY)],
            out_specs=pl.BlockSpec(memory_space=pl.ANY),
            scratch_shapes=[pltpu.VMEM((2, TM, d), F32), pltpu.SemaphoreType.DMA((2,))]),
        input_output_aliases={4: 0},
        compiler_params=_params("arbitrary"), name="moe_dispatch",
    )(pos, x, g.reshape(1, d), mods, zeros)


def _expert_kernel(be_ref, nu_ref, xs_ref, wg_ref, wu_ref, wd_ref, ys_ref, wg_s, wu_s, wd_s):
    j = pl.program_id(0)

    @pl.when(j < nu_ref[0])
    def _():
        @pl.when(jnp.logical_or(j == 0, be_ref[j] != be_ref[jnp.maximum(j - 1, 0)]))
        def _():
            wg_s[...] = wg_ref[0].astype(BF16)
            wu_s[...] = wu_ref[0].astype(BF16)
            wd_s[...] = wd_ref[0].astype(BF16)

        h = xs_ref[...].astype(BF16)
        a = (_silu(jnp.dot(h, wg_s[...], preferred_element_type=F32))
             * jnp.dot(h, wu_s[...], preferred_element_type=F32))
        ys_ref[...] = jnp.dot(a.astype(BF16), wd_s[...], preferred_element_type=F32)

    @pl.when(j >= nu_ref[0])
    def _():
        ys_ref[...] = jnp.zeros_like(ys_ref)


def _experts(blk_e, n_used, xs, wg, wu, wd):
    n_rows, d = xs.shape
    de = wg.shape[-1]
    last = lambda j, nu: jnp.minimum(j, nu[0] - 1)
    return pl.pallas_call(
        _expert_kernel,
        out_shape=jax.ShapeDtypeStruct((n_rows, d), F32),
        grid_spec=pltpu.PrefetchScalarGridSpec(
            num_scalar_prefetch=2, grid=(n_rows // MOE_BLOCK,),
            in_specs=[pl.BlockSpec((MOE_BLOCK, d), lambda j, be, nu: (last(j, nu), 0)),
                      pl.BlockSpec((1, d, de), lambda j, be, nu: (be[last(j, nu)], 0, 0)),
                      pl.BlockSpec((1, d, de), lambda j, be, nu: (be[last(j, nu)], 0, 0)),
                      pl.BlockSpec((1, de, d), lambda j, be, nu: (be[last(j, nu)], 0, 0))],
            out_specs=pl.BlockSpec((MOE_BLOCK, d), lambda j, be, nu: (j, 0)),
            scratch_shapes=[pltpu.VMEM((d, de), BF16), pltpu.VMEM((d, de), BF16), pltpu.VMEM((de, d), BF16)]),
        compiler_params=_params("arbitrary"), name="moe_experts",
    )(blk_e, n_used, xs, wg, wu, wd)


def _combine_kernel(pos_ref, x_ref, mod_ref, w_ref, ys_ref, o_ref, buf, sem, *, rows, nt):
    i = pl.program_id(0)
    slot = i % 2

    def issue(tile, s):
        base = tile * TM

        def body(t, carry):
            pltpu.make_async_copy(ys_ref.at[pl.ds(pos_ref[base + t], 1)], buf.at[s, 0, pl.ds(t, 1)],
                                  sem.at[s]).start()
            pltpu.make_async_copy(ys_ref.at[pl.ds(pos_ref[rows + base + t], 1)], buf.at[s, 1, pl.ds(t, 1)],
                                  sem.at[s]).start()
            return carry

        lax.fori_loop(0, TM, body, 0, unroll=8)

    @pl.when(i == 0)
    def _():
        issue(0, 0)

    @pl.when(i + 1 < nt)
    def _():
        issue(i + 1, 1 - slot)

    for k in range(2):
        pltpu.make_async_copy(ys_ref.at[pl.ds(0, TM)], buf.at[slot, k], sem.at[slot]).wait()
    w = w_ref[...]
    y = w[:, 0:1] * buf[slot, 0] + w[:, 1:2] * buf[slot, 1]
    o_ref[...] = x_ref[...] + mod_ref[0, 5:6, :] * y


def _combine(pos, x, mods, w_cols, ys, *, n_x_tiles):
    r, d = x.shape
    nt = r // TM
    return pl.pallas_call(
        functools.partial(_combine_kernel, rows=r, nt=nt),
        out_shape=jax.ShapeDtypeStruct((r, d), F32),
        grid_spec=pltpu.PrefetchScalarGridSpec(
            num_scalar_prefetch=1, grid=(nt,),
            in_specs=[pl.BlockSpec((TM, d), lambda i, p: (i, 0)),
                      pl.BlockSpec((1, 6, d), lambda i, p: (i // n_x_tiles, 0, 0)),
                      pl.BlockSpec((TM, 2), lambda i, p: (i, 0)),
                      pl.BlockSpec(memory_space=pl.ANY)],
            out_specs=pl.BlockSpec((TM, d), lambda i, p: (i, 0)),
            scratch_shapes=[pltpu.VMEM((2, 2, TM, d), F32), pltpu.SemaphoreType.DMA((2,))]),
        compiler_params=_params("arbitrary"), name="moe_combine",
    )(pos, x, mods, w_cols, ys)


def _moe(x, g, mods, w_router_t, b_router, wg, wu, wd, *, n_x_tiles):
    r = x.shape[0]
    n_blocks = -(-2 * r // MOE_BLOCK) + N_EXPERTS
    e01, w01, rank01, counts = _router(x, g, mods, w_router_t, b_router, n_x_tiles=n_x_tiles)
    pos, blk_e, n_used = _dispatch_plan(e01, rank01, counts, n_blocks)
    xs = _dispatch(pos, x, g, mods, n_blocks, n_x_tiles=n_x_tiles)
    ys = _experts(blk_e, n_used, xs, wg, wu, wd)
    return _combine(pos, x, mods, w01.T, ys, n_x_tiles=n_x_tiles)


def _final_norm_kernel(x_ref, g_ref, o_ref):
    x = x_ref[...]
    ms = jnp.mean(x * x, axis=-1, keepdims=True)
    o_ref[...] = x * lax.rsqrt(ms + EPS) * g_ref[...]


def _final_norm(x, g, *, seq_len):
    d = x.shape[1]
    return pl.pallas_call(
        _final_norm_kernel,
        out_shape=jax.ShapeDtypeStruct((seq_len, d), F32),
        grid=(seq_len // TM,),
        in_specs=[pl.BlockSpec((TM, d), lambda i: (i, 0)), pl.BlockSpec((1, d), lambda i: (0, 0))],
        out_specs=pl.BlockSpec((TM, d), lambda i: (i, 0)),
        compiler_params=_params("parallel"), name="final_norm",
    )(x, g.reshape(1, d))


def _rope_tables(seq_len, n_rows_total):
    half = HEAD_DIM // 2
    t = jnp.arange(seq_len)
    inv_freq = ROPE_BASE ** (-jnp.arange(0, half, 2, dtype=F32) / half)

    def tab(pos):
        ang = pos.astype(F32)[:, None] * inv_freq
        ang = jnp.concatenate([ang, ang], axis=-1)
        return jnp.cos(ang), jnp.sin(ang)

    cr, sr = tab(t // GRID_W)
    cc, sc = tab(t % GRID_W)
    cos = jnp.concatenate([cr, cc], axis=-1)
    sin = jnp.concatenate([sr, sc], axis=-1)
    first = (jnp.arange(HEAD_DIM) % half) < (half // 2)
    sa = jnp.where(first, -sin, 0.0)
    sb = jnp.where(first, 0.0, sin)
    pad = n_rows_total - seq_len
    wide = lambda a, fill: jnp.tile(jnp.pad(a, ((0, pad), (0, 0)), constant_values=fill), (1, LANES // HEAD_DIM))
    return wide(cos, 1.0), wide(sa, 0.0), wide(sb, 0.0)


def kernel(x, c, ctx, c_ctx, w_mod, b_mod, g_mix, g_ffn, g_final, conv_w_in, conv_w, conv_w_out,
           gla_w_proj, gla_w_gk1, gla_w_gk2, gla_b_gk, gla_g_norm, gla_w_out, swa_w_qkv, swa_sinks,
           swa_w_out, na_w_qkv, na_rpb, na_w_out, router_w, router_b, moe_w_gate, moe_w_up, moe_w_down):
    seq_len, d = x.shape[1], x.shape[2]
    ctx_len = ctx.shape[1]
    assert x.shape[0] == 1 and ctx_len == TM and seq_len % TM == 0 and d == D_MODEL
    assert seq_len % GRID_W == 0 and seq_len // GRID_W >= NA_KH
    n_x_tiles = seq_len // TM
    rows = seq_len + ctx_len
    xs = jnp.concatenate([x[0], ctx[0]], axis=0)
    mods_all = _ada_params(c, c_ctx, w_mod, b_mod)
    w_router_t = router_w.T.astype(F32)
    n_mixers = 4

    for i in range(DEPTH):
        kind, j = i % n_mixers, i // n_mixers
        mods = mods_all[i]
        if kind == 0:
            u = _project(xs, g_mix[i], mods, conv_w_in[j].astype(BF16), n_x_tiles=n_x_tiles)
            xs = _conv_mixer(xs, u, conv_w[j], mods, conv_w_out[j].astype(BF16), n_x_tiles=n_x_tiles)
        elif kind == 1:
            n_extra = LANES
            w_ext = jnp.concatenate(
                [gla_w_proj[j], gla_w_gk1[j, 0], gla_w_gk1[j, 1],
                 jnp.zeros((d, n_extra - 2 * GLA_RANK), F32)], axis=1).astype(BF16)
            u, r_low = _project(xs, g_mix[i], mods, w_ext, n_x_tiles=n_x_tiles, n_extra=n_extra)
            w2pad = [jnp.zeros((n_extra, GLA_DK), F32).at[k * GLA_RANK:(k + 1) * GLA_RANK].set(gla_w_gk2[j, k])
                     for k in range(2)]
            o_f = _gla_pass(u, r_low, w2pad[0], gla_b_gk[j, 0], reverse=False)
            ypre = _gla_pass(u, r_low, w2pad[1], gla_b_gk[j, 1], reverse=True, o_fwd=o_f,
                             g_norm=gla_g_norm[j])
            xs = _out_proj(xs, ypre, mods, gla_w_out[j].astype(BF16), n_x_tiles=n_x_tiles)
        elif kind == 2:
            rope = _rope_tables(seq_len, rows)
            u = _project(xs, g_mix[i], mods, swa_w_qkv[j].astype(BF16), n_x_tiles=n_x_tiles,
                         q_cols=SWA_HEADS * HEAD_DIM, q_scale=HEAD_DIM ** -0.5, rope=rope,
                         rope_cols=(SWA_HEADS + SWA_KV_HEADS) * HEAD_DIM)
            ypre = _swa_attention(u, swa_sinks[j], seq_len=seq_len)
            xs = _out_proj(xs, ypre, mods, swa_w_out[j].astype(BF16), n_x_tiles=n_x_tiles)
        else:
            u = _project(xs, g_mix[i], mods, na_w_qkv[j].astype(BF16), n_x_tiles=n_x_tiles,
                         q_cols=NA_HEADS * HEAD_DIM, q_scale=HEAD_DIM ** -0.5)
            ypre = _na_attention(u, _na_bias_table(na_rpb[j]), n_rows=seq_len // GRID_W)
            xs = _out_proj(xs, ypre, mods, na_w_out[j].astype(BF16), n_x_tiles=n_x_tiles)
        xs = _moe(xs, g_ffn[i], mods, w_router_t, router_b, moe_w_gate[i], moe_w_up[i], moe_w_down[i],
                  n_x_tiles=n_x_tiles)
    return _final_norm(xs, g_final, seq_len=seq_len)[None]
```

```python
import functools

import jax
import jax.numpy as jnp
from jax import lax
from jax.experimental import pallas as pl
from jax.experimental.pallas import tpu as pltpu

F32 = jnp.float32
BF16 = jnp.bfloat16
HIGHEST = lax.Precision.HIGHEST

D_MODEL = 1024
DEPTH = 4
GRID_W = 64
EPS = 1e-6
CONV_W = 3
GLA_HEADS = 4
GLA_DK = D_MODEL // 2
GLA_DV = D_MODEL
GLA_HK = GLA_DK // GLA_HEADS
GLA_HV = GLA_DV // GLA_HEADS
GLA_RANK = 16
GLA_NORMALIZER = 16.0
GLA_CHUNK = 64
HEAD_DIM = 64
SWA_HEADS = D_MODEL // HEAD_DIM
SWA_KV_HEADS = SWA_HEADS // 4
SWA_WINDOW = 128
ROPE_BASE = 10000.0
NA_HEADS = D_MODEL // HEAD_DIM
NA_KH = 8
NA_KW = 16
N_EXPERTS = 16
N_GROUPS = 4
EXPERTS_PER_GROUP = N_EXPERTS // N_GROUPS
D_EXPERT = D_MODEL // 2

TM = 256
MOE_BLOCK = 512
LANES = 128
NEG = -1e30
LOG2E = 1.4426950408889634
VMEM_LIMIT = 56 * 1024 * 1024

_NT = (((1,), (1,)), ((), ()))
_TN = (((0,), (0,)), ((), ()))


def _params(*sem):
    return pltpu.CompilerParams(dimension_semantics=sem, vmem_limit_bytes=VMEM_LIMIT)


def _norm_mod(x, g, scale, shift):
    ms = jnp.mean(x * x, axis=-1, keepdims=True)
    return (x * lax.rsqrt(ms + EPS) * g) * (1.0 + scale) + shift


def _silu(x):
    return x * jax.nn.sigmoid(x)


def _ada_kernel(cond_ref, w_ref, b_ref, o_ref):
    s = _silu(cond_ref[...])
    o_ref[0] = jnp.dot(s, w_ref[0], precision=HIGHEST, preferred_element_type=F32) + b_ref[0]


def _ada_params(c, c_ctx, w_mod, b_mod):
    depth, d, n = w_mod.shape
    nc = n // 4
    cond = jnp.zeros((8, d), F32).at[0].set(c[0]).at[1].set(c_ctx)
    out = pl.pallas_call(
        _ada_kernel,
        out_shape=jax.ShapeDtypeStruct((depth, 8, n), F32),
        grid=(depth, n // nc),
        in_specs=[pl.BlockSpec((8, d), lambda l, j: (0, 0)),
                  pl.BlockSpec((1, d, nc), lambda l, j: (l, 0, j)),
                  pl.BlockSpec((1, 1, nc), lambda l, j: (l, 0, j))],
        out_specs=pl.BlockSpec((1, 8, nc), lambda l, j: (l, 0, j)),
        compiler_params=_params("parallel", "parallel"),
        name="ada_params",
    )(cond, w_mod, b_mod.reshape(depth, 1, n))
    return out[:, :2].reshape(depth, 2, 6, d)


def _proj_kernel(x_ref, g_ref, mod_ref, w_ref, *rest, n_chunk, q_cols, q_scale, rope_cols, n_main):
    if rope_cols:
        row_ref, col_ref = rest[:2]
        rest = rest[2:]
        rpt = TM // GRID_W
        cos_t, sa_t, sb_t = [
            jnp.concatenate([jnp.broadcast_to(row_ref[a, k:k + 1, :], (GRID_W, LANES)) for a in range(rpt)],
                            axis=0) + col_ref[0, k]
            for k in range(3)]
    o_ref = rest[0]
    h = _norm_mod(x_ref[...], g_ref[...], mod_ref[0, 1:2, :], mod_ref[0, 0:1, :]).astype(BF16)
    for c0 in range(0, n_main, n_chunk):
        c1 = min(c0 + n_chunk, n_main)
        y = jnp.dot(h, w_ref[:, c0:c1], preferred_element_type=F32)
        if c0 < q_cols:
            y = y * q_scale
        if c0 >= rope_cols:
            o_ref[:, c0:c1] = y.astype(o_ref.dtype)
            continue
        for s0 in range(c0, c1, LANES):
            ys = y[:, s0 - c0:s0 - c0 + LANES]
            if s0 < rope_cols:
                ys = ys * cos_t + pltpu.roll(ys, LANES - 16, 1) * sa_t + pltpu.roll(ys, 16, 1) * sb_t
            o_ref[:, s0:s0 + LANES] = ys.astype(o_ref.dtype)
    if n_main < w_ref.shape[1]:
        rest[1][...] = jnp.dot(h, w_ref[:, n_main:], preferred_element_type=F32)


def _project(x, g, mods, w, *, n_x_tiles, n_chunk=512, q_cols=0, q_scale=1.0, rope=None, rope_cols=0,
             n_extra=0):
    r, d = x.shape
    n_total = w.shape[1]
    n_main = n_total - n_extra
    nt = r // TM
    in_specs = [pl.BlockSpec((TM, d), lambda i: (i, 0)),
                pl.BlockSpec((1, d), lambda i: (0, 0)),
                pl.BlockSpec((1, 6, d), lambda i: (i // n_x_tiles, 0, 0)),
                pl.BlockSpec((d, n_total), lambda i: (0, 0))]
    args = [x, g.reshape(1, d), mods, w]
    if rope_cols:
        rpt = TM // GRID_W
        in_specs += [pl.BlockSpec((rpt, 3, LANES), lambda i: (i, 0, 0)),
                     pl.BlockSpec((1, 3, TM, LANES), lambda i: (i // n_x_tiles, 0, 0, 0))]
        args += list(rope)
    out_shape = [jax.ShapeDtypeStruct((r, n_main), BF16)]
    out_specs = [pl.BlockSpec((TM, n_main), lambda i: (i, 0))]
    if n_extra:
        out_shape.append(jax.ShapeDtypeStruct((r, n_extra), F32))
        out_specs.append(pl.BlockSpec((TM, n_extra), lambda i: (i, 0)))
    out = pl.pallas_call(
        functools.partial(_proj_kernel, n_chunk=n_chunk, q_cols=q_cols, q_scale=q_scale,
                          rope_cols=rope_cols, n_main=n_main),
        out_shape=out_shape, grid=(nt,), in_specs=in_specs, out_specs=out_specs,
        compiler_params=_params("parallel"), name="norm_proj",
    )(*args)
    return out if n_extra else out[0]


def _out_proj_kernel(x_ref, y_ref, mod_ref, w_ref, o_ref):
    y = jnp.dot(y_ref[...], w_ref[...], preferred_element_type=F32)
    o_ref[...] = x_ref[...] + mod_ref[0, 2:3, :] * y


def _out_proj(x, ypre, mods, w, *, n_x_tiles):
    r, d = x.shape
    return pl.pallas_call(
        _out_proj_kernel,
        out_shape=jax.ShapeDtypeStruct((r, d), F32),
        grid=(r // TM,),
        in_specs=[pl.BlockSpec((TM, d), lambda i: (i, 0)),
                  pl.BlockSpec((TM, d), lambda i: (i, 0)),
                  pl.BlockSpec((1, 6, d), lambda i: (i // n_x_tiles, 0, 0)),
                  pl.BlockSpec((d, d), lambda i: (0, 0))],
        out_specs=pl.BlockSpec((TM, d), lambda i: (i, 0)),
        compiler_params=_params("parallel"), name="out_proj",
    )(x, ypre, mods, w)


def _conv_kernel(x_ref, bg_ref, cg_ref, v_ref, cgp_ref, vp_ref, cgn_ref, vn_ref, cw_ref, mod_ref, w_ref,
                 o_ref, *, n_x_tiles):
    i = pl.program_id(0)
    has_prev = jnp.logical_and(i != 0, i != n_x_tiles).astype(F32)
    has_next = jnp.logical_and(i != n_x_tiles - 1, i != n_x_tiles).astype(F32)
    z = cg_ref[...].astype(F32) * v_ref[...].astype(F32)
    z_prev = cgp_ref[7:8, :].astype(F32) * vp_ref[7:8, :].astype(F32) * has_prev
    z_next = cgn_ref[0:1, :].astype(F32) * vn_ref[0:1, :].astype(F32) * has_next
    row = lax.broadcasted_iota(jnp.int32, z.shape, 0)
    z_dn = jnp.where(row == 0, z_prev, pltpu.roll(z, 1, 0))
    z_up = jnp.where(row == TM - 1, z_next, pltpu.roll(z, TM - 1, 0))
    conv = z_dn * cw_ref[0:1, :] + z * cw_ref[1:2, :] + z_up * cw_ref[2:3, :]
    ypre = (bg_ref[...].astype(F32) * conv).astype(BF16)
    y = jnp.dot(ypre, w_ref[...], preferred_element_type=F32)
    o_ref[...] = x_ref[...] + mod_ref[0, 2:3, :] * y


def _conv_mixer(x, u, conv_w, mods, w_out, *, n_x_tiles):
    r, d = x.shape
    n8 = r // 8
    t8 = TM // 8
    tile = lambda c: pl.BlockSpec((TM, d), lambda i: (i, c))
    prev = lambda c: pl.BlockSpec((8, d), lambda i: (jnp.maximum(i * t8 - 1, 0), c))
    nxt = lambda c: pl.BlockSpec((8, d), lambda i: (jnp.minimum((i + 1) * t8, n8 - 1), c))
    return pl.pallas_call(
        functools.partial(_conv_kernel, n_x_tiles=n_x_tiles),
        out_shape=jax.ShapeDtypeStruct((r, d), F32),
        grid=(r // TM,),
        in_specs=[pl.BlockSpec((TM, d), lambda i: (i, 0)), tile(0), tile(1), tile(2),
                  prev(1), prev(2), nxt(1), nxt(2),
                  pl.BlockSpec((CONV_W, d), lambda i: (0, 0)),
                  pl.BlockSpec((1, 6, d), lambda i: (i // n_x_tiles, 0, 0)),
                  pl.BlockSpec((d, d), lambda i: (0, 0))],
        out_specs=pl.BlockSpec((TM, d), lambda i: (i, 0)),
        compiler_params=_params("parallel"), name="conv_mixer",
    )(x, u, u, u, u, u, u, u, conv_w, mods, w_out)


def _log_sigmoid(z):
    return jnp.minimum(z, 0.0) - jnp.log1p(jnp.exp(-jnp.abs(z)))


def _gla_chunk(q_ref, k_ref, v_ref, gk, o_ref, st_ref, d, c, reverse):
    C = GLA_CHUNK
    rows = slice(c * C, (c + 1) * C)
    ii = lax.broadcasted_iota(jnp.int32, (C, C), 0)
    jj = lax.broadcasted_iota(jnp.int32, (C, C), 1)
    causal = (jj >= ii) if reverse else (jj <= ii)
    bcum = jnp.dot(causal.astype(F32), gk[rows], precision=HIGHEST, preferred_element_type=F32)
    total = bcum[0:1, :] if reverse else bcum[C - 1:C, :]
    q = q_ref[rows, :].astype(F32) * (GLA_HK ** -0.5)
    k = k_ref[rows, :].astype(F32)
    q_dec = (q * jnp.exp(bcum)).astype(BF16)
    k_inv = (k * jnp.exp(-bcum)).astype(BF16)
    k_end = (k * jnp.exp(total - bcum)).astype(BF16)
    decay = jnp.exp(total)
    for h in range(GLA_HEADS):
        kcols = slice(h * GLA_HK, (h + 1) * GLA_HK)
        vcols = slice(h * GLA_HV, (h + 1) * GLA_HV)
        vh = v_ref[rows, vcols]
        att = lax.dot_general(q_dec[:, kcols], k_inv[:, kcols], _NT, preferred_element_type=F32)
        att = jnp.where(causal, att, 0.0).astype(BF16)
        st = st_ref[d, h]
        o = (jnp.dot(att, vh, preferred_element_type=F32)
             + lax.dot_general(q_dec[:, kcols], st.astype(BF16), _NT, preferred_element_type=F32))
        st_ref[d, h] = st * decay[:, kcols] + lax.dot_general(vh, k_end[:, kcols], _TN,
                                                              preferred_element_type=F32)
        o_ref[rows, vcols] = o.astype(o_ref.dtype)


def _gla_kernel(qf_ref, kf_ref, vf_ref, rf_ref, qb_ref, kb_ref, vb_ref, rb_ref, w2_ref, b_ref,
                of_ref, ob_ref, st_ref):
    @pl.when(pl.program_id(0) == 0)
    def _():
        st_ref[...] = jnp.zeros_like(st_ref)

    def gates(r_ref, d):
        z = jnp.dot(r_ref[...], w2_ref[d], precision=HIGHEST, preferred_element_type=F32) + b_ref[d]
        return _log_sigmoid(z) * (1.0 / GLA_NORMALIZER)

    gk_f, gk_b = gates(rf_ref, 0), gates(rb_ref, 1)
    n_chunks = TM // GLA_CHUNK
    for c in range(n_chunks):
        _gla_chunk(qf_ref, kf_ref, vf_ref, gk_f, of_ref, st_ref, 0, c, False)
        _gla_chunk(qb_ref, kb_ref, vb_ref, gk_b, ob_ref, st_ref, 1, n_chunks - 1 - c, True)


def _gla_scan(u, r_low, w2pad, b_gk):
    rws = u.shape[0]
    nt = rws // TM
    fwd = lambda s: (s + nt - 1) % nt
    bwd = lambda s: nt - 1 - s
    qkvr = lambda order: [pl.BlockSpec((TM, GLA_DK), lambda s: (order(s), 0)),
                          pl.BlockSpec((TM, GLA_DK), lambda s: (order(s), 1)),
                          pl.BlockSpec((TM, GLA_DV), lambda s: (order(s), 1)),
                          pl.BlockSpec((TM, LANES), lambda s: (order(s), 0))]
    out = jax.ShapeDtypeStruct((rws, GLA_DV), BF16)
    return pl.pallas_call(
        _gla_kernel,
        out_shape=[out, out],
        grid=(nt,),
        in_specs=qkvr(fwd) + qkvr(bwd) + [pl.BlockSpec((2, LANES, GLA_DK), lambda s: (0, 0, 0)),
                                          pl.BlockSpec((2, 1, GLA_DK), lambda s: (0, 0, 0))],
        out_specs=[pl.BlockSpec((TM, GLA_DV), lambda s: (fwd(s), 0)),
                   pl.BlockSpec((TM, GLA_DV), lambda s: (bwd(s), 0))],
        scratch_shapes=[pltpu.VMEM((2, GLA_HEADS, GLA_HV, GLA_HK), F32)],
        compiler_params=_params("arbitrary"), name="gla_scan",
    )(u, u, u, r_low, u, u, u, r_low, w2pad, b_gk.reshape(2, 1, GLA_DK))


def _gla_out_kernel(x_ref, of_ref, ob_ref, g_ref, gn_ref, mod_ref, w_ref, o_ref):
    o = of_ref[...].astype(F32) + ob_ref[...].astype(F32)
    parts = []
    for h in range(GLA_HEADS):
        oh = o[:, h * GLA_HV:(h + 1) * GLA_HV]
        parts.append(oh * lax.rsqrt(jnp.mean(oh * oh, axis=-1, keepdims=True) + EPS) * gn_ref[...])
    ypre = (jnp.concatenate(parts, axis=1) * _silu(g_ref[...].astype(F32))).astype(BF16)
    y = jnp.dot(ypre, w_ref[...], preferred_element_type=F32)
    o_ref[...] = x_ref[...] + mod_ref[0, 2:3, :] * y


def _gla_out(x, o_f, o_b, u, g_norm, mods, w, *, n_x_tiles):
    r, d = x.shape
    tile = lambda c: pl.BlockSpec((TM, d), lambda i: (i, c))
    return pl.pallas_call(
        _gla_out_kernel,
        out_shape=jax.ShapeDtypeStruct((r, d), F32),
        grid=(r // TM,),
        in_specs=[tile(0), tile(0), tile(0), tile(2),
                  pl.BlockSpec((1, GLA_HV), lambda i: (0, 0)),
                  pl.BlockSpec((1, 6, d), lambda i: (i // n_x_tiles, 0, 0)),
                  pl.BlockSpec((d, d), lambda i: (0, 0))],
        out_specs=tile(0),
        compiler_params=_params("parallel"), name="gla_out",
    )(x, o_f, o_b, u, g_norm.reshape(1, GLA_HV), mods, w)


def _swa_kernel(sink_ref, q_ref, kvp_ref, kvo_ref, kvn_ref, kvc_ref, o_ref, kv_buf, *, seq_len):
    i = pl.program_id(0)
    half = TM // 2
    kv_buf[0:half] = kvp_ref[...]
    kv_buf[half:half + TM] = kvo_ref[...]
    kv_buf[half + TM:2 * TM] = kvn_ref[...]
    kv_buf[2 * TM:3 * TM] = kvc_ref[...]
    nk = 3 * TM
    qpos = i * TM + lax.broadcasted_iota(jnp.int32, (TM, nk), 0)
    col = lax.broadcasted_iota(jnp.int32, (TM, nk), 1)
    kpos = i * TM - half + col
    local = ((jnp.abs(kpos - qpos) <= SWA_WINDOW) & (kpos >= 0) & (kpos < seq_len) & (qpos < seq_len))
    valid = local | (col >= 2 * TM)
    kv_w = SWA_KV_HEADS * HEAD_DIM
    group = SWA_HEADS // SWA_KV_HEADS
    for h in range(SWA_HEADS):
        kv = h // group
        qh = q_ref[:, h * HEAD_DIM:(h + 1) * HEAD_DIM]
        kh = kv_buf[:, kv * HEAD_DIM:(kv + 1) * HEAD_DIM]
        vh = kv_buf[:, kv_w + kv * HEAD_DIM:kv_w + (kv + 1) * HEAD_DIM]
        s = lax.dot_general(qh, kh, _NT, preferred_element_type=F32)
        s = jnp.where(valid, s, NEG)
        sink = sink_ref[h] * LOG2E
        m = jnp.maximum(jnp.max(s, axis=-1, keepdims=True), sink)
        p = jnp.exp2(s - m)
        l = jnp.sum(p, axis=-1, keepdims=True) + jnp.exp2(sink - m)
        o = jnp.dot(p.astype(BF16), vh, preferred_element_type=F32) / l
        o_ref[:, h * HEAD_DIM:(h + 1) * HEAD_DIM] = o.astype(o_ref.dtype)


def _swa_attention(u, sinks, *, seq_len):
    rws = u.shape[0]
    nt = rws // TM
    half = TM // 2
    n_half = rws // half
    kv_w = 2 * SWA_KV_HEADS * HEAD_DIM
    qw = SWA_HEADS * HEAD_DIM
    kvc = qw // kv_w
    return pl.pallas_call(
        functools.partial(_swa_kernel, seq_len=seq_len),
        out_shape=jax.ShapeDtypeStruct((rws, qw), BF16),
        grid=(nt,),
        in_specs=[pl.BlockSpec(memory_space=pltpu.SMEM),
                  pl.BlockSpec((TM, qw), lambda i: (i, 0)),
                  pl.BlockSpec((half, kv_w), lambda i: (jnp.maximum(2 * i - 1, 0), kvc)),
                  pl.BlockSpec((TM, kv_w), lambda i: (i, kvc)),
                  pl.BlockSpec((half, kv_w), lambda i: (jnp.minimum(2 * i + 2, n_half - 1), kvc)),
                  pl.BlockSpec((TM, kv_w), lambda i: (nt - 1, kvc))],
        out_specs=pl.BlockSpec((TM, qw), lambda i: (i, 0)),
        scratch_shapes=[pltpu.VMEM((3 * TM, kv_w), BF16)],
        compiler_params=_params("parallel"), name="swa_attention",
    )(sinks, u, u, u, u, u)


def _na_kernel(q_ref, kp_ref, ko_ref, kn_ref, kc_ref, vp_ref, vo_ref, vn_ref, vc_ref, bias_ref, o_ref,
               k_buf, v_buf, *, n_rows):
    i = pl.program_id(0)
    for j, (kr, vr) in enumerate(((kp_ref, vp_ref), (ko_ref, vo_ref), (kn_ref, vn_ref), (kc_ref, vc_ref))):
        k_buf[j * TM:(j + 1) * TM] = kr[...]
        v_buf[j * TM:(j + 1) * TM] = vr[...]
    n_loc = 3 * TM
    rpt = TM // GRID_W
    qi = lax.broadcasted_iota(jnp.int32, (TM, n_loc), 0)
    ki = lax.broadcasted_iota(jnp.int32, (TM, n_loc), 1)
    r = i * rpt + qi // GRID_W
    c = qi % GRID_W
    krow = (i - 1) * rpt + ki // GRID_W
    kcol = ki % GRID_W
    r0 = jnp.clip(r - NA_KH // 2, 0, n_rows - NA_KH)
    c0 = jnp.clip(c - NA_KW // 2, 0, GRID_W - NA_KW)
    valid = ((krow >= r0) & (krow < r0 + NA_KH) & (kcol >= c0) & (kcol < c0 + NA_KW) & (r < n_rows))
    for h in range(NA_HEADS):
        cols = slice(h * HEAD_DIM, (h + 1) * HEAD_DIM)
        qh = q_ref[:, cols]
        s = lax.dot_general(qh, k_buf[:, cols], _NT, preferred_element_type=F32)
        s_loc = jnp.where(valid, s[:, :n_loc] + bias_ref[h], NEG)
        s_ctx = s[:, n_loc:]
        m = jnp.maximum(jnp.max(s_loc, axis=-1, keepdims=True), jnp.max(s_ctx, axis=-1, keepdims=True))
        p_loc = jnp.exp2(s_loc - m)
        p_ctx = jnp.exp2(s_ctx - m)
        l = jnp.sum(p_loc, axis=-1, keepdims=True) + jnp.sum(p_ctx, axis=-1, keepdims=True)
        o = (jnp.dot(p_loc.astype(BF16), v_buf[0:n_loc, cols], preferred_element_type=F32)
             + jnp.dot(p_ctx.astype(BF16), v_buf[n_loc:, cols], preferred_element_type=F32)) / l
        o_ref[:, cols] = o.astype(o_ref.dtype)


def _na_bias_kernel(t_ref, o_ref, *, lo):
    rpt = TM // GRID_W
    for a in range(rpt):
        for b in range(3 * rpt):
            o_ref[0, a * GRID_W:(a + 1) * GRID_W, b * GRID_W:(b + 1) * GRID_W] = t_ref[0, lo - a + b]


def _na_bias_table(rpb):
    n_heads = rpb.shape[0]
    rpt = TM // GRID_W
    lo = NA_KH - 1 - rpt
    assert lo - (rpt - 1) >= 0 and lo + 3 * rpt <= 2 * NA_KH - 1
    c = jnp.arange(GRID_W)
    col_off = c[None, :] - c[:, None] + NA_KW - 1
    onehot = (col_off[None] == jnp.arange(2 * NA_KW - 1)[:, None, None]).astype(F32)
    tcol = jnp.einsum('hrj,jck->hrck', rpb.astype(F32) * LOG2E, onehot, precision=HIGHEST)
    return pl.pallas_call(
        functools.partial(_na_bias_kernel, lo=lo),
        out_shape=jax.ShapeDtypeStruct((n_heads, TM, 3 * TM), F32),
        grid=(n_heads,),
        in_specs=[pl.BlockSpec((1, 2 * NA_KH - 1, GRID_W, GRID_W), lambda h: (h, 0, 0, 0))],
        out_specs=pl.BlockSpec((1, TM, 3 * TM), lambda h: (h, 0, 0)),
        compiler_params=_params("parallel"), name="na_bias_table",
    )(tcol)


def _na_attention(u, bias, *, n_rows):
    rws = u.shape[0]
    nt = rws // TM
    d = NA_HEADS * HEAD_DIM
    blk = lambda f, c: pl.BlockSpec((TM, d), lambda i: (f(i), c))
    prev = lambda i: jnp.maximum(i - 1, 0)
    own = lambda i: i
    nxt = lambda i: jnp.minimum(i + 1, nt - 1)
    ctx = lambda i: nt - 1
    return pl.pallas_call(
        functools.partial(_na_kernel, n_rows=n_rows),
        out_shape=jax.ShapeDtypeStruct((rws, d), BF16),
        grid=(nt,),
        in_specs=[blk(own, 0), blk(prev, 1), blk(own, 1), blk(nxt, 1), blk(ctx, 1),
                  blk(prev, 2), blk(own, 2), blk(nxt, 2), blk(ctx, 2),
                  pl.BlockSpec((NA_HEADS, TM, 3 * TM), lambda i: (0, 0, 0))],
        out_specs=pl.BlockSpec((TM, d), lambda i: (i, 0)),
        scratch_shapes=[pltpu.VMEM((4 * TM, d), BF16), pltpu.VMEM((4 * TM, d), BF16)],
        compiler_params=_params("parallel"), name="na_attention",
    )(u, u, u, u, u, u, u, u, u, bias)


def _router_kernel(x_ref, g_ref, mod_ref, wr_ref, br_ref, e_ref, w_ref, rank_ref, cnt_ref, carry):
    @pl.when(pl.program_id(0) == 0)
    def _():
        carry[...] = jnp.zeros_like(carry)

    h = _norm_mod(x_ref[...], g_ref[...], mod_ref[0, 4:5, :], mod_ref[0, 3:4, :])
    logits = lax.dot_general(wr_ref[...], h, _NT, precision=HIGHEST, preferred_element_type=F32)
    scores = jax.nn.sigmoid(logits)
    biased = scores + br_ref[...]
    row = lambda a, e: a[e:e + 1, :]
    best_g = jnp.zeros((1, TM), jnp.int32)
    best_s = None
    for g in range(N_GROUPS):
        v = [row(biased, g * EXPERTS_PER_GROUP + j) for j in range(EXPERTS_PER_GROUP)]
        gs = None
        for a in range(EXPERTS_PER_GROUP):
            for b in range(a + 1, EXPERTS_PER_GROUP):
                pair = v[a] + v[b]
                gs = pair if gs is None else jnp.maximum(gs, pair)
        if best_s is None:
            best_s = gs
        else:
            better = gs > best_s
            best_g = jnp.where(better, g, best_g)
            best_s = jnp.where(better, gs, best_s)
    picks = []
    for _ in range(2):
        top_v = jnp.full((1, TM), -jnp.inf, F32)
        top_i = jnp.full((1, TM), -1, jnp.int32)
        for e in range(N_EXPERTS):
            ok = best_g == (e // EXPERTS_PER_GROUP)
            for p in picks:
                ok = jnp.logical_and(ok, p != e)
            cand = jnp.where(ok, row(biased, e), -jnp.inf)
            better = cand > top_v
            top_i = jnp.where(better, e, top_i)
            top_v = jnp.where(better, cand, top_v)
        picks.append(top_i)
    e_iota = lax.broadcasted_iota(jnp.int32, (N_EXPERTS, TM), 0)
    sel0 = e_iota == picks[0]
    sel1 = e_iota == picks[1]
    w0 = jnp.sum(jnp.where(sel0, scores, 0.0), axis=0, keepdims=True)
    w1 = jnp.sum(jnp.where(sel1, scores, 0.0), axis=0, keepdims=True)
    tot = w0 + w1
    e_ref[0:1, :] = picks[0]
    e_ref[1:2, :] = picks[1]
    w_ref[0:1, :] = w0 / tot
    w_ref[1:2, :] = w1 / tot
    member = jnp.where(jnp.logical_or(sel0, sel1), 1.0, 0.0)
    earlier = (lax.broadcasted_iota(jnp.int32, (TM, TM), 0)
               < lax.broadcasted_iota(jnp.int32, (TM, TM), 1)).astype(BF16)
    before = carry[:, 0:1] + jnp.dot(member.astype(BF16), earlier, preferred_element_type=F32)
    rank_ref[0:1, :] = jnp.sum(jnp.where(sel0, before, 0.0), axis=0, keepdims=True).astype(jnp.int32)
    rank_ref[1:2, :] = jnp.sum(jnp.where(sel1, before, 0.0), axis=0, keepdims=True).astype(jnp.int32)
    carry[...] = carry[...] + jnp.sum(member, axis=1, keepdims=True)
    cnt_ref[...] = carry[...].astype(jnp.int32)


def _router(x, g, mods, w_router_t, b_router, *, n_x_tiles):
    r, d = x.shape
    pair = pl.BlockSpec((2, TM), lambda i: (0, i))
    return pl.pallas_call(
        _router_kernel,
        out_shape=[jax.ShapeDtypeStruct((2, r), jnp.int32), jax.ShapeDtypeStruct((2, r), F32),
                   jax.ShapeDtypeStruct((2, r), jnp.int32), jax.ShapeDtypeStruct((N_EXPERTS, LANES), jnp.int32)],
        grid=(r // TM,),
        in_specs=[pl.BlockSpec((TM, d), lambda i: (i, 0)),
                  pl.BlockSpec((1, d), lambda i: (0, 0)),
                  pl.BlockSpec((1, 6, d), lambda i: (i // n_x_tiles, 0, 0)),
                  pl.BlockSpec((N_EXPERTS, d), lambda i: (0, 0)),
                  pl.BlockSpec((N_EXPERTS, 1), lambda i: (0, 0))],
        out_specs=[pair, pair, pair, pl.BlockSpec((N_EXPERTS, LANES), lambda i: (0, 0))],
        scratch_shapes=[pltpu.VMEM((N_EXPERTS, LANES), F32)],
        compiler_params=_params("arbitrary"), name="router",
    )(x, g.reshape(1, d), mods, w_router_t, b_router.reshape(N_EXPERTS, 1))


def _dispatch_plan(e01, rank01, counts, n_blocks):
    cnt = counts[:, 0]
    nblk = (cnt + MOE_BLOCK - 1) // MOE_BLOCK
    blk_end = jnp.cumsum(nblk)
    off = (blk_end - nblk) * MOE_BLOCK
    eidx = jnp.arange(N_EXPERTS, dtype=jnp.int32)[:, None, None]
    pos = rank01 + jnp.sum(jnp.where(e01[None] == eidx, off[:, None, None], 0), axis=0)
    blk_e = jnp.sum(jnp.arange(n_blocks, dtype=jnp.int32)[:, None] >= blk_end[None, :], axis=1)
    blk_e = jnp.minimum(blk_e, N_EXPERTS - 1).astype(jnp.int32)
    n_used = blk_end[-1:]
    fill = jnp.concatenate([off + cnt, nblk * MOE_BLOCK - cnt, n_used]).astype(jnp.int32)
    return pos.reshape(-1).astype(jnp.int32), blk_e, n_used.astype(jnp.int32), fill


def _zero_fill(fill_ref, zbuf, xs_ref, zsem, n_blocks, wait):
    def copy(start, size):
        cp = pltpu.make_async_copy(zbuf.at[pl.ds(0, size)], xs_ref.at[pl.ds(start, size)], zsem)
        cp.wait() if wait else cp.start()

    sub = 8
    for e in range(N_EXPERTS):
        start, length = fill_ref[e], fill_ref[N_EXPERTS + e]
        head = (-start) & (sub - 1)
        for r in range(sub - 1):
            @pl.when(r < head)
            def _():
                copy(start + r, 1)
        start, length = start + head, length - head
        piece = MOE_BLOCK // 2
        while piece >= sub:
            @pl.when((length & piece) != 0)
            def _():
                copy(pl.multiple_of(start + (length & (-2 * piece)), sub), piece)
            piece //= 2

    def body(j, carry):
        for part in range(MOE_BLOCK // TM):
            copy(pl.multiple_of(j * MOE_BLOCK + part * TM, TM), TM)
        return carry

    lax.fori_loop(fill_ref[2 * N_EXPERTS], n_blocks, body, 0)


def _dispatch_kernel(pos_ref, fill_ref, x_ref, g_ref, mod_ref, xs_ref, h_scr, zbuf, sem, zsem, *,
                     rows, nt, n_blocks):
    i = pl.program_id(0)
    slot = i % 2

    @pl.when(i == 0)
    def _():
        zbuf[...] = jnp.zeros_like(zbuf)
        _zero_fill(fill_ref, zbuf, xs_ref, zsem, n_blocks, wait=False)

    def wait_slot(s):
        for _ in range(2):
            pltpu.make_async_copy(h_scr.at[s], xs_ref.at[pl.ds(0, TM)], sem.at[s]).wait()

    @pl.when(i >= 2)
    def _():
        wait_slot(slot)

    h_scr[slot] = _norm_mod(x_ref[...], g_ref[...], mod_ref[0, 4:5, :], mod_ref[0, 3:4, :])
    base = i * TM

    def body(t, carry):
        src = h_scr.at[slot, pl.ds(t, 1)]
        pltpu.make_async_copy(src, xs_ref.at[pl.ds(pos_ref[base + t], 1)], sem.at[slot]).start()
        pltpu.make_async_copy(src, xs_ref.at[pl.ds(pos_ref[rows + base + t], 1)], sem.at[slot]).start()
        return carry

    lax.fori_loop(0, TM, body, 0, unroll=8)

    @pl.when(i == nt - 1)
    def _():
        wait_slot(slot)
        if nt > 1:
            wait_slot(1 - slot)
        _zero_fill(fill_ref, zbuf, xs_ref, zsem, n_blocks, wait=True)


def _dispatch(pos, fill, x, g, mods, n_blocks, *, n_x_tiles):
    r, d = x.shape
    nt = r // TM
    assert MOE_BLOCK % TM == 0 and MOE_BLOCK // 2 <= TM
    return pl.pallas_call(
        functools.partial(_dispatch_kernel, rows=r, nt=nt, n_blocks=n_blocks),
        out_shape=jax.ShapeDtypeStruct((n_blocks * MOE_BLOCK, d), F32),
        grid_spec=pltpu.PrefetchScalarGridSpec(
            num_scalar_prefetch=2, grid=(nt,),
            in_specs=[pl.BlockSpec((TM, d), lambda i, p, f: (i, 0)),
                      pl.BlockSpec((1, d), lambda i, p, f: (0, 0)),
                      pl.BlockSpec((1, 6, d), lambda i, p, f: (i // n_x_tiles, 0, 0))],
            out_specs=pl.BlockSpec(memory_space=pl.ANY),
            scratch_shapes=[pltpu.VMEM((2, TM, d), F32), pltpu.VMEM((TM, d), F32),
                            pltpu.SemaphoreType.DMA((2,)), pltpu.SemaphoreType.DMA]),
        compiler_params=_params("arbitrary"), name="moe_dispatch",
    )(pos, fill, x, g.reshape(1, d), mods)


def _expert_kernel(be_ref, nu_ref, xs_ref, wg_ref, wu_ref, wd_ref, ys_ref, wg_s, wu_s, wd_s):
    j = pl.program_id(0)

    @pl.when(j < nu_ref[0])
    def _():
        @pl.when(jnp.logical_or(j == 0, be_ref[j] != be_ref[jnp.maximum(j - 1, 0)]))
        def _():
            wg_s[...] = wg_ref[0].astype(BF16)
            wu_s[...] = wu_ref[0].astype(BF16)
            wd_s[...] = wd_ref[0].astype(BF16)

        h = xs_ref[...].astype(BF16)
        a = (_silu(jnp.dot(h, wg_s[...], preferred_element_type=F32))
             * jnp.dot(h, wu_s[...], preferred_element_type=F32))
        ys_ref[...] = jnp.dot(a.astype(BF16), wd_s[...], preferred_element_type=F32)

    @pl.when(j >= nu_ref[0])
    def _():
        ys_ref[...] = jnp.zeros_like(ys_ref)


def _experts(blk_e, n_used, xs, wg, wu, wd):
    n_rows, d = xs.shape
    de = wg.shape[-1]
    last = lambda j, nu: jnp.minimum(j, jnp.maximum(nu[0] - 1, 0))
    return pl.pallas_call(
        _expert_kernel,
        out_shape=jax.ShapeDtypeStruct((n_rows, d), F32),
        grid_spec=pltpu.PrefetchScalarGridSpec(
            num_scalar_prefetch=2, grid=(n_rows // MOE_BLOCK,),
            in_specs=[pl.BlockSpec((MOE_BLOCK, d), lambda j, be, nu: (last(j, nu), 0)),
                      pl.BlockSpec((1, d, de), lambda j, be, nu: (be[last(j, nu)], 0, 0)),
                      pl.BlockSpec((1, d, de), lambda j, be, nu: (be[last(j, nu)], 0, 0)),
                      pl.BlockSpec((1, de, d), lambda j, be, nu: (be[last(j, nu)], 0, 0))],
            out_specs=pl.BlockSpec((MOE_BLOCK, d), lambda j, be, nu: (j, 0)),
            scratch_shapes=[pltpu.VMEM((d, de), BF16), pltpu.VMEM((d, de), BF16), pltpu.VMEM((de, d), BF16)]),
        compiler_params=_params("arbitrary"), name="moe_experts",
    )(blk_e, n_used, xs, wg, wu, wd)


def _combine_kernel(pos_ref, x_ref, mod_ref, w_ref, ys_ref, *rest, rows, nt, final):
    if final:
        gf_ref, o_ref, buf, sem = rest
    else:
        o_ref, buf, sem = rest
    i = pl.program_id(0)
    slot = i % 2

    def issue(tile, s):
        base = tile * TM

        def body(t, carry):
            pltpu.make_async_copy(ys_ref.at[pl.ds(pos_ref[base + t], 1)], buf.at[s, 0, pl.ds(t, 1)],
                                  sem.at[s]).start()
            pltpu.make_async_copy(ys_ref.at[pl.ds(pos_ref[rows + base + t], 1)], buf.at[s, 1, pl.ds(t, 1)],
                                  sem.at[s]).start()
            return carry

        lax.fori_loop(0, TM, body, 0, unroll=8)

    @pl.when(i == 0)
    def _():
        issue(0, 0)

    @pl.when(i + 1 < nt)
    def _():
        issue(i + 1, 1 - slot)

    for k in range(2):
        pltpu.make_async_copy(ys_ref.at[pl.ds(0, TM)], buf.at[slot, k], sem.at[slot]).wait()
    w = w_ref[...]
    y = w[:, 0:1] * buf[slot, 0] + w[:, 1:2] * buf[slot, 1]
    x = x_ref[...] + mod_ref[0, 5:6, :] * y
    if final:
        x = x * lax.rsqrt(jnp.mean(x * x, axis=-1, keepdims=True) + EPS) * gf_ref[...]
    o_ref[...] = x


def _combine(pos, x, mods, w_cols, ys, *, n_x_tiles, g_final=None):
    r, d = x.shape
    final = g_final is not None
    nt = n_x_tiles if final else r // TM
    in_specs = [pl.BlockSpec((TM, d), lambda i, p: (i, 0)),
                pl.BlockSpec((1, 6, d), lambda i, p: (i // n_x_tiles, 0, 0)),
                pl.BlockSpec((TM, 2), lambda i, p: (i, 0)),
                pl.BlockSpec(memory_space=pl.ANY)]
    args = [pos, x, mods, w_cols, ys]
    if final:
        in_specs.append(pl.BlockSpec((1, d), lambda i, p: (0, 0)))
        args.append(g_final.reshape(1, d))
    return pl.pallas_call(
        functools.partial(_combine_kernel, rows=r, nt=nt, final=final),
        out_shape=jax.ShapeDtypeStruct((nt * TM, d), F32),
        grid_spec=pltpu.PrefetchScalarGridSpec(
            num_scalar_prefetch=1, grid=(nt,), in_specs=in_specs,
            out_specs=pl.BlockSpec((TM, d), lambda i, p: (i, 0)),
            scratch_shapes=[pltpu.VMEM((2, 2, TM, d), F32), pltpu.SemaphoreType.DMA((2,))]),
        compiler_params=_params("arbitrary"), name="moe_combine",
    )(*args)


def _moe(x, g, mods, w_router_t, b_router, wg, wu, wd, *, n_x_tiles, g_final=None):
    r = x.shape[0]
    n_blocks = -(-2 * r // MOE_BLOCK) + N_EXPERTS
    e01, w01, rank01, counts = _router(x, g, mods, w_router_t, b_router, n_x_tiles=n_x_tiles)
    pos, blk_e, n_used, fill = _dispatch_plan(e01, rank01, counts, n_blocks)
    xs = _dispatch(pos, fill, x, g, mods, n_blocks, n_x_tiles=n_x_tiles)
    ys = _experts(blk_e, n_used, xs, wg, wu, wd)
    return _combine(pos, x, mods, w01.T, ys, n_x_tiles=n_x_tiles, g_final=g_final)


def _rope_tables(seq_len, n_rows_total):
    half = HEAD_DIM // 2
    inv_freq = ROPE_BASE ** (-jnp.arange(0, half, 2, dtype=F32) / half)
    first = jnp.arange(half) < (half // 2)

    def tab(n):
        ang = jnp.arange(n, dtype=F32)[:, None] * inv_freq
        ang = jnp.concatenate([ang, ang], axis=-1)
        c, s = jnp.cos(ang), jnp.sin(ang)
        return jnp.stack([c, jnp.where(first, -s, 0.0), jnp.where(first, 0.0, s)], axis=1)

    ident = jnp.stack([jnp.ones((half,), F32), jnp.zeros((half,), F32), jnp.zeros((half,), F32)])
    reps = LANES // HEAD_DIM
    n_ctx_rows = (n_rows_total - seq_len) // GRID_W
    row = jnp.concatenate([tab(seq_len // GRID_W), jnp.broadcast_to(ident, (n_ctx_rows, 3, half))], axis=0)
    row = jnp.tile(jnp.concatenate([row, jnp.zeros_like(row)], axis=-1), (1, 1, reps))
    col = jnp.stack([tab(GRID_W), jnp.broadcast_to(ident, (GRID_W, 3, half))])
    col = jnp.tile(jnp.concatenate([jnp.zeros_like(col), col], axis=-1), (1, TM // GRID_W, 1, reps))
    return row, col.transpose(0, 2, 1, 3)


def kernel(x, c, ctx, c_ctx, w_mod, b_mod, g_mix, g_ffn, g_final, conv_w_in, conv_w, conv_w_out,
           gla_w_proj, gla_w_gk1, gla_w_gk2, gla_b_gk, gla_g_norm, gla_w_out, swa_w_qkv, swa_sinks,
           swa_w_out, na_w_qkv, na_rpb, na_w_out, router_w, router_b, moe_w_gate, moe_w_up, moe_w_down):
    seq_len, d = x.shape[1], x.shape[2]
    ctx_len = ctx.shape[1]
    assert x.shape[0] == 1 and ctx_len == TM and seq_len % TM == 0 and d == D_MODEL
    assert seq_len % GRID_W == 0 and seq_len // GRID_W >= NA_KH
    n_x_tiles = seq_len // TM
    rows = seq_len + ctx_len
    xs = jnp.concatenate([x[0], ctx[0]], axis=0)
    mods_all = _ada_params(c, c_ctx, w_mod, b_mod)
    w_router_t = router_w.T.astype(F32)
    n_mixers = 4

    for i in range(DEPTH):
        kind, j = i % n_mixers, i // n_mixers
        mods = mods_all[i]
        if kind == 0:
            u = _project(xs, g_mix[i], mods, conv_w_in[j].astype(BF16), n_x_tiles=n_x_tiles)
            xs = _conv_mixer(xs, u, conv_w[j], mods, conv_w_out[j].astype(BF16), n_x_tiles=n_x_tiles)
        elif kind == 1:
            n_extra = LANES
            w_ext = jnp.concatenate(
                [gla_w_proj[j], gla_w_gk1[j, 0], gla_w_gk1[j, 1],
                 jnp.zeros((d, n_extra - 2 * GLA_RANK), F32)], axis=1).astype(BF16)
            u, r_low = _project(xs, g_mix[i], mods, w_ext, n_x_tiles=n_x_tiles, n_extra=n_extra)
            w2pad = jnp.stack(
                [jnp.zeros((n_extra, GLA_DK), F32).at[k * GLA_RANK:(k + 1) * GLA_RANK].set(gla_w_gk2[j, k])
                 for k in range(2)])
            o_f, o_b = _gla_scan(u, r_low, w2pad, gla_b_gk[j])
            xs = _gla_out(xs, o_f, o_b, u, gla_g_norm[j], mods, gla_w_out[j].astype(BF16), n_x_tiles=n_x_tiles)
        elif kind == 2:
            rope = _rope_tables(seq_len, rows)
            u = _project(xs, g_mix[i], mods, swa_w_qkv[j].astype(BF16), n_x_tiles=n_x_tiles,
                         q_cols=SWA_HEADS * HEAD_DIM, q_scale=HEAD_DIM ** -0.5 * LOG2E, rope=rope,
                         rope_cols=(SWA_HEADS + SWA_KV_HEADS) * HEAD_DIM)
            ypre = _swa_attention(u, swa_sinks[j], seq_len=seq_len)
            xs = _out_proj(xs, ypre, mods, swa_w_out[j].astype(BF16), n_x_tiles=n_x_tiles)
        else:
            u = _project(xs, g_mix[i], mods, na_w_qkv[j].astype(BF16), n_x_tiles=n_x_tiles,
                         q_cols=NA_HEADS * HEAD_DIM, q_scale=HEAD_DIM ** -0.5 * LOG2E)
            ypre = _na_attention(u, _na_bias_table(na_rpb[j]), n_rows=seq_len // GRID_W)
            xs = _out_proj(xs, ypre, mods, na_w_out[j].astype(BF16), n_x_tiles=n_x_tiles)
        xs = _moe(xs, g_ffn[i], mods, w_router_t, router_b, moe_w_gate[i], moe_w_up[i], moe_w_down[i],
                  n_x_tiles=n_x_tiles, g_final=g_final if i == DEPTH - 1 else None)
    return xs[None]
```

```python
import functools

import jax
import jax.numpy as jnp
from jax import lax
from jax.experimental import pallas as pl
from jax.experimental.pallas import tpu as pltpu

F32 = jnp.float32
BF16 = jnp.bfloat16
HIGHEST = lax.Precision.HIGHEST

D_MODEL = 1024
DEPTH = 4
GRID_W = 64
EPS = 1e-6
CONV_W = 3
GLA_HEADS = 4
GLA_DK = D_MODEL // 2
GLA_DV = D_MODEL
GLA_HK = GLA_DK // GLA_HEADS
GLA_HV = GLA_DV // GLA_HEADS
GLA_RANK = 16
GLA_NORMALIZER = 16.0
GLA_CHUNK = 64
HEAD_DIM = 64
SWA_HEADS = D_MODEL // HEAD_DIM
SWA_KV_HEADS = SWA_HEADS // 4
SWA_WINDOW = 128
ROPE_BASE = 10000.0
NA_HEADS = D_MODEL // HEAD_DIM
NA_KH = 8
NA_KW = 16
N_EXPERTS = 16
N_GROUPS = 4
EXPERTS_PER_GROUP = N_EXPERTS // N_GROUPS
D_EXPERT = D_MODEL // 2

TM = 256
MOE_BLOCK = 512
LANES = 128
SUBLANES = 8
CHUNKS = D_MODEL // LANES
NEG = -1e30
LOG2E = 1.4426950408889634
VMEM_LIMIT = 56 * 1024 * 1024

_NT = (((1,), (1,)), ((), ()))
_TN = (((0,), (0,)), ((), ()))


def _params(*sem):
    return pltpu.CompilerParams(dimension_semantics=sem, vmem_limit_bytes=VMEM_LIMIT)


def _norm_mod(x, g, scale, shift):
    ms = jnp.mean(x * x, axis=-1, keepdims=True)
    return (x * lax.rsqrt(ms + EPS) * g) * (1.0 + scale) + shift


def _silu(x):
    return x * jax.nn.sigmoid(x)


def _ada_kernel(cond_ref, w_ref, b_ref, o_ref):
    s = _silu(cond_ref[...])
    o_ref[0] = jnp.dot(s, w_ref[0], precision=HIGHEST, preferred_element_type=F32) + b_ref[0]


def _ada_params(c, c_ctx, w_mod, b_mod):
    depth, d, n = w_mod.shape
    nc = n // 4
    cond = jnp.zeros((8, d), F32).at[0].set(c[0]).at[1].set(c_ctx)
    out = pl.pallas_call(
        _ada_kernel,
        out_shape=jax.ShapeDtypeStruct((depth, 8, n), F32),
        grid=(depth, n // nc),
        in_specs=[pl.BlockSpec((8, d), lambda l, j: (0, 0)),
                  pl.BlockSpec((1, d, nc), lambda l, j: (l, 0, j)),
                  pl.BlockSpec((1, 1, nc), lambda l, j: (l, 0, j))],
        out_specs=pl.BlockSpec((1, 8, nc), lambda l, j: (l, 0, j)),
        compiler_params=_params("parallel", "parallel"),
        name="ada_params",
    )(cond, w_mod, b_mod.reshape(depth, 1, n))
    return out[:, :2].reshape(depth, 2, 6, d)


def _proj_kernel(x_ref, g_ref, mod_ref, w_ref, *rest, n_chunk, q_cols, q_scale, rope_cols, has_extra):
    n_main = w_ref.shape[1]
    if has_extra:
        wx_ref, rest = rest[0], rest[1:]
    if rope_cols:
        row_ref, col_ref = rest[:2]
        rest = rest[2:]
        rpt = TM // GRID_W
        cos_t, sa_t, sb_t = [
            jnp.concatenate([jnp.broadcast_to(row_ref[a, k:k + 1, :], (GRID_W, LANES)) for a in range(rpt)],
                            axis=0) + col_ref[0, k]
            for k in range(3)]
    o_ref = rest[0]
    h = _norm_mod(x_ref[...], g_ref[...], mod_ref[0, 1:2, :], mod_ref[0, 0:1, :]).astype(BF16)
    for c0 in range(0, n_main, n_chunk):
        c1 = min(c0 + n_chunk, n_main)
        y = jnp.dot(h, w_ref[:, c0:c1], preferred_element_type=F32)
        if c0 < q_cols:
            y = y * q_scale
        if c0 >= rope_cols:
            o_ref[:, c0:c1] = y.astype(o_ref.dtype)
            continue
        for s0 in range(c0, c1, LANES):
            ys = y[:, s0 - c0:s0 - c0 + LANES]
            if s0 < rope_cols:
                ys = ys * cos_t + pltpu.roll(ys, LANES - 16, 1) * sa_t + pltpu.roll(ys, 16, 1) * sb_t
            o_ref[:, s0:s0 + LANES] = ys.astype(o_ref.dtype)
    if has_extra:
        rest[1][...] = jnp.dot(h, wx_ref[...], preferred_element_type=F32)


def _project(x, g, mods, w, *, n_x_tiles, n_chunk=512, q_cols=0, q_scale=1.0, rope=None, rope_cols=0,
             w_extra=None):
    r, d = x.shape
    n_main = w.shape[1]
    n_extra = 0 if w_extra is None else w_extra.shape[1]
    nt = r // TM
    in_specs = [pl.BlockSpec((TM, d), lambda i: (i, 0)),
                pl.BlockSpec((1, d), lambda i: (0, 0)),
                pl.BlockSpec((1, 6, d), lambda i: (i // n_x_tiles, 0, 0)),
                pl.BlockSpec((d, n_main), lambda i: (0, 0))]
    args = [x, g.reshape(1, d), mods, w]
    if n_extra:
        in_specs.append(pl.BlockSpec((d, n_extra), lambda i: (0, 0)))
        args.append(w_extra)
    if rope_cols:
        rpt = TM // GRID_W
        in_specs += [pl.BlockSpec((rpt, 3, LANES), lambda i: (i, 0, 0)),
                     pl.BlockSpec((1, 3, TM, LANES), lambda i: (i // n_x_tiles, 0, 0, 0))]
        args += list(rope)
    out_shape = [jax.ShapeDtypeStruct((r, n_main), BF16)]
    out_specs = [pl.BlockSpec((TM, n_main), lambda i: (i, 0))]
    if n_extra:
        out_shape.append(jax.ShapeDtypeStruct((r, n_extra), F32))
        out_specs.append(pl.BlockSpec((TM, n_extra), lambda i: (i, 0)))
    out = pl.pallas_call(
        functools.partial(_proj_kernel, n_chunk=n_chunk, q_cols=q_cols, q_scale=q_scale,
                          rope_cols=rope_cols, has_extra=bool(n_extra)),
        out_shape=out_shape, grid=(nt,), in_specs=in_specs, out_specs=out_specs,
        compiler_params=_params("parallel"), name="norm_proj",
    )(*args)
    return out if n_extra else out[0]


def _out_proj_kernel(x_ref, y_ref, mod_ref, w_ref, o_ref):
    y = jnp.dot(y_ref[...], w_ref[...], preferred_element_type=F32)
    o_ref[...] = x_ref[...] + mod_ref[0, 2:3, :] * y


def _out_proj(x, ypre, mods, w, *, n_x_tiles):
    r, d = x.shape
    return pl.pallas_call(
        _out_proj_kernel,
        out_shape=jax.ShapeDtypeStruct((r, d), F32),
        grid=(r // TM,),
        in_specs=[pl.BlockSpec((TM, d), lambda i: (i, 0)),
                  pl.BlockSpec((TM, d), lambda i: (i, 0)),
                  pl.BlockSpec((1, 6, d), lambda i: (i // n_x_tiles, 0, 0)),
                  pl.BlockSpec((d, d), lambda i: (0, 0))],
        out_specs=pl.BlockSpec((TM, d), lambda i: (i, 0)),
        compiler_params=_params("parallel"), name="out_proj",
    )(x, ypre, mods, w)


def _conv_kernel(x_ref, bg_ref, cg_ref, v_ref, cgp_ref, vp_ref, cgn_ref, vn_ref, cw_ref, mod_ref, w_ref,
                 o_ref, *, n_x_tiles):
    i = pl.program_id(0)
    has_prev = jnp.logical_and(i != 0, i != n_x_tiles).astype(F32)
    has_next = jnp.logical_and(i != n_x_tiles - 1, i != n_x_tiles).astype(F32)
    z = cg_ref[...].astype(F32) * v_ref[...].astype(F32)
    z_prev = cgp_ref[7:8, :].astype(F32) * vp_ref[7:8, :].astype(F32) * has_prev
    z_next = cgn_ref[0:1, :].astype(F32) * vn_ref[0:1, :].astype(F32) * has_next
    row = lax.broadcasted_iota(jnp.int32, z.shape, 0)
    z_dn = jnp.where(row == 0, z_prev, pltpu.roll(z, 1, 0))
    z_up = jnp.where(row == TM - 1, z_next, pltpu.roll(z, TM - 1, 0))
    conv = z_dn * cw_ref[0:1, :] + z * cw_ref[1:2, :] + z_up * cw_ref[2:3, :]
    ypre = (bg_ref[...].astype(F32) * conv).astype(BF16)
    y = jnp.dot(ypre, w_ref[...], preferred_element_type=F32)
    o_ref[...] = x_ref[...] + mod_ref[0, 2:3, :] * y


def _conv_mixer(x, u, conv_w, mods, w_out, *, n_x_tiles):
    r, d = x.shape
    n8 = r // 8
    t8 = TM // 8
    tile = lambda c: pl.BlockSpec((TM, d), lambda i: (i, c))
    prev = lambda c: pl.BlockSpec((8, d), lambda i: (jnp.maximum(i * t8 - 1, 0), c))
    nxt = lambda c: pl.BlockSpec((8, d), lambda i: (jnp.minimum((i + 1) * t8, n8 - 1), c))
    return pl.pallas_call(
        functools.partial(_conv_kernel, n_x_tiles=n_x_tiles),
        out_shape=jax.ShapeDtypeStruct((r, d), F32),
        grid=(r // TM,),
        in_specs=[pl.BlockSpec((TM, d), lambda i: (i, 0)), tile(0), tile(1), tile(2),
                  prev(1), prev(2), nxt(1), nxt(2),
                  pl.BlockSpec((CONV_W, d), lambda i: (0, 0)),
                  pl.BlockSpec((1, 6, d), lambda i: (i // n_x_tiles, 0, 0)),
                  pl.BlockSpec((d, d), lambda i: (0, 0))],
        out_specs=pl.BlockSpec((TM, d), lambda i: (i, 0)),
        compiler_params=_params("parallel"), name="conv_mixer",
    )(x, u, u, u, u, u, u, u, conv_w, mods, w_out)


def _log_sigmoid(z):
    return jnp.minimum(z, 0.0) - jnp.log1p(jnp.exp(-jnp.abs(z)))


def _gla_chunk(q_ref, k_ref, v_ref, gk, o_ref, st_ref, d, c, reverse):
    C = GLA_CHUNK
    rows = slice(c * C, (c + 1) * C)
    ii = lax.broadcasted_iota(jnp.int32, (C, C), 0)
    jj = lax.broadcasted_iota(jnp.int32, (C, C), 1)
    causal = (jj >= ii) if reverse else (jj <= ii)
    bcum = jnp.dot(causal.astype(F32), gk[rows], precision=HIGHEST, preferred_element_type=F32)
    total = bcum[0:1, :] if reverse else bcum[C - 1:C, :]
    q = q_ref[rows, :].astype(F32) * (GLA_HK ** -0.5)
    k = k_ref[rows, :].astype(F32)
    q_dec = (q * jnp.exp(bcum)).astype(BF16)
    k_inv = (k * jnp.exp(-bcum)).astype(BF16)
    k_end = (k * jnp.exp(total - bcum)).astype(BF16)
    decay = jnp.exp(total)
    for h in range(GLA_HEADS):
        kcols = slice(h * GLA_HK, (h + 1) * GLA_HK)
        vcols = slice(h * GLA_HV, (h + 1) * GLA_HV)
        vh = v_ref[rows, vcols]
        att = lax.dot_general(q_dec[:, kcols], k_inv[:, kcols], _NT, preferred_element_type=F32)
        att = jnp.where(causal, att, 0.0).astype(BF16)
        st = st_ref[d, h]
        o = (jnp.dot(att, vh, preferred_element_type=F32)
             + lax.dot_general(q_dec[:, kcols], st.astype(BF16), _NT, preferred_element_type=F32))
        st_ref[d, h] = st * decay[:, kcols] + lax.dot_general(vh, k_end[:, kcols], _TN,
                                                              preferred_element_type=F32)
        o_ref[rows, vcols] = o.astype(o_ref.dtype)


def _gla_kernel(qf_ref, kf_ref, vf_ref, rf_ref, qb_ref, kb_ref, vb_ref, rb_ref, w2_ref, b_ref,
                of_ref, ob_ref, st_ref):
    @pl.when(pl.program_id(0) == 0)
    def _():
        st_ref[...] = jnp.zeros_like(st_ref)

    def gates(r_ref, d):
        z = jnp.dot(r_ref[...], w2_ref[d], precision=HIGHEST, preferred_element_type=F32) + b_ref[d]
        return _log_sigmoid(z) * (1.0 / GLA_NORMALIZER)

    gk_f, gk_b = gates(rf_ref, 0), gates(rb_ref, 1)
    n_chunks = TM // GLA_CHUNK
    for c in range(n_chunks):
        _gla_chunk(qf_ref, kf_ref, vf_ref, gk_f, of_ref, st_ref, 0, c, False)
        _gla_chunk(qb_ref, kb_ref, vb_ref, gk_b, ob_ref, st_ref, 1, n_chunks - 1 - c, True)


def _gla_scan(u, r_low, w2pad, b_gk):
    rws = u.shape[0]
    nt = rws // TM
    fwd = lambda s: (s + nt - 1) % nt
    bwd = lambda s: nt - 1 - s
    qkvr = lambda order: [pl.BlockSpec((TM, GLA_DK), lambda s: (order(s), 0)),
                          pl.BlockSpec((TM, GLA_DK), lambda s: (order(s), 1)),
                          pl.BlockSpec((TM, GLA_DV), lambda s: (order(s), 1)),
                          pl.BlockSpec((TM, LANES), lambda s: (order(s), 0))]
    out = jax.ShapeDtypeStruct((rws, GLA_DV), BF16)
    return pl.pallas_call(
        _gla_kernel,
        out_shape=[out, out],
        grid=(nt,),
        in_specs=qkvr(fwd) + qkvr(bwd) + [pl.BlockSpec((2, LANES, GLA_DK), lambda s: (0, 0, 0)),
                                          pl.BlockSpec((2, 1, GLA_DK), lambda s: (0, 0, 0))],
        out_specs=[pl.BlockSpec((TM, GLA_DV), lambda s: (fwd(s), 0)),
                   pl.BlockSpec((TM, GLA_DV), lambda s: (bwd(s), 0))],
        scratch_shapes=[pltpu.VMEM((2, GLA_HEADS, GLA_HV, GLA_HK), F32)],
        compiler_params=_params("arbitrary"), name="gla_scan",
    )(u, u, u, r_low, u, u, u, r_low, w2pad, b_gk.reshape(2, 1, GLA_DK))


def _gla_out_kernel(x_ref, of_ref, ob_ref, g_ref, gn_ref, mod_ref, w_ref, o_ref):
    o = of_ref[...].astype(F32) + ob_ref[...].astype(F32)
    parts = []
    for h in range(GLA_HEADS):
        oh = o[:, h * GLA_HV:(h + 1) * GLA_HV]
        parts.append(oh * lax.rsqrt(jnp.mean(oh * oh, axis=-1, keepdims=True) + EPS) * gn_ref[...])
    ypre = (jnp.concatenate(parts, axis=1) * _silu(g_ref[...].astype(F32))).astype(BF16)
    y = jnp.dot(ypre, w_ref[...], preferred_element_type=F32)
    o_ref[...] = x_ref[...] + mod_ref[0, 2:3, :] * y


def _gla_out(x, o_f, o_b, u, g_norm, mods, w, *, n_x_tiles):
    r, d = x.shape
    tile = lambda c: pl.BlockSpec((TM, d), lambda i: (i, c))
    return pl.pallas_call(
        _gla_out_kernel,
        out_shape=jax.ShapeDtypeStruct((r, d), F32),
        grid=(r // TM,),
        in_specs=[tile(0), tile(0), tile(0), tile(2),
                  pl.BlockSpec((1, GLA_HV), lambda i: (0, 0)),
                  pl.BlockSpec((1, 6, d), lambda i: (i // n_x_tiles, 0, 0)),
                  pl.BlockSpec((d, d), lambda i: (0, 0))],
        out_specs=tile(0),
        compiler_params=_params("parallel"), name="gla_out",
    )(x, o_f, o_b, u, g_norm.reshape(1, GLA_HV), mods, w)


def _swa_kernel(sink_ref, q_ref, kvp_ref, kvo_ref, kvn_ref, kvc_ref, o_ref, kv_buf, *, seq_len):
    i = pl.program_id(0)
    half = TM // 2
    kv_buf[0:half] = kvp_ref[...]
    kv_buf[half:half + TM] = kvo_ref[...]
    kv_buf[half + TM:2 * TM] = kvn_ref[...]
    kv_buf[2 * TM:3 * TM] = kvc_ref[...]
    nk = 3 * TM
    qpos = i * TM + lax.broadcasted_iota(jnp.int32, (TM, nk), 0)
    col = lax.broadcasted_iota(jnp.int32, (TM, nk), 1)
    kpos = i * TM - half + col
    local = ((jnp.abs(kpos - qpos) <= SWA_WINDOW) & (kpos >= 0) & (kpos < seq_len) & (qpos < seq_len))
    valid = local | (col >= 2 * TM)
    kv_w = SWA_KV_HEADS * HEAD_DIM
    group = SWA_HEADS // SWA_KV_HEADS
    for h in range(SWA_HEADS):
        kv = h // group
        qh = q_ref[:, h * HEAD_DIM:(h + 1) * HEAD_DIM]
        kh = kv_buf[:, kv * HEAD_DIM:(kv + 1) * HEAD_DIM]
        vh = kv_buf[:, kv_w + kv * HEAD_DIM:kv_w + (kv + 1) * HEAD_DIM]
        s = lax.dot_general(qh, kh, _NT, preferred_element_type=F32)
        s = jnp.where(valid, s, NEG)
        sink = sink_ref[h] * LOG2E
        m = jnp.maximum(jnp.max(s, axis=-1, keepdims=True), sink)
        p = jnp.exp2(s - m)
        l = jnp.sum(p, axis=-1, keepdims=True) + jnp.exp2(sink - m)
        o = jnp.dot(p.astype(BF16), vh, preferred_element_type=F32) / l
        o_ref[:, h * HEAD_DIM:(h + 1) * HEAD_DIM] = o.astype(o_ref.dtype)


def _swa_attention(u, sinks, *, seq_len):
    rws = u.shape[0]
    nt = rws // TM
    half = TM // 2
    n_half = rws // half
    kv_w = 2 * SWA_KV_HEADS * HEAD_DIM
    qw = SWA_HEADS * HEAD_DIM
    kvc = qw // kv_w
    return pl.pallas_call(
        functools.partial(_swa_kernel, seq_len=seq_len),
        out_shape=jax.ShapeDtypeStruct((rws, qw), BF16),
        grid=(nt,),
        in_specs=[pl.BlockSpec(memory_space=pltpu.SMEM),
                  pl.BlockSpec((TM, qw), lambda i: (i, 0)),
                  pl.BlockSpec((half, kv_w), lambda i: (jnp.maximum(2 * i - 1, 0), kvc)),
                  pl.BlockSpec((TM, kv_w), lambda i: (i, kvc)),
                  pl.BlockSpec((half, kv_w), lambda i: (jnp.minimum(2 * i + 2, n_half - 1), kvc)),
                  pl.BlockSpec((TM, kv_w), lambda i: (nt - 1, kvc))],
        out_specs=pl.BlockSpec((TM, qw), lambda i: (i, 0)),
        scratch_shapes=[pltpu.VMEM((3 * TM, kv_w), BF16)],
        compiler_params=_params("parallel"), name="swa_attention",
    )(sinks, u, u, u, u, u)


def _na_kernel(q_ref, kp_ref, ko_ref, kn_ref, kc_ref, vp_ref, vo_ref, vn_ref, vc_ref, bias_ref, o_ref,
               k_buf, v_buf, *, n_rows):
    i = pl.program_id(0)
    for j, (kr, vr) in enumerate(((kp_ref, vp_ref), (ko_ref, vo_ref), (kn_ref, vn_ref), (kc_ref, vc_ref))):
        k_buf[j * TM:(j + 1) * TM] = kr[...]
        v_buf[j * TM:(j + 1) * TM] = vr[...]
    n_loc = 3 * TM
    rpt = TM // GRID_W
    qi = lax.broadcasted_iota(jnp.int32, (TM, n_loc), 0)
    ki = lax.broadcasted_iota(jnp.int32, (TM, n_loc), 1)
    r = i * rpt + qi // GRID_W
    c = qi % GRID_W
    krow = (i - 1) * rpt + ki // GRID_W
    kcol = ki % GRID_W
    r0 = jnp.clip(r - NA_KH // 2, 0, n_rows - NA_KH)
    c0 = jnp.clip(c - NA_KW // 2, 0, GRID_W - NA_KW)
    valid = ((krow >= r0) & (krow < r0 + NA_KH) & (kcol >= c0) & (kcol < c0 + NA_KW) & (r < n_rows))
    for h in range(NA_HEADS):
        cols = slice(h * HEAD_DIM, (h + 1) * HEAD_DIM)
        qh = q_ref[:, cols]
        s = lax.dot_general(qh, k_buf[:, cols], _NT, preferred_element_type=F32)
        s_loc = jnp.where(valid, s[:, :n_loc] + bias_ref[h], NEG)
        s_ctx = s[:, n_loc:]
        m = jnp.maximum(jnp.max(s_loc, axis=-1, keepdims=True), jnp.max(s_ctx, axis=-1, keepdims=True))
        p_loc = jnp.exp2(s_loc - m)
        p_ctx = jnp.exp2(s_ctx - m)
        l = jnp.sum(p_loc, axis=-1, keepdims=True) + jnp.sum(p_ctx, axis=-1, keepdims=True)
        o = (jnp.dot(p_loc.astype(BF16), v_buf[0:n_loc, cols], preferred_element_type=F32)
             + jnp.dot(p_ctx.astype(BF16), v_buf[n_loc:, cols], preferred_element_type=F32)) / l
        o_ref[:, cols] = o.astype(o_ref.dtype)


def _na_bias_kernel(t_ref, o_ref, *, lo):
    rpt = TM // GRID_W
    for a in range(rpt):
        for b in range(3 * rpt):
            o_ref[0, a * GRID_W:(a + 1) * GRID_W, b * GRID_W:(b + 1) * GRID_W] = t_ref[0, lo - a + b]


def _na_bias_table(rpb):
    n_heads = rpb.shape[0]
    rpt = TM // GRID_W
    lo = NA_KH - 1 - rpt
    assert lo - (rpt - 1) >= 0 and lo + 3 * rpt <= 2 * NA_KH - 1
    c = jnp.arange(GRID_W)
    col_off = c[None, :] - c[:, None] + NA_KW - 1
    onehot = (col_off[None] == jnp.arange(2 * NA_KW - 1)[:, None, None]).astype(F32)
    tcol = jnp.einsum('hrj,jck->hrck', rpb.astype(F32) * LOG2E, onehot, precision=HIGHEST)
    return pl.pallas_call(
        functools.partial(_na_bias_kernel, lo=lo),
        out_shape=jax.ShapeDtypeStruct((n_heads, TM, 3 * TM), F32),
        grid=(n_heads,),
        in_specs=[pl.BlockSpec((1, 2 * NA_KH - 1, GRID_W, GRID_W), lambda h: (h, 0, 0, 0))],
        out_specs=pl.BlockSpec((1, TM, 3 * TM), lambda h: (h, 0, 0)),
        compiler_params=_params("parallel"), name="na_bias_table",
    )(tcol)


def _na_attention(u, bias, *, n_rows):
    rws = u.shape[0]
    nt = rws // TM
    d = NA_HEADS * HEAD_DIM
    blk = lambda f, c: pl.BlockSpec((TM, d), lambda i: (f(i), c))
    prev = lambda i: jnp.maximum(i - 1, 0)
    own = lambda i: i
    nxt = lambda i: jnp.minimum(i + 1, nt - 1)
    ctx = lambda i: nt - 1
    return pl.pallas_call(
        functools.partial(_na_kernel, n_rows=n_rows),
        out_shape=jax.ShapeDtypeStruct((rws, d), BF16),
        grid=(nt,),
        in_specs=[blk(own, 0), blk(prev, 1), blk(own, 1), blk(nxt, 1), blk(ctx, 1),
                  blk(prev, 2), blk(own, 2), blk(nxt, 2), blk(ctx, 2),
                  pl.BlockSpec((NA_HEADS, TM, 3 * TM), lambda i: (0, 0, 0))],
        out_specs=pl.BlockSpec((TM, d), lambda i: (i, 0)),
        scratch_shapes=[pltpu.VMEM((4 * TM, d), BF16), pltpu.VMEM((4 * TM, d), BF16)],
        compiler_params=_params("parallel"), name="na_attention",
    )(u, u, u, u, u, u, u, u, u, bias)


def _router_kernel(x_ref, g_ref, mod_ref, wr_ref, br_ref, e_ref, w_ref, rank_ref, cnt_ref, carry):
    @pl.when(pl.program_id(0) == 0)
    def _():
        carry[...] = jnp.zeros_like(carry)

    h = _norm_mod(x_ref[...], g_ref[...], mod_ref[0, 4:5, :], mod_ref[0, 3:4, :])
    logits = lax.dot_general(wr_ref[...], h, _NT, precision=HIGHEST, preferred_element_type=F32)
    scores = jax.nn.sigmoid(logits)
    biased = scores + br_ref[...]
    row = lambda a, e: a[e:e + 1, :]
    best_g = jnp.zeros((1, TM), jnp.int32)
    best_s = None
    for g in range(N_GROUPS):
        v = [row(biased, g * EXPERTS_PER_GROUP + j) for j in range(EXPERTS_PER_GROUP)]
        gs = None
        for a in range(EXPERTS_PER_GROUP):
            for b in range(a + 1, EXPERTS_PER_GROUP):
                pair = v[a] + v[b]
                gs = pair if gs is None else jnp.maximum(gs, pair)
        if best_s is None:
            best_s = gs
        else:
            better = gs > best_s
            best_g = jnp.where(better, g, best_g)
            best_s = jnp.where(better, gs, best_s)
    picks = []
    for _ in range(2):
        top_v = jnp.full((1, TM), -jnp.inf, F32)
        top_i = jnp.full((1, TM), -1, jnp.int32)
        for e in range(N_EXPERTS):
            ok = best_g == (e // EXPERTS_PER_GROUP)
            for p in picks:
                ok = jnp.logical_and(ok, p != e)
            cand = jnp.where(ok, row(biased, e), -jnp.inf)
            better = cand > top_v
            top_i = jnp.where(better, e, top_i)
            top_v = jnp.where(better, cand, top_v)
        picks.append(top_i)
    e_iota = lax.broadcasted_iota(jnp.int32, (N_EXPERTS, TM), 0)
    sel0 = e_iota == picks[0]
    sel1 = e_iota == picks[1]
    w0 = jnp.sum(jnp.where(sel0, scores, 0.0), axis=0, keepdims=True)
    w1 = jnp.sum(jnp.where(sel1, scores, 0.0), axis=0, keepdims=True)
    tot = w0 + w1
    e_ref[0:1, :] = picks[0]
    e_ref[1:2, :] = picks[1]
    w_ref[0:1, :] = w0 / tot
    w_ref[1:2, :] = w1 / tot
    member = jnp.where(jnp.logical_or(sel0, sel1), 1.0, 0.0)
    earlier = (lax.broadcasted_iota(jnp.int32, (TM, TM), 0)
               < lax.broadcasted_iota(jnp.int32, (TM, TM), 1)).astype(BF16)
    before = carry[:, 0:1] + jnp.dot(member.astype(BF16), earlier, preferred_element_type=F32)
    rank_ref[0:1, :] = jnp.sum(jnp.where(sel0, before, 0.0), axis=0, keepdims=True).astype(jnp.int32)
    rank_ref[1:2, :] = jnp.sum(jnp.where(sel1, before, 0.0), axis=0, keepdims=True).astype(jnp.int32)
    carry[...] = carry[...] + jnp.sum(member, axis=1, keepdims=True)
    cnt_ref[...] = carry[...].astype(jnp.int32)


def _router(x, g, mods, w_router_t, b_router, *, n_x_tiles):
    r, d = x.shape
    pair = pl.BlockSpec((2, TM), lambda i: (0, i))
    return pl.pallas_call(
        _router_kernel,
        out_shape=[jax.ShapeDtypeStruct((2, r), jnp.int32), jax.ShapeDtypeStruct((2, r), F32),
                   jax.ShapeDtypeStruct((2, r), jnp.int32), jax.ShapeDtypeStruct((N_EXPERTS, LANES), jnp.int32)],
        grid=(r // TM,),
        in_specs=[pl.BlockSpec((TM, d), lambda i: (i, 0)),
                  pl.BlockSpec((1, d), lambda i: (0, 0)),
                  pl.BlockSpec((1, 6, d), lambda i: (i // n_x_tiles, 0, 0)),
                  pl.BlockSpec((N_EXPERTS, d), lambda i: (0, 0)),
                  pl.BlockSpec((N_EXPERTS, 1), lambda i: (0, 0))],
        out_specs=[pair, pair, pair, pl.BlockSpec((N_EXPERTS, LANES), lambda i: (0, 0))],
        scratch_shapes=[pltpu.VMEM((N_EXPERTS, LANES), F32)],
        compiler_params=_params("arbitrary"), name="router",
    )(x, g.reshape(1, d), mods, w_router_t, b_router.reshape(N_EXPERTS, 1))


def _dispatch_plan(e01, rank01, counts, n_blocks):
    cnt = counts[:, 0]
    nblk = (cnt + MOE_BLOCK - 1) // MOE_BLOCK
    blk_end = jnp.cumsum(nblk)
    off = (blk_end - nblk) * MOE_BLOCK
    eidx = jnp.arange(N_EXPERTS, dtype=jnp.int32)[:, None, None]
    pos = rank01 + jnp.sum(jnp.where(e01[None] == eidx, off[:, None, None], 0), axis=0)
    blk_e = jnp.sum(jnp.arange(n_blocks, dtype=jnp.int32)[:, None] >= blk_end[None, :], axis=1)
    blk_e = jnp.minimum(blk_e, N_EXPERTS - 1).astype(jnp.int32)
    n_used = blk_end[-1:]
    fill = jnp.concatenate([off + cnt, nblk * MOE_BLOCK - cnt, n_used]).astype(jnp.int32)
    slab_row = (pos * CHUNKS).reshape(-1).astype(jnp.int32)
    return slab_row, blk_e, n_used.astype(jnp.int32), fill


def _zero_fill(fill_ref, zbuf, xs_ref, zsem, n_blocks, wait):
    def copy(start_row, n_rows):
        cp = pltpu.make_async_copy(zbuf.at[pl.ds(0, n_rows * CHUNKS)],
                                   xs_ref.at[pl.ds(pl.multiple_of(start_row * CHUNKS, CHUNKS), n_rows * CHUNKS)],
                                   zsem)
        cp.wait() if wait else cp.start()

    for e in range(N_EXPERTS):
        start, length = fill_ref[e], fill_ref[N_EXPERTS + e]
        piece = MOE_BLOCK // 2
        while piece >= 1:
            @pl.when((length & piece) != 0)
            def _():
                copy(start + (length & (-2 * piece)), piece)
            piece //= 2

    def body(j, carry):
        for part in range(MOE_BLOCK // TM):
            copy(j * MOE_BLOCK + part * TM, TM)
        return carry

    lax.fori_loop(fill_ref[2 * N_EXPERTS], n_blocks, body, 0)


def _to_slabs(ref, base, x):
    n = x.shape[0]
    for c in range(CHUNKS):
        ref[pl.ds(base + c, n, stride=CHUNKS), :] = x[:, c * LANES:(c + 1) * LANES]


def _from_slabs(ref, base, n):
    return jnp.concatenate([ref[pl.ds(base + c, n, stride=CHUNKS), :] for c in range(CHUNKS)], axis=1)


def _dispatch_kernel(pos_ref, fill_ref, x_ref, g_ref, mod_ref, xs_ref, h_scr, zbuf, sem, zsem, *,
                     rows, nt, n_blocks):
    i = pl.program_id(0)
    slot = i % 2
    slab_rows = TM * CHUNKS

    @pl.when(i == 0)
    def _():
        zbuf[...] = jnp.zeros_like(zbuf)
        _zero_fill(fill_ref, zbuf, xs_ref, zsem, n_blocks, wait=False)

    def wait_slot(s):
        for _ in range(2):
            pltpu.make_async_copy(h_scr.at[pl.ds(pl.multiple_of(s * slab_rows, slab_rows), slab_rows)],
                                  xs_ref.at[pl.ds(0, slab_rows)], sem.at[s]).wait()

    @pl.when(i >= 2)
    def _():
        wait_slot(slot)

    h = _norm_mod(x_ref[...], g_ref[...], mod_ref[0, 4:5, :], mod_ref[0, 3:4, :])
    _to_slabs(h_scr, slot * slab_rows, h)
    base = i * TM

    def body(grp, carry):
        for u in range(SUBLANES):
            t = grp * SUBLANES + u
            src = h_scr.at[pl.ds(pl.multiple_of(slot * slab_rows + t * CHUNKS, CHUNKS), CHUNKS)]
            for k in range(2):
                p = pl.multiple_of(pos_ref[k * rows + base + t], CHUNKS)
                pltpu.make_async_copy(src, xs_ref.at[pl.ds(p, CHUNKS)], sem.at[slot]).start(priority=k)
        return carry

    lax.fori_loop(0, TM // SUBLANES, body, 0)

    @pl.when(i == nt - 1)
    def _():
        wait_slot(slot)
        if nt > 1:
            wait_slot(1 - slot)
        _zero_fill(fill_ref, zbuf, xs_ref, zsem, n_blocks, wait=True)


def _dispatch(pos, fill, x, g, mods, n_blocks, *, n_x_tiles):
    r, d = x.shape
    nt = r // TM
    assert MOE_BLOCK % TM == 0 and MOE_BLOCK // 2 <= TM and d == CHUNKS * LANES
    n_rows = n_blocks * MOE_BLOCK
    return pl.pallas_call(
        functools.partial(_dispatch_kernel, rows=r, nt=nt, n_blocks=n_blocks),
        out_shape=jax.ShapeDtypeStruct((n_rows * CHUNKS, LANES), F32),
        grid_spec=pltpu.PrefetchScalarGridSpec(
            num_scalar_prefetch=2, grid=(nt,),
            in_specs=[pl.BlockSpec((TM, d), lambda i, p, f: (i, 0)),
                      pl.BlockSpec((1, d), lambda i, p, f: (0, 0)),
                      pl.BlockSpec((1, 6, d), lambda i, p, f: (i // n_x_tiles, 0, 0))],
            out_specs=pl.BlockSpec(memory_space=pl.ANY),
            scratch_shapes=[pltpu.VMEM((2 * TM * CHUNKS, LANES), F32),
                            pltpu.VMEM((TM * CHUNKS, LANES), F32),
                            pltpu.SemaphoreType.DMA((2,)), pltpu.SemaphoreType.DMA]),
        compiler_params=_params("arbitrary"), name="moe_dispatch",
    )(pos, fill, x, g.reshape(1, d), mods)


def _expert_kernel(be_ref, nu_ref, xs_ref, wg_ref, wu_ref, wd_ref, ys_ref, wg_s, wu_s, wd_s):
    j = pl.program_id(0)

    @pl.when(j < nu_ref[0])
    def _():
        @pl.when(jnp.logical_or(j == 0, be_ref[j] != be_ref[jnp.maximum(j - 1, 0)]))
        def _():
            wg_s[...] = wg_ref[0, 0].astype(BF16)
            wu_s[...] = wu_ref[0, 0].astype(BF16)
            wd_s[...] = wd_ref[0, 0].astype(BF16)

        h = _from_slabs(xs_ref, 0, MOE_BLOCK).astype(BF16)
        a = (_silu(jnp.dot(h, wg_s[...], preferred_element_type=F32))
             * jnp.dot(h, wu_s[...], preferred_element_type=F32))
        _to_slabs(ys_ref, 0, jnp.dot(a.astype(BF16), wd_s[...], preferred_element_type=F32))

    @pl.when(j >= nu_ref[0])
    def _():
        ys_ref[...] = jnp.zeros_like(ys_ref)


def _experts(blk_e, n_used, xs, wg, wu, wd, layer):
    d, de = wg.shape[-2:]
    blk = MOE_BLOCK * CHUNKS
    last = lambda j, nu: jnp.minimum(j, jnp.maximum(nu[0] - 1, 0))
    return pl.pallas_call(
        _expert_kernel,
        out_shape=jax.ShapeDtypeStruct(xs.shape, F32),
        grid_spec=pltpu.PrefetchScalarGridSpec(
            num_scalar_prefetch=2, grid=(xs.shape[0] // blk,),
            in_specs=[pl.BlockSpec((blk, LANES), lambda j, be, nu: (last(j, nu), 0)),
                      pl.BlockSpec((1, 1, d, de), lambda j, be, nu: (layer, be[last(j, nu)], 0, 0)),
                      pl.BlockSpec((1, 1, d, de), lambda j, be, nu: (layer, be[last(j, nu)], 0, 0)),
                      pl.BlockSpec((1, 1, de, d), lambda j, be, nu: (layer, be[last(j, nu)], 0, 0))],
            out_specs=pl.BlockSpec((blk, LANES), lambda j, be, nu: (j, 0)),
            scratch_shapes=[pltpu.VMEM((d, de), BF16), pltpu.VMEM((d, de), BF16), pltpu.VMEM((de, d), BF16)]),
        compiler_params=_params("arbitrary"), name="moe_experts",
    )(blk_e, n_used, xs, wg, wu, wd)


def _combine_kernel(pos_ref, x_ref, mod_ref, w_ref, ys_ref, *rest, rows, nt, final):
    if final:
        gf_ref, o_ref, buf, sem = rest
    else:
        o_ref, buf, sem = rest
    i = pl.program_id(0)
    slot = i % 2
    slab_rows = TM * CHUNKS
    where = lambda s, k: (s * 2 + k) * slab_rows

    def issue(row_tile, s):
        base = row_tile * TM

        def body(grp, carry):
            for u in range(SUBLANES):
                t = grp * SUBLANES + u
                for k in range(2):
                    p = pl.multiple_of(pos_ref[k * rows + base + t], CHUNKS)
                    dst = buf.at[pl.ds(pl.multiple_of(where(s, k) + t * CHUNKS, CHUNKS), CHUNKS)]
                    pltpu.make_async_copy(ys_ref.at[pl.ds(p, CHUNKS)], dst, sem.at[s]).start(priority=k)
            return carry

        lax.fori_loop(0, TM // SUBLANES, body, 0)

    @pl.when(i == 0)
    def _():
        issue(0, 0)

    @pl.when(i + 1 < nt)
    def _():
        issue(i + 1, 1 - slot)

    for k in range(2):
        pltpu.make_async_copy(ys_ref.at[pl.ds(0, slab_rows)],
                              buf.at[pl.ds(pl.multiple_of(where(slot, k), slab_rows), slab_rows)],
                              sem.at[slot]).wait()
    w = w_ref[...]
    y = w[:, 0:1] * _from_slabs(buf, where(slot, 0), TM) + w[:, 1:2] * _from_slabs(buf, where(slot, 1), TM)
    x = x_ref[...] + mod_ref[0, 5:6, :] * y
    if final:
        x = x * lax.rsqrt(jnp.mean(x * x, axis=-1, keepdims=True) + EPS) * gf_ref[...]
    o_ref[...] = x


def _combine(pos, x, mods, w_cols, ys, *, n_x_tiles, g_final=None):
    r, d = x.shape
    final = g_final is not None
    nt = n_x_tiles if final else r // TM
    in_specs = [pl.BlockSpec((TM, d), lambda i, p: (i, 0)),
                pl.BlockSpec((1, 6, d), lambda i, p: (i // n_x_tiles, 0, 0)),
                pl.BlockSpec((TM, 2), lambda i, p: (i, 0)),
                pl.BlockSpec(memory_space=pl.ANY)]
    args = [pos, x, mods, w_cols, ys]
    if final:
        in_specs.append(pl.BlockSpec((1, d), lambda i, p: (0, 0)))
        args.append(g_final.reshape(1, d))
    return pl.pallas_call(
        functools.partial(_combine_kernel, rows=r, nt=nt, final=final),
        out_shape=jax.ShapeDtypeStruct((nt * TM, d), F32),
        grid_spec=pltpu.PrefetchScalarGridSpec(
            num_scalar_prefetch=1, grid=(nt,), in_specs=in_specs,
            out_specs=pl.BlockSpec((TM, d), lambda i, p: (i, 0)),
            scratch_shapes=[pltpu.VMEM((2 * 2 * TM * CHUNKS, LANES), F32), pltpu.SemaphoreType.DMA((2,))]),
        compiler_params=_params("arbitrary"), name="moe_combine",
    )(*args)


def _moe(x, g, mods, w_router_t, b_router, wg, wu, wd, layer, *, n_x_tiles, g_final=None):
    r = x.shape[0]
    n_blocks = -(-2 * r // MOE_BLOCK) + N_EXPERTS
    e01, w01, rank01, counts = _router(x, g, mods, w_router_t, b_router, n_x_tiles=n_x_tiles)
    pos, blk_e, n_used, fill = _dispatch_plan(e01, rank01, counts, n_blocks)
    xs = _dispatch(pos, fill, x, g, mods, n_blocks, n_x_tiles=n_x_tiles)
    ys = _experts(blk_e, n_used, xs, wg, wu, wd, layer)
    return _combine(pos, x, mods, w01.T, ys, n_x_tiles=n_x_tiles, g_final=g_final)


def _rope_tables(seq_len, n_rows_total):
    half = HEAD_DIM // 2
    inv_freq = ROPE_BASE ** (-jnp.arange(0, half, 2, dtype=F32) / half)
    first = jnp.arange(half) < (half // 2)

    def tab(n):
        ang = jnp.arange(n, dtype=F32)[:, None] * inv_freq
        ang = jnp.concatenate([ang, ang], axis=-1)
        c, s = jnp.cos(ang), jnp.sin(ang)
        return jnp.stack([c, jnp.where(first, -s, 0.0), jnp.where(first, 0.0, s)], axis=1)

    ident = jnp.stack([jnp.ones((half,), F32), jnp.zeros((half,), F32), jnp.zeros((half,), F32)])
    reps = LANES // HEAD_DIM
    n_ctx_rows = (n_rows_total - seq_len) // GRID_W
    row = jnp.concatenate([tab(seq_len // GRID_W), jnp.broadcast_to(ident, (n_ctx_rows, 3, half))], axis=0)
    row = jnp.tile(jnp.concatenate([row, jnp.zeros_like(row)], axis=-1), (1, 1, reps))
    col = jnp.stack([tab(GRID_W), jnp.broadcast_to(ident, (GRID_W, 3, half))])
    col = jnp.tile(jnp.concatenate([jnp.zeros_like(col), col], axis=-1), (1, TM // GRID_W, 1, reps))
    return row, col.transpose(0, 2, 1, 3)


def kernel(x, c, ctx, c_ctx, w_mod, b_mod, g_mix, g_ffn, g_final, conv_w_in, conv_w, conv_w_out,
           gla_w_proj, gla_w_gk1, gla_w_gk2, gla_b_gk, gla_g_norm, gla_w_out, swa_w_qkv, swa_sinks,
           swa_w_out, na_w_qkv, na_rpb, na_w_out, router_w, router_b, moe_w_gate, moe_w_up, moe_w_down):
    seq_len, d = x.shape[1], x.shape[2]
    ctx_len = ctx.shape[1]
    assert x.shape[0] == 1 and ctx_len == TM and seq_len % TM == 0 and d == D_MODEL
    assert seq_len % GRID_W == 0 and seq_len // GRID_W >= NA_KH
    n_x_tiles = seq_len // TM
    rows = seq_len + ctx_len
    xs = jnp.concatenate([x[0], ctx[0]], axis=0)
    mods_all = _ada_params(c, c_ctx, w_mod, b_mod)
    w_router_t = router_w.T.astype(F32)
    n_mixers = 4

    for i in range(DEPTH):
        kind, j = i % n_mixers, i // n_mixers
        mods = mods_all[i]
        if kind == 0:
            u = _project(xs, g_mix[i], mods, conv_w_in[j].astype(BF16), n_x_tiles=n_x_tiles)
            xs = _conv_mixer(xs, u, conv_w[j], mods, conv_w_out[j].astype(BF16), n_x_tiles=n_x_tiles)
        elif kind == 1:
            n_extra = LANES
            w_low = jnp.concatenate([gla_w_gk1[j, 0], gla_w_gk1[j, 1],
                                     jnp.zeros((d, n_extra - 2 * GLA_RANK), F32)], axis=1).astype(BF16)
            u, r_low = _project(xs, g_mix[i], mods, gla_w_proj[j].astype(BF16), n_x_tiles=n_x_tiles,
                                w_extra=w_low)
            w2pad = jnp.stack(
                [jnp.zeros((n_extra, GLA_DK), F32).at[k * GLA_RANK:(k + 1) * GLA_RANK].set(gla_w_gk2[j, k])
                 for k in range(2)])
            o_f, o_b = _gla_scan(u, r_low, w2pad, gla_b_gk[j])
            xs = _gla_out(xs, o_f, o_b, u, gla_g_norm[j], mods, gla_w_out[j].astype(BF16), n_x_tiles=n_x_tiles)
        elif kind == 2:
            rope = _rope_tables(seq_len, rows)
            u = _project(xs, g_mix[i], mods, swa_w_qkv[j].astype(BF16), n_x_tiles=n_x_tiles,
                         q_cols=SWA_HEADS * HEAD_DIM, q_scale=HEAD_DIM ** -0.5 * LOG2E, rope=rope,
                         rope_cols=(SWA_HEADS + SWA_KV_HEADS) * HEAD_DIM)
            ypre = _swa_attention(u, swa_sinks[j], seq_len=seq_len)
            xs = _out_proj(xs, ypre, mods, swa_w_out[j].astype(BF16), n_x_tiles=n_x_tiles)
        else:
            u = _project(xs, g_mix[i], mods, na_w_qkv[j].astype(BF16), n_x_tiles=n_x_tiles,
                         q_cols=NA_HEADS * HEAD_DIM, q_scale=HEAD_DIM ** -0.5 * LOG2E)
            ypre = _na_attention(u, _na_bias_table(na_rpb[j]), n_rows=seq_len // GRID_W)
            xs = _out_proj(xs, ypre, mods, na_w_out[j].astype(BF16), n_x_tiles=n_x_tiles)
        xs = _moe(xs, g_ffn[i], mods, w_router_t, router_b, moe_w_gate, moe_w_up, moe_w_down, i,
                  n_x_tiles=n_x_tiles, g_final=g_final if i == DEPTH - 1 else None)
    return xs[None]
```

```python
import functools

import jax
import jax.numpy as jnp
from jax import lax
from jax.experimental import pallas as pl
from jax.experimental.pallas import tpu as pltpu

F32 = jnp.float32
BF16 = jnp.bfloat16
HIGHEST = lax.Precision.HIGHEST

D_MODEL = 1024
DEPTH = 4
GRID_W = 64
EPS = 1e-6
CONV_W = 3
GLA_HEADS = 4
GLA_DK = D_MODEL // 2
GLA_DV = D_MODEL
GLA_HK = GLA_DK // GLA_HEADS
GLA_HV = GLA_DV // GLA_HEADS
GLA_RANK = 16
GLA_NORMALIZER = 16.0
GLA_CHUNK = 64
HEAD_DIM = 64
SWA_HEADS = D_MODEL // HEAD_DIM
SWA_KV_HEADS = SWA_HEADS // 4
SWA_WINDOW = 128
ROPE_BASE = 10000.0
NA_HEADS = D_MODEL // HEAD_DIM
NA_KH = 8
NA_KW = 16
N_EXPERTS = 16
N_GROUPS = 4
EXPERTS_PER_GROUP = N_EXPERTS // N_GROUPS
D_EXPERT = D_MODEL // 2

TM = 256
MOE_BLOCK = 512
LANES = 128
SUBLANES = 8
CHUNKS = D_MODEL // LANES
NEG = -1e30
LOG2E = 1.4426950408889634
VMEM_LIMIT = 56 * 1024 * 1024

_NT = (((1,), (1,)), ((), ()))
_TN = (((0,), (0,)), ((), ()))


def _params(*sem):
    return pltpu.CompilerParams(dimension_semantics=sem, vmem_limit_bytes=VMEM_LIMIT)


def _norm_mod(x, g, scale, shift):
    ms = jnp.mean(x * x, axis=-1, keepdims=True)
    return (x * lax.rsqrt(ms + EPS) * g) * (1.0 + scale) + shift


def _silu(x):
    return x * jax.nn.sigmoid(x)


def _split2(x):
    hi = x.astype(BF16)
    return hi, (x - hi.astype(F32)).astype(BF16)


def _ada_kernel(cond_ref, w_ref, b_ref, o_ref):
    s = _silu(cond_ref[...])
    o_ref[0] = jnp.dot(s, w_ref[0], precision=HIGHEST, preferred_element_type=F32) + b_ref[0]


def _ada_params(c, c_ctx, w_mod, b_mod):
    depth, d, n = w_mod.shape
    nc = n // 4
    cond = jnp.zeros((8, d), F32).at[0].set(c[0]).at[1].set(c_ctx)
    out = pl.pallas_call(
        _ada_kernel,
        out_shape=jax.ShapeDtypeStruct((depth, 8, n), F32),
        grid=(depth, n // nc),
        in_specs=[pl.BlockSpec((8, d), lambda l, j: (0, 0)),
                  pl.BlockSpec((1, d, nc), lambda l, j: (l, 0, j)),
                  pl.BlockSpec((1, 1, nc), lambda l, j: (l, 0, j))],
        out_specs=pl.BlockSpec((1, 8, nc), lambda l, j: (l, 0, j)),
        compiler_params=_params("parallel", "parallel"),
        name="ada_params",
    )(cond, w_mod, b_mod.reshape(depth, 1, n))
    return out[:, :2].reshape(depth, 2, 6, d)


def _proj_kernel(x_ref, g_ref, mod_ref, w_ref, *rest, n_chunk, q_cols, q_scale, rope_cols, has_extra):
    n_main = w_ref.shape[1]
    if has_extra:
        wx_ref, rest = rest[0], rest[1:]
    if rope_cols:
        row_ref, col_ref = rest[:2]
        rest = rest[2:]
        rpt = TM // GRID_W
        cos_t, sa_t, sb_t = [
            jnp.concatenate([jnp.broadcast_to(row_ref[a, k:k + 1, :], (GRID_W, LANES)) for a in range(rpt)],
                            axis=0) + col_ref[0, k]
            for k in range(3)]
    o_ref = rest[0]
    h = _norm_mod(x_ref[...], g_ref[...], mod_ref[0, 1:2, :], mod_ref[0, 0:1, :]).astype(BF16)
    for c0 in range(0, n_main, n_chunk):
        c1 = min(c0 + n_chunk, n_main)
        y = jnp.dot(h, w_ref[:, c0:c1], preferred_element_type=F32)
        if c0 < q_cols:
            y = y * q_scale
        if c0 >= rope_cols:
            o_ref[:, c0:c1] = y.astype(o_ref.dtype)
            continue
        for s0 in range(c0, c1, LANES):
            ys = y[:, s0 - c0:s0 - c0 + LANES]
            if s0 < rope_cols:
                ys = ys * cos_t + pltpu.roll(ys, LANES - 16, 1) * sa_t + pltpu.roll(ys, 16, 1) * sb_t
            o_ref[:, s0:s0 + LANES] = ys.astype(o_ref.dtype)
    if has_extra:
        rest[1][...] = jnp.dot(h, wx_ref[...], preferred_element_type=F32)


def _project(x, g, mods, w, *, n_x_tiles, n_chunk=512, q_cols=0, q_scale=1.0, rope=None, rope_cols=0,
             w_extra=None):
    r, d = x.shape
    n_main = w.shape[1]
    n_extra = 0 if w_extra is None else w_extra.shape[1]
    nt = r // TM
    in_specs = [pl.BlockSpec((TM, d), lambda i: (i, 0)),
                pl.BlockSpec((1, d), lambda i: (0, 0)),
                pl.BlockSpec((1, 6, d), lambda i: (i // n_x_tiles, 0, 0)),
                pl.BlockSpec((d, n_main), lambda i: (0, 0))]
    args = [x, g.reshape(1, d), mods, w]
    if n_extra:
        in_specs.append(pl.BlockSpec((d, n_extra), lambda i: (0, 0)))
        args.append(w_extra)
    if rope_cols:
        rpt = TM // GRID_W
        in_specs += [pl.BlockSpec((rpt, 3, LANES), lambda i: (i, 0, 0)),
                     pl.BlockSpec((1, 3, TM, LANES), lambda i: (i // n_x_tiles, 0, 0, 0))]
        args += list(rope)
    out_shape = [jax.ShapeDtypeStruct((r, n_main), BF16)]
    out_specs = [pl.BlockSpec((TM, n_main), lambda i: (i, 0))]
    if n_extra:
        out_shape.append(jax.ShapeDtypeStruct((r, n_extra), F32))
        out_specs.append(pl.BlockSpec((TM, n_extra), lambda i: (i, 0)))
    out = pl.pallas_call(
        functools.partial(_proj_kernel, n_chunk=n_chunk, q_cols=q_cols, q_scale=q_scale,
                          rope_cols=rope_cols, has_extra=bool(n_extra)),
        out_shape=out_shape, grid=(nt,), in_specs=in_specs, out_specs=out_specs,
        compiler_params=_params("parallel"), name="norm_proj",
    )(*args)
    return out if n_extra else out[0]


def _out_proj_kernel(x_ref, y_ref, mod_ref, w_ref, o_ref):
    y = jnp.dot(y_ref[...], w_ref[...], preferred_element_type=F32)
    o_ref[...] = x_ref[...] + mod_ref[0, 2:3, :] * y


def _out_proj(x, ypre, mods, w, *, n_x_tiles):
    r, d = x.shape
    return pl.pallas_call(
        _out_proj_kernel,
        out_shape=jax.ShapeDtypeStruct((r, d), F32),
        grid=(r // TM,),
        in_specs=[pl.BlockSpec((TM, d), lambda i: (i, 0)),
                  pl.BlockSpec((TM, d), lambda i: (i, 0)),
                  pl.BlockSpec((1, 6, d), lambda i: (i // n_x_tiles, 0, 0)),
                  pl.BlockSpec((d, d), lambda i: (0, 0))],
        out_specs=pl.BlockSpec((TM, d), lambda i: (i, 0)),
        compiler_params=_params("parallel"), name="out_proj",
    )(x, ypre, mods, w)


def _conv_kernel(x_ref, bg_ref, cg_ref, v_ref, cgp_ref, vp_ref, cgn_ref, vn_ref, cw_ref, mod_ref, w_ref,
                 o_ref, *, n_x_tiles):
    i = pl.program_id(0)
    has_prev = jnp.logical_and(i != 0, i != n_x_tiles).astype(F32)
    has_next = jnp.logical_and(i != n_x_tiles - 1, i != n_x_tiles).astype(F32)
    z = cg_ref[...].astype(F32) * v_ref[...].astype(F32)
    z_prev = cgp_ref[7:8, :].astype(F32) * vp_ref[7:8, :].astype(F32) * has_prev
    z_next = cgn_ref[0:1, :].astype(F32) * vn_ref[0:1, :].astype(F32) * has_next
    row = lax.broadcasted_iota(jnp.int32, z.shape, 0)
    z_dn = jnp.where(row == 0, z_prev, pltpu.roll(z, 1, 0))
    z_up = jnp.where(row == TM - 1, z_next, pltpu.roll(z, TM - 1, 0))
    conv = z_dn * cw_ref[0:1, :] + z * cw_ref[1:2, :] + z_up * cw_ref[2:3, :]
    ypre = (bg_ref[...].astype(F32) * conv).astype(BF16)
    y = jnp.dot(ypre, w_ref[...], preferred_element_type=F32)
    o_ref[...] = x_ref[...] + mod_ref[0, 2:3, :] * y


def _conv_mixer(x, u, conv_w, mods, w_out, *, n_x_tiles):
    r, d = x.shape
    n8 = r // 8
    t8 = TM // 8
    tile = lambda c: pl.BlockSpec((TM, d), lambda i: (i, c))
    prev = lambda c: pl.BlockSpec((8, d), lambda i: (jnp.maximum(i * t8 - 1, 0), c))
    nxt = lambda c: pl.BlockSpec((8, d), lambda i: (jnp.minimum((i + 1) * t8, n8 - 1), c))
    return pl.pallas_call(
        functools.partial(_conv_kernel, n_x_tiles=n_x_tiles),
        out_shape=jax.ShapeDtypeStruct((r, d), F32),
        grid=(r // TM,),
        in_specs=[pl.BlockSpec((TM, d), lambda i: (i, 0)), tile(0), tile(1), tile(2),
                  prev(1), prev(2), nxt(1), nxt(2),
                  pl.BlockSpec((CONV_W, d), lambda i: (0, 0)),
                  pl.BlockSpec((1, 6, d), lambda i: (i // n_x_tiles, 0, 0)),
                  pl.BlockSpec((d, d), lambda i: (0, 0))],
        out_specs=pl.BlockSpec((TM, d), lambda i: (i, 0)),
        compiler_params=_params("parallel"), name="conv_mixer",
    )(x, u, u, u, u, u, u, u, conv_w, mods, w_out)


def _log_sigmoid(z):
    return jnp.minimum(z, 0.0) - jnp.log1p(jnp.exp(-jnp.abs(z)))


def _gla_chunk(q_ref, k_ref, v_ref, gk, o_ref, st_ref, d, c, reverse):
    C = GLA_CHUNK
    rows = slice(c * C, (c + 1) * C)
    ii = lax.broadcasted_iota(jnp.int32, (C, C), 0)
    jj = lax.broadcasted_iota(jnp.int32, (C, C), 1)
    causal = (jj >= ii) if reverse else (jj <= ii)
    ones = jnp.where(causal, 1.0, 0.0).astype(BF16)
    g_hi, g_lo = _split2(gk[rows])
    bcum = (jnp.dot(ones, g_hi, preferred_element_type=F32)
            + jnp.dot(ones, g_lo, preferred_element_type=F32))
    total = bcum[0:1, :] if reverse else bcum[C - 1:C, :]
    q = q_ref[rows, :].astype(F32) * (GLA_HK ** -0.5)
    k = k_ref[rows, :].astype(F32)
    q_dec = (q * jnp.exp(bcum)).astype(BF16)
    k_inv = (k * jnp.exp(-bcum)).astype(BF16)
    k_end = (k * jnp.exp(total - bcum)).astype(BF16)
    decay = jnp.exp(total)
    for h in range(GLA_HEADS):
        kcols = slice(h * GLA_HK, (h + 1) * GLA_HK)
        vcols = slice(h * GLA_HV, (h + 1) * GLA_HV)
        vh = v_ref[rows, vcols]
        att = lax.dot_general(q_dec[:, kcols], k_inv[:, kcols], _NT, preferred_element_type=F32)
        att = jnp.where(causal, att, 0.0).astype(BF16)
        st = st_ref[d, h]
        o = (jnp.dot(att, vh, preferred_element_type=F32)
             + lax.dot_general(q_dec[:, kcols], st.astype(BF16), _NT, preferred_element_type=F32))
        st_ref[d, h] = st * decay[:, kcols] + lax.dot_general(vh, k_end[:, kcols], _TN,
                                                              preferred_element_type=F32)
        o_ref[rows, vcols] = o.astype(o_ref.dtype)


def _gla_kernel(qf_ref, kf_ref, vf_ref, rf_ref, qb_ref, kb_ref, vb_ref, rb_ref, w2_ref, b_ref,
                of_ref, ob_ref, st_ref):
    @pl.when(pl.program_id(0) == 0)
    def _():
        st_ref[...] = jnp.zeros_like(st_ref)

    def gates(r_ref, d):
        z = jnp.dot(r_ref[...].astype(BF16), w2_ref[d], preferred_element_type=F32) + b_ref[d]
        return _log_sigmoid(z) * (1.0 / GLA_NORMALIZER)

    gk_f, gk_b = gates(rf_ref, 0), gates(rb_ref, 1)
    n_chunks = TM // GLA_CHUNK
    for c in range(n_chunks):
        _gla_chunk(qf_ref, kf_ref, vf_ref, gk_f, of_ref, st_ref, 0, c, False)
        _gla_chunk(qb_ref, kb_ref, vb_ref, gk_b, ob_ref, st_ref, 1, n_chunks - 1 - c, True)


def _gla_scan(u, r_low, w2pad, b_gk):
    rws = u.shape[0]
    nt = rws // TM
    fwd = lambda s: (s + nt - 1) % nt
    bwd = lambda s: nt - 1 - s
    qkvr = lambda order: [pl.BlockSpec((TM, GLA_DK), lambda s: (order(s), 0)),
                          pl.BlockSpec((TM, GLA_DK), lambda s: (order(s), 1)),
                          pl.BlockSpec((TM, GLA_DV), lambda s: (order(s), 1)),
                          pl.BlockSpec((TM, LANES), lambda s: (order(s), 0))]
    out = jax.ShapeDtypeStruct((rws, GLA_DV), BF16)
    return pl.pallas_call(
        _gla_kernel,
        out_shape=[out, out],
        grid=(nt,),
        in_specs=qkvr(fwd) + qkvr(bwd) + [pl.BlockSpec((2, LANES, GLA_DK), lambda s: (0, 0, 0)),
                                          pl.BlockSpec((2, 1, GLA_DK), lambda s: (0, 0, 0))],
        out_specs=[pl.BlockSpec((TM, GLA_DV), lambda s: (fwd(s), 0)),
                   pl.BlockSpec((TM, GLA_DV), lambda s: (bwd(s), 0))],
        scratch_shapes=[pltpu.VMEM((2, GLA_HEADS, GLA_HV, GLA_HK), F32)],
        compiler_params=_params("arbitrary"), name="gla_scan",
    )(u, u, u, r_low, u, u, u, r_low, w2pad, b_gk.reshape(2, 1, GLA_DK))


def _gla_out_kernel(x_ref, of_ref, ob_ref, g_ref, gn_ref, mod_ref, w_ref, o_ref):
    o = of_ref[...].astype(F32) + ob_ref[...].astype(F32)
    parts = []
    for h in range(GLA_HEADS):
        oh = o[:, h * GLA_HV:(h + 1) * GLA_HV]
        parts.append(oh * lax.rsqrt(jnp.mean(oh * oh, axis=-1, keepdims=True) + EPS) * gn_ref[...])
    ypre = (jnp.concatenate(parts, axis=1) * _silu(g_ref[...].astype(F32))).astype(BF16)
    y = jnp.dot(ypre, w_ref[...], preferred_element_type=F32)
    o_ref[...] = x_ref[...] + mod_ref[0, 2:3, :] * y


def _gla_out(x, o_f, o_b, u, g_norm, mods, w, *, n_x_tiles):
    r, d = x.shape
    tile = lambda c: pl.BlockSpec((TM, d), lambda i: (i, c))
    return pl.pallas_call(
        _gla_out_kernel,
        out_shape=jax.ShapeDtypeStruct((r, d), F32),
        grid=(r // TM,),
        in_specs=[tile(0), tile(0), tile(0), tile(2),
                  pl.BlockSpec((1, GLA_HV), lambda i: (0, 0)),
                  pl.BlockSpec((1, 6, d), lambda i: (i // n_x_tiles, 0, 0)),
                  pl.BlockSpec((d, d), lambda i: (0, 0))],
        out_specs=tile(0),
        compiler_params=_params("parallel"), name="gla_out",
    )(x, o_f, o_b, u, g_norm.reshape(1, GLA_HV), mods, w)


def _swa_kernel(sink_ref, q_ref, kvp_ref, kvo_ref, kvn_ref, kvc_ref, o_ref, kv_buf, *, seq_len):
    i = pl.program_id(0)
    half = TM // 2
    kv_buf[0:half] = kvp_ref[...]
    kv_buf[half:half + TM] = kvo_ref[...]
    kv_buf[half + TM:2 * TM] = kvn_ref[...]
    kv_buf[2 * TM:3 * TM] = kvc_ref[...]
    nk = 3 * TM
    qpos = i * TM + lax.broadcasted_iota(jnp.int32, (TM, nk), 0)
    col = lax.broadcasted_iota(jnp.int32, (TM, nk), 1)
    kpos = i * TM - half + col
    local = ((jnp.abs(kpos - qpos) <= SWA_WINDOW) & (kpos >= 0) & (kpos < seq_len) & (qpos < seq_len))
    valid = local | (col >= 2 * TM)
    kv_w = SWA_KV_HEADS * HEAD_DIM
    group = SWA_HEADS // SWA_KV_HEADS
    for h in range(SWA_HEADS):
        kv = h // group
        qh = q_ref[:, h * HEAD_DIM:(h + 1) * HEAD_DIM]
        kh = kv_buf[:, kv * HEAD_DIM:(kv + 1) * HEAD_DIM]
        vh = kv_buf[:, kv_w + kv * HEAD_DIM:kv_w + (kv + 1) * HEAD_DIM]
        s = lax.dot_general(qh, kh, _NT, preferred_element_type=F32)
        s = jnp.where(valid, s, NEG)
        sink = sink_ref[h] * LOG2E
        m = jnp.maximum(jnp.max(s, axis=-1, keepdims=True), sink)
        p = jnp.exp2(s - m)
        l = jnp.sum(p, axis=-1, keepdims=True) + jnp.exp2(sink - m)
        o = jnp.dot(p.astype(BF16), vh, preferred_element_type=F32) / l
        o_ref[:, h * HEAD_DIM:(h + 1) * HEAD_DIM] = o.astype(o_ref.dtype)


def _swa_attention(u, sinks, *, seq_len):
    rws = u.shape[0]
    nt = rws // TM
    half = TM // 2
    n_half = rws // half
    kv_w = 2 * SWA_KV_HEADS * HEAD_DIM
    qw = SWA_HEADS * HEAD_DIM
    kvc = qw // kv_w
    return pl.pallas_call(
        functools.partial(_swa_kernel, seq_len=seq_len),
        out_shape=jax.ShapeDtypeStruct((rws, qw), BF16),
        grid=(nt,),
        in_specs=[pl.BlockSpec(memory_space=pltpu.SMEM),
                  pl.BlockSpec((TM, qw), lambda i: (i, 0)),
                  pl.BlockSpec((half, kv_w), lambda i: (jnp.maximum(2 * i - 1, 0), kvc)),
                  pl.BlockSpec((TM, kv_w), lambda i: (i, kvc)),
                  pl.BlockSpec((half, kv_w), lambda i: (jnp.minimum(2 * i + 2, n_half - 1), kvc)),
                  pl.BlockSpec((TM, kv_w), lambda i: (nt - 1, kvc))],
        out_specs=pl.BlockSpec((TM, qw), lambda i: (i, 0)),
        scratch_shapes=[pltpu.VMEM((3 * TM, kv_w), BF16)],
        compiler_params=_params("parallel"), name="swa_attention",
    )(sinks, u, u, u, u, u)


def _na_kernel(q_ref, kp_ref, ko_ref, kn_ref, kc_ref, vp_ref, vo_ref, vn_ref, vc_ref, bias_ref, o_ref,
               k_buf, v_buf, *, n_rows):
    i = pl.program_id(0)
    for j, (kr, vr) in enumerate(((kp_ref, vp_ref), (ko_ref, vo_ref), (kn_ref, vn_ref), (kc_ref, vc_ref))):
        k_buf[j * TM:(j + 1) * TM] = kr[...]
        v_buf[j * TM:(j + 1) * TM] = vr[...]
    n_loc = 3 * TM
    rpt = TM // GRID_W
    qi = lax.broadcasted_iota(jnp.int32, (TM, n_loc), 0)
    ki = lax.broadcasted_iota(jnp.int32, (TM, n_loc), 1)
    r = i * rpt + qi // GRID_W
    c = qi % GRID_W
    krow = (i - 1) * rpt + ki // GRID_W
    kcol = ki % GRID_W
    r0 = jnp.clip(r - NA_KH // 2, 0, n_rows - NA_KH)
    c0 = jnp.clip(c - NA_KW // 2, 0, GRID_W - NA_KW)
    valid = ((krow >= r0) & (krow < r0 + NA_KH) & (kcol >= c0) & (kcol < c0 + NA_KW) & (r < n_rows))
    for h in range(NA_HEADS):
        cols = slice(h * HEAD_DIM, (h + 1) * HEAD_DIM)
        qh = q_ref[:, cols]
        s = lax.dot_general(qh, k_buf[:, cols], _NT, preferred_element_type=F32)
        s_loc = jnp.where(valid, s[:, :n_loc] + bias_ref[h], NEG)
        s_ctx = s[:, n_loc:]
        m = jnp.maximum(jnp.max(s_loc, axis=-1, keepdims=True), jnp.max(s_ctx, axis=-1, keepdims=True))
        p_loc = jnp.exp2(s_loc - m)
        p_ctx = jnp.exp2(s_ctx - m)
        l = jnp.sum(p_loc, axis=-1, keepdims=True) + jnp.sum(p_ctx, axis=-1, keepdims=True)
        o = (jnp.dot(p_loc.astype(BF16), v_buf[0:n_loc, cols], preferred_element_type=F32)
             + jnp.dot(p_ctx.astype(BF16), v_buf[n_loc:, cols], preferred_element_type=F32)) / l
        o_ref[:, cols] = o.astype(o_ref.dtype)


def _na_bias_kernel(t_ref, o_ref, *, lo):
    rpt = TM // GRID_W
    for a in range(rpt):
        for b in range(3 * rpt):
            o_ref[0, a * GRID_W:(a + 1) * GRID_W, b * GRID_W:(b + 1) * GRID_W] = t_ref[0, lo - a + b]


def _na_bias_table(rpb):
    n_heads = rpb.shape[0]
    rpt = TM // GRID_W
    lo = NA_KH - 1 - rpt
    assert lo - (rpt - 1) >= 0 and lo + 3 * rpt <= 2 * NA_KH - 1
    c = jnp.arange(GRID_W)
    col_off = c[None, :] - c[:, None] + NA_KW - 1
    onehot = (col_off[None] == jnp.arange(2 * NA_KW - 1)[:, None, None]).astype(F32)
    tcol = jnp.einsum('hrj,jck->hrck', rpb.astype(F32) * LOG2E, onehot, precision=HIGHEST)
    return pl.pallas_call(
        functools.partial(_na_bias_kernel, lo=lo),
        out_shape=jax.ShapeDtypeStruct((n_heads, TM, 3 * TM), F32),
        grid=(n_heads,),
        in_specs=[pl.BlockSpec((1, 2 * NA_KH - 1, GRID_W, GRID_W), lambda h: (h, 0, 0, 0))],
        out_specs=pl.BlockSpec((1, TM, 3 * TM), lambda h: (h, 0, 0)),
        compiler_params=_params("parallel"), name="na_bias_table",
    )(tcol)


def _na_attention(u, bias, *, n_rows):
    rws = u.shape[0]
    nt = rws // TM
    d = NA_HEADS * HEAD_DIM
    blk = lambda f, c: pl.BlockSpec((TM, d), lambda i: (f(i), c))
    prev = lambda i: jnp.maximum(i - 1, 0)
    own = lambda i: i
    nxt = lambda i: jnp.minimum(i + 1, nt - 1)
    ctx = lambda i: nt - 1
    return pl.pallas_call(
        functools.partial(_na_kernel, n_rows=n_rows),
        out_shape=jax.ShapeDtypeStruct((rws, d), BF16),
        grid=(nt,),
        in_specs=[blk(own, 0), blk(prev, 1), blk(own, 1), blk(nxt, 1), blk(ctx, 1),
                  blk(prev, 2), blk(own, 2), blk(nxt, 2), blk(ctx, 2),
                  pl.BlockSpec((NA_HEADS, TM, 3 * TM), lambda i: (0, 0, 0))],
        out_specs=pl.BlockSpec((TM, d), lambda i: (i, 0)),
        scratch_shapes=[pltpu.VMEM((4 * TM, d), BF16), pltpu.VMEM((4 * TM, d), BF16)],
        compiler_params=_params("parallel"), name="na_attention",
    )(u, u, u, u, u, u, u, u, u, bias)


def _router_kernel(x_ref, g_ref, mod_ref, wr_ref, br_ref, e_ref, w_ref, rank_ref, cnt_ref, carry):
    @pl.when(pl.program_id(0) == 0)
    def _():
        carry[...] = jnp.zeros_like(carry)

    h = _norm_mod(x_ref[...], g_ref[...], mod_ref[0, 4:5, :], mod_ref[0, 3:4, :])
    h_hi, h_lo = _split2(h)
    lt = (jnp.dot(h_hi, wr_ref[0], preferred_element_type=F32) + jnp.dot(h_hi, wr_ref[1], preferred_element_type=F32)
          + jnp.dot(h_lo, wr_ref[0], preferred_element_type=F32))
    logits = jnp.transpose(lt)[:N_EXPERTS]
    scores = jax.nn.sigmoid(logits)
    biased = scores + br_ref[...]
    row = lambda a, e: a[e:e + 1, :]
    best_g = jnp.zeros((1, TM), jnp.int32)
    best_s = None
    for g in range(N_GROUPS):
        v = [row(biased, g * EXPERTS_PER_GROUP + j) for j in range(EXPERTS_PER_GROUP)]
        gs = None
        for a in range(EXPERTS_PER_GROUP):
            for b in range(a + 1, EXPERTS_PER_GROUP):
                pair = v[a] + v[b]
                gs = pair if gs is None else jnp.maximum(gs, pair)
        if best_s is None:
            best_s = gs
        else:
            better = gs > best_s
            best_g = jnp.where(better, g, best_g)
            best_s = jnp.where(better, gs, best_s)
    picks = []
    for _ in range(2):
        top_v = jnp.full((1, TM), -jnp.inf, F32)
        top_i = jnp.full((1, TM), -1, jnp.int32)
        for e in range(N_EXPERTS):
            ok = best_g == (e // EXPERTS_PER_GROUP)
            for p in picks:
                ok = jnp.logical_and(ok, p != e)
            cand = jnp.where(ok, row(biased, e), -jnp.inf)
            better = cand > top_v
            top_i = jnp.where(better, e, top_i)
            top_v = jnp.where(better, cand, top_v)
        picks.append(top_i)
    e_iota = lax.broadcasted_iota(jnp.int32, (N_EXPERTS, TM), 0)
    sel0 = e_iota == picks[0]
    sel1 = e_iota == picks[1]
    w0 = jnp.sum(jnp.where(sel0, scores, 0.0), axis=0, keepdims=True)
    w1 = jnp.sum(jnp.where(sel1, scores, 0.0), axis=0, keepdims=True)
    tot = w0 + w1
    e_ref[0:1, :] = picks[0]
    e_ref[1:2, :] = picks[1]
    w_ref[0:1, :] = w0 / tot
    w_ref[1:2, :] = w1 / tot
    member = jnp.where(jnp.logical_or(sel0, sel1), 1.0, 0.0)
    earlier = (lax.broadcasted_iota(jnp.int32, (TM, TM), 0)
               < lax.broadcasted_iota(jnp.int32, (TM, TM), 1)).astype(BF16)
    before = carry[:, 0:1] + jnp.dot(member.astype(BF16), earlier, preferred_element_type=F32)
    rank_ref[0:1, :] = jnp.sum(jnp.where(sel0, before, 0.0), axis=0, keepdims=True).astype(jnp.int32)
    rank_ref[1:2, :] = jnp.sum(jnp.where(sel1, before, 0.0), axis=0, keepdims=True).astype(jnp.int32)
    carry[...] = carry[...] + jnp.sum(member, axis=1, keepdims=True)
    cnt_ref[...] = carry[...].astype(jnp.int32)


def _router(x, g, mods, w_router_t, b_router, *, n_x_tiles):
    r, d = x.shape
    pair = pl.BlockSpec((2, TM), lambda i: (0, i))
    return pl.pallas_call(
        _router_kernel,
        out_shape=[jax.ShapeDtypeStruct((2, r), jnp.int32), jax.ShapeDtypeStruct((2, r), F32),
                   jax.ShapeDtypeStruct((2, r), jnp.int32), jax.ShapeDtypeStruct((N_EXPERTS, LANES), jnp.int32)],
        grid=(r // TM,),
        in_specs=[pl.BlockSpec((TM, d), lambda i: (i, 0)),
                  pl.BlockSpec((1, d), lambda i: (0, 0)),
                  pl.BlockSpec((1, 6, d), lambda i: (i // n_x_tiles, 0, 0)),
                  pl.BlockSpec((2, d, LANES), lambda i: (0, 0, 0)),
                  pl.BlockSpec((N_EXPERTS, 1), lambda i: (0, 0))],
        out_specs=[pair, pair, pair, pl.BlockSpec((N_EXPERTS, LANES), lambda i: (0, 0))],
        scratch_shapes=[pltpu.VMEM((N_EXPERTS, LANES), F32)],
        compiler_params=_params("arbitrary"), name="router",
    )(x, g.reshape(1, d), mods, w_router_t, b_router.reshape(N_EXPERTS, 1))


def _dispatch_plan(e01, rank01, counts, n_blocks):
    cnt = counts[:, 0]
    nblk = (cnt + MOE_BLOCK - 1) // MOE_BLOCK
    blk_end = jnp.cumsum(nblk)
    off = (blk_end - nblk) * MOE_BLOCK
    eidx = jnp.arange(N_EXPERTS, dtype=jnp.int32)[:, None, None]
    pos = rank01 + jnp.sum(jnp.where(e01[None] == eidx, off[:, None, None], 0), axis=0)
    blk_e = jnp.sum(jnp.arange(n_blocks, dtype=jnp.int32)[:, None] >= blk_end[None, :], axis=1)
    blk_e = jnp.minimum(blk_e, N_EXPERTS - 1).astype(jnp.int32)
    n_used = blk_end[-1:]
    fill = jnp.concatenate([off + cnt, nblk * MOE_BLOCK - cnt, n_used]).astype(jnp.int32)
    slab_row = (pos * CHUNKS).reshape(-1).astype(jnp.int32)
    return slab_row, blk_e, n_used.astype(jnp.int32), fill


def _zero_fill(fill_ref, zbuf, xs_ref, zsem, n_blocks, wait):
    def copy(start_row, n_rows):
        cp = pltpu.make_async_copy(zbuf.at[pl.ds(0, n_rows * CHUNKS)],
                                   xs_ref.at[pl.ds(pl.multiple_of(start_row * CHUNKS, CHUNKS), n_rows * CHUNKS)],
                                   zsem)
        cp.wait() if wait else cp.start()

    for e in range(N_EXPERTS):
        start, length = fill_ref[e], fill_ref[N_EXPERTS + e]
        piece = MOE_BLOCK // 2
        while piece >= 1:
            @pl.when((length & piece) != 0)
            def _():
                copy(start + (length & (-2 * piece)), piece)
            piece //= 2

    def body(j, carry):
        for part in range(MOE_BLOCK // TM):
            copy(j * MOE_BLOCK + part * TM, TM)
        return carry

    lax.fori_loop(fill_ref[2 * N_EXPERTS], n_blocks, body, 0)


def _to_slabs(ref, base, x):
    n = x.shape[0]
    for c in range(CHUNKS):
        ref[pl.ds(base + c, n, stride=CHUNKS), :] = x[:, c * LANES:(c + 1) * LANES]


def _from_slabs(ref, base, n):
    return jnp.concatenate([ref[pl.ds(base + c, n, stride=CHUNKS), :] for c in range(CHUNKS)], axis=1)


def _dispatch_kernel(pos_ref, fill_ref, x_ref, g_ref, mod_ref, xs_ref, h_scr, zbuf, sem, zsem, *,
                     rows, nt, n_blocks):
    i = pl.program_id(0)
    slot = i % 2
    slab_rows = TM * CHUNKS

    @pl.when(i == 0)
    def _():
        zbuf[...] = jnp.zeros_like(zbuf)
        _zero_fill(fill_ref, zbuf, xs_ref, zsem, n_blocks, wait=False)

    def wait_slot(s):
        for _ in range(2):
            pltpu.make_async_copy(h_scr.at[pl.ds(pl.multiple_of(s * slab_rows, slab_rows), slab_rows)],
                                  xs_ref.at[pl.ds(0, slab_rows)], sem.at[s]).wait()

    @pl.when(i >= 2)
    def _():
        wait_slot(slot)

    h = _norm_mod(x_ref[...], g_ref[...], mod_ref[0, 4:5, :], mod_ref[0, 3:4, :])
    _to_slabs(h_scr, slot * slab_rows, h)
    base = i * TM

    def body(grp, carry):
        for u in range(SUBLANES):
            t = grp * SUBLANES + u
            src = h_scr.at[pl.ds(pl.multiple_of(slot * slab_rows + t * CHUNKS, CHUNKS), CHUNKS)]
            for k in range(2):
                p = pl.multiple_of(pos_ref[k * rows + base + t], CHUNKS)
                pltpu.make_async_copy(src, xs_ref.at[pl.ds(p, CHUNKS)], sem.at[slot]).start(priority=k)
        return carry

    lax.fori_loop(0, TM // SUBLANES, body, 0)

    @pl.when(i == nt - 1)
    def _():
        wait_slot(slot)
        if nt > 1:
            wait_slot(1 - slot)
        _zero_fill(fill_ref, zbuf, xs_ref, zsem, n_blocks, wait=True)


def _dispatch(pos, fill, x, g, mods, n_blocks, *, n_x_tiles):
    r, d = x.shape
    nt = r // TM
    assert MOE_BLOCK % TM == 0 and MOE_BLOCK // 2 <= TM and d == CHUNKS * LANES
    n_rows = n_blocks * MOE_BLOCK
    return pl.pallas_call(
        functools.partial(_dispatch_kernel, rows=r, nt=nt, n_blocks=n_blocks),
        out_shape=jax.ShapeDtypeStruct((n_rows * CHUNKS, LANES), F32),
        grid_spec=pltpu.PrefetchScalarGridSpec(
            num_scalar_prefetch=2, grid=(nt,),
            in_specs=[pl.BlockSpec((TM, d), lambda i, p, f: (i, 0)),
                      pl.BlockSpec((1, d), lambda i, p, f: (0, 0)),
                      pl.BlockSpec((1, 6, d), lambda i, p, f: (i // n_x_tiles, 0, 0))],
            out_specs=pl.BlockSpec(memory_space=pl.ANY),
            scratch_shapes=[pltpu.VMEM((2 * TM * CHUNKS, LANES), F32),
                            pltpu.VMEM((TM * CHUNKS, LANES), F32),
                            pltpu.SemaphoreType.DMA((2,)), pltpu.SemaphoreType.DMA]),
        compiler_params=_params("arbitrary"), name="moe_dispatch",
    )(pos, fill, x, g.reshape(1, d), mods)


def _expert_kernel(be_ref, nu_ref, xs_ref, wg_ref, wu_ref, wd_ref, ys_ref, wg_s, wu_s, wd_s):
    j = pl.program_id(0)

    @pl.when(j < nu_ref[0])
    def _():
        @pl.when(jnp.logical_or(j == 0, be_ref[j] != be_ref[jnp.maximum(j - 1, 0)]))
        def _():
            wg_s[...] = wg_ref[0, 0].astype(BF16)
            wu_s[...] = wu_ref[0, 0].astype(BF16)
            wd_s[...] = wd_ref[0, 0].astype(BF16)

        h = _from_slabs(xs_ref, 0, MOE_BLOCK).astype(BF16)
        a = (_silu(jnp.dot(h, wg_s[...], preferred_element_type=F32))
             * jnp.dot(h, wu_s[...], preferred_element_type=F32))
        _to_slabs(ys_ref, 0, jnp.dot(a.astype(BF16), wd_s[...], preferred_element_type=F32))

    @pl.when(j >= nu_ref[0])
    def _():
        ys_ref[...] = jnp.zeros_like(ys_ref)


def _experts(blk_e, n_used, xs, wg, wu, wd, layer):
    d, de = wg.shape[-2:]
    blk = MOE_BLOCK * CHUNKS
    last = lambda j, nu: jnp.minimum(j, jnp.maximum(nu[0] - 1, 0))
    return pl.pallas_call(
        _expert_kernel,
        out_shape=jax.ShapeDtypeStruct(xs.shape, F32),
        grid_spec=pltpu.PrefetchScalarGridSpec(
            num_scalar_prefetch=2, grid=(xs.shape[0] // blk,),
            in_specs=[pl.BlockSpec((blk, LANES), lambda j, be, nu: (last(j, nu), 0)),
                      pl.BlockSpec((1, 1, d, de), lambda j, be, nu: (layer, be[last(j, nu)], 0, 0)),
                      pl.BlockSpec((1, 1, d, de), lambda j, be, nu: (layer, be[last(j, nu)], 0, 0)),
                      pl.BlockSpec((1, 1, de, d), lambda j, be, nu: (layer, be[last(j, nu)], 0, 0))],
            out_specs=pl.BlockSpec((blk, LANES), lambda j, be, nu: (j, 0)),
            scratch_shapes=[pltpu.VMEM((d, de), BF16), pltpu.VMEM((d, de), BF16), pltpu.VMEM((de, d), BF16)]),
        compiler_params=_params("arbitrary"), name="moe_experts",
    )(blk_e, n_used, xs, wg, wu, wd)


def _combine_kernel(pos_ref, x_ref, mod_ref, w_ref, ys_ref, *rest, rows, nt, final):
    if final:
        gf_ref, o_ref, buf, sem = rest
    else:
        o_ref, buf, sem = rest
    i = pl.program_id(0)
    slot = i % 2
    slab_rows = TM * CHUNKS
    where = lambda s, k: (s * 2 + k) * slab_rows

    def issue(row_tile, s):
        base = row_tile * TM

        def body(grp, carry):
            for u in range(SUBLANES):
                t = grp * SUBLANES + u
                for k in range(2):
                    p = pl.multiple_of(pos_ref[k * rows + base + t], CHUNKS)
                    dst = buf.at[pl.ds(pl.multiple_of(where(s, k) + t * CHUNKS, CHUNKS), CHUNKS)]
                    pltpu.make_async_copy(ys_ref.at[pl.ds(p, CHUNKS)], dst, sem.at[s]).start(priority=k)
            return carry

        lax.fori_loop(0, TM // SUBLANES, body, 0)

    @pl.when(i == 0)
    def _():
        issue(0, 0)

    @pl.when(i + 1 < nt)
    def _():
        issue(i + 1, 1 - slot)

    for k in range(2):
        pltpu.make_async_copy(ys_ref.at[pl.ds(0, slab_rows)],
                              buf.at[pl.ds(pl.multiple_of(where(slot, k), slab_rows), slab_rows)],
                              sem.at[slot]).wait()
    w = w_ref[...]
    y = w[:, 0:1] * _from_slabs(buf, where(slot, 0), TM) + w[:, 1:2] * _from_slabs(buf, where(slot, 1), TM)
    x = x_ref[...] + mod_ref[0, 5:6, :] * y
    if final:
        x = x * lax.rsqrt(jnp.mean(x * x, axis=-1, keepdims=True) + EPS) * gf_ref[...]
    o_ref[...] = x


def _combine(pos, x, mods, w_cols, ys, *, n_x_tiles, g_final=None):
    r, d = x.shape
    final = g_final is not None
    nt = n_x_tiles if final else r // TM
    in_specs = [pl.BlockSpec((TM, d), lambda i, p: (i, 0)),
                pl.BlockSpec((1, 6, d), lambda i, p: (i // n_x_tiles, 0, 0)),
                pl.BlockSpec((TM, 2), lambda i, p: (i, 0)),
                pl.BlockSpec(memory_space=pl.ANY)]
    args = [pos, x, mods, w_cols, ys]
    if final:
        in_specs.append(pl.BlockSpec((1, d), lambda i, p: (0, 0)))
        args.append(g_final.reshape(1, d))
    return pl.pallas_call(
        functools.partial(_combine_kernel, rows=r, nt=nt, final=final),
        out_shape=jax.ShapeDtypeStruct((nt * TM, d), F32),
        grid_spec=pltpu.PrefetchScalarGridSpec(
            num_scalar_prefetch=1, grid=(nt,), in_specs=in_specs,
            out_specs=pl.BlockSpec((TM, d), lambda i, p: (i, 0)),
            scratch_shapes=[pltpu.VMEM((2 * 2 * TM * CHUNKS, LANES), F32), pltpu.SemaphoreType.DMA((2,))]),
        compiler_params=_params("arbitrary"), name="moe_combine",
    )(*args)


def _moe(x, g, mods, w_router_t, b_router, wg, wu, wd, layer, *, n_x_tiles, g_final=None):
    r = x.shape[0]
    n_blocks = -(-2 * r // MOE_BLOCK) + N_EXPERTS
    e01, w01, rank01, counts = _router(x, g, mods, w_router_t, b_router, n_x_tiles=n_x_tiles)
    pos, blk_e, n_used, fill = _dispatch_plan(e01, rank01, counts, n_blocks)
    xs = _dispatch(pos, fill, x, g, mods, n_blocks, n_x_tiles=n_x_tiles)
    ys = _experts(blk_e, n_used, xs, wg, wu, wd, layer)
    return _combine(pos, x, mods, w01.T, ys, n_x_tiles=n_x_tiles, g_final=g_final)


def _rope_tables(seq_len, n_rows_total):
    half = HEAD_DIM // 2
    inv_freq = ROPE_BASE ** (-jnp.arange(0, half, 2, dtype=F32) / half)
    first = jnp.arange(half) < (half // 2)

    def tab(n):
        ang = jnp.arange(n, dtype=F32)[:, None] * inv_freq
        ang = jnp.concatenate([ang, ang], axis=-1)
        c, s = jnp.cos(ang), jnp.sin(ang)
        return jnp.stack([c, jnp.where(first, -s, 0.0), jnp.where(first, 0.0, s)], axis=1)

    ident = jnp.stack([jnp.ones((half,), F32), jnp.zeros((half,), F32), jnp.zeros((half,), F32)])
    reps = LANES // HEAD_DIM
    n_ctx_rows = (n_rows_total - seq_len) // GRID_W
    row = jnp.concatenate([tab(seq_len // GRID_W), jnp.broadcast_to(ident, (n_ctx_rows, 3, half))], axis=0)
    row = jnp.tile(jnp.concatenate([row, jnp.zeros_like(row)], axis=-1), (1, 1, reps))
    col = jnp.stack([tab(GRID_W), jnp.broadcast_to(ident, (GRID_W, 3, half))])
    col = jnp.tile(jnp.concatenate([jnp.zeros_like(col), col], axis=-1), (1, TM // GRID_W, 1, reps))
    return row, col.transpose(0, 2, 1, 3)


def kernel(x, c, ctx, c_ctx, w_mod, b_mod, g_mix, g_ffn, g_final, conv_w_in, conv_w, conv_w_out,
           gla_w_proj, gla_w_gk1, gla_w_gk2, gla_b_gk, gla_g_norm, gla_w_out, swa_w_qkv, swa_sinks,
           swa_w_out, na_w_qkv, na_rpb, na_w_out, router_w, router_b, moe_w_gate, moe_w_up, moe_w_down):
    seq_len, d = x.shape[1], x.shape[2]
    ctx_len = ctx.shape[1]
    assert x.shape[0] == 1 and ctx_len == TM and seq_len % TM == 0 and d == D_MODEL
    assert seq_len % GRID_W == 0 and seq_len // GRID_W >= NA_KH
    n_x_tiles = seq_len // TM
    rows = seq_len + ctx_len
    xs = jnp.concatenate([x[0], ctx[0]], axis=0)
    mods_all = _ada_params(c, c_ctx, w_mod, b_mod)
    w_router = jnp.pad(router_w.astype(F32), ((0, 0), (0, LANES - N_EXPERTS)))
    w_router_t = jnp.stack(_split2(w_router))
    n_mixers = 4

    for i in range(DEPTH):
        kind, j = i % n_mixers, i // n_mixers
        mods = mods_all[i]
        if kind == 0:
            u = _project(xs, g_mix[i], mods, conv_w_in[j].astype(BF16), n_x_tiles=n_x_tiles)
            xs = _conv_mixer(xs, u, conv_w[j], mods, conv_w_out[j].astype(BF16), n_x_tiles=n_x_tiles)
        elif kind == 1:
            n_extra = LANES
            w_low = jnp.concatenate([gla_w_gk1[j, 0], gla_w_gk1[j, 1],
                                     jnp.zeros((d, n_extra - 2 * GLA_RANK), F32)], axis=1).astype(BF16)
            u, r_low = _project(xs, g_mix[i], mods, gla_w_proj[j].astype(BF16), n_x_tiles=n_x_tiles,
                                w_extra=w_low)
            w2pad = jnp.stack(
                [jnp.zeros((n_extra, GLA_DK), F32).at[k * GLA_RANK:(k + 1) * GLA_RANK].set(gla_w_gk2[j, k])
                 for k in range(2)]).astype(BF16)
            o_f, o_b = _gla_scan(u, r_low, w2pad, gla_b_gk[j])
            xs = _gla_out(xs, o_f, o_b, u, gla_g_norm[j], mods, gla_w_out[j].astype(BF16), n_x_tiles=n_x_tiles)
        elif kind == 2:
            rope = _rope_tables(seq_len, rows)
            u = _project(xs, g_mix[i], mods, swa_w_qkv[j].astype(BF16), n_x_tiles=n_x_tiles,
                         q_cols=SWA_HEADS * HEAD_DIM, q_scale=HEAD_DIM ** -0.5 * LOG2E, rope=rope,
                         rope_cols=(SWA_HEADS + SWA_KV_HEADS) * HEAD_DIM)
            ypre = _swa_attention(u, swa_sinks[j], seq_len=seq_len)
            xs = _out_proj(xs, ypre, mods, swa_w_out[j].astype(BF16), n_x_tiles=n_x_tiles)
        else:
            u = _project(xs, g_mix[i], mods, na_w_qkv[j].astype(BF16), n_x_tiles=n_x_tiles,
                         q_cols=NA_HEADS * HEAD_DIM, q_scale=HEAD_DIM ** -0.5 * LOG2E)
            ypre = _na_attention(u, _na_bias_table(na_rpb[j]), n_rows=seq_len // GRID_W)
            xs = _out_proj(xs, ypre, mods, na_w_out[j].astype(BF16), n_x_tiles=n_x_tiles)
        xs = _moe(xs, g_ffn[i], mods, w_router_t, router_b, moe_w_gate, moe_w_up, moe_w_down, i,
                  n_x_tiles=n_x_tiles, g_final=g_final if i == DEPTH - 1 else None)
    return xs[None]
```

```python
import functools

import jax
import jax.numpy as jnp
from jax import lax
from jax.experimental import pallas as pl
from jax.experimental.pallas import tpu as pltpu

F32 = jnp.float32
BF16 = jnp.bfloat16
HIGHEST = lax.Precision.HIGHEST

D_MODEL = 1024
DEPTH = 4
GRID_W = 64
EPS = 1e-6
CONV_W = 3
GLA_HEADS = 4
GLA_DK = D_MODEL // 2
GLA_DV = D_MODEL
GLA_HK = GLA_DK // GLA_HEADS
GLA_HV = GLA_DV // GLA_HEADS
GLA_RANK = 16
GLA_NORMALIZER = 16.0
GLA_CHUNK = 64
HEAD_DIM = 64
SWA_HEADS = D_MODEL // HEAD_DIM
SWA_KV_HEADS = SWA_HEADS // 4
SWA_WINDOW = 128
ROPE_BASE = 10000.0
NA_HEADS = D_MODEL // HEAD_DIM
NA_KH = 8
NA_KW = 16
N_EXPERTS = 16
N_GROUPS = 4
EXPERTS_PER_GROUP = N_EXPERTS // N_GROUPS
D_EXPERT = D_MODEL // 2

TM = 256
MOE_BLOCK = 512
LANES = 128
SUBLANES = 8
CHUNKS = D_MODEL // LANES
NEG = -1e30
LOG2E = 1.4426950408889634
VMEM_LIMIT = 56 * 1024 * 1024

_NT = (((1,), (1,)), ((), ()))
_TN = (((0,), (0,)), ((), ()))


def _params(*sem):
    return pltpu.CompilerParams(dimension_semantics=sem, vmem_limit_bytes=VMEM_LIMIT)


def _norm_mod(x, g, scale, shift):
    ms = jnp.mean(x * x, axis=-1, keepdims=True)
    return (x * lax.rsqrt(ms + EPS) * g) * (1.0 + scale) + shift


def _silu(x):
    return x * jax.nn.sigmoid(x)


def _split2(x):
    hi = x.astype(BF16)
    return hi, (x - hi.astype(F32)).astype(BF16)


def _ada_kernel(cond_ref, w_ref, b_ref, o_ref):
    s = _silu(cond_ref[...])
    o_ref[0] = jnp.dot(s, w_ref[0], precision=HIGHEST, preferred_element_type=F32) + b_ref[0]


def _ada_params(c, c_ctx, w_mod, b_mod):
    depth, d, n = w_mod.shape
    nc = n // 4
    cond = jnp.zeros((8, d), F32).at[0].set(c[0]).at[1].set(c_ctx)
    out = pl.pallas_call(
        _ada_kernel,
        out_shape=jax.ShapeDtypeStruct((depth, 8, n), F32),
        grid=(depth, n // nc),
        in_specs=[pl.BlockSpec((8, d), lambda l, j: (0, 0)),
                  pl.BlockSpec((1, d, nc), lambda l, j: (l, 0, j)),
                  pl.BlockSpec((1, 1, nc), lambda l, j: (l, 0, j))],
        out_specs=pl.BlockSpec((1, 8, nc), lambda l, j: (l, 0, j)),
        compiler_params=_params("parallel", "parallel"),
        name="ada_params",
    )(cond, w_mod, b_mod.reshape(depth, 1, n))
    return out[:, :2].reshape(depth, 2, 6, d)


def _row_tile(rows, cap):
    return next(t for t in (1280, 640, TM) if t <= cap and rows % t == 0)


def _mod_rows(mod_ref, tb, seq_len):
    row = pl.program_id(0) * tb + lax.broadcasted_iota(jnp.int32, (tb, 1), 0)
    is_ctx = row >= seq_len
    return lambda j: jnp.where(is_ctx, mod_ref[1, j:j + 1, :], mod_ref[0, j:j + 1, :])


def _proj_kernel(x_ref, g_ref, mod_ref, w_ref, *rest, tb, seq_len, n_chunk, q_cols, q_scale, rope_cols,
                 has_extra):
    n_main = w_ref.shape[1]
    if has_extra:
        wx_ref, rest = rest[0], rest[1:]
    if rope_cols:
        row_ref, col_ref = rest[:2]
        rest = rest[2:]
        rpt = TM // GRID_W
        cos_t, sa_t, sb_t = [
            jnp.concatenate([jnp.broadcast_to(row_ref[a, k:k + 1, :], (GRID_W, LANES)) for a in range(rpt)],
                            axis=0) + col_ref[0, k]
            for k in range(3)]
    o_ref = rest[0]
    pick = _mod_rows(mod_ref, tb, seq_len)
    h = _norm_mod(x_ref[...], g_ref[...], pick(1), pick(0)).astype(BF16)
    for c0 in range(0, n_main, n_chunk):
        c1 = min(c0 + n_chunk, n_main)
        y = jnp.dot(h, w_ref[:, c0:c1], preferred_element_type=F32)
        if c0 < q_cols:
            y = y * q_scale
        if c0 >= rope_cols:
            o_ref[:, c0:c1] = y.astype(o_ref.dtype)
            continue
        for s0 in range(c0, c1, LANES):
            ys = y[:, s0 - c0:s0 - c0 + LANES]
            if s0 < rope_cols:
                ys = ys * cos_t + pltpu.roll(ys, LANES - 16, 1) * sa_t + pltpu.roll(ys, 16, 1) * sb_t
            o_ref[:, s0:s0 + LANES] = ys.astype(o_ref.dtype)
    if has_extra:
        rest[1][...] = jnp.dot(h, wx_ref[...], preferred_element_type=F32)


def _project(x, g, mods, w, *, seq_len, n_chunk=512, q_cols=0, q_scale=1.0, rope=None, rope_cols=0,
             w_extra=None):
    r, d = x.shape
    n_main = w.shape[1]
    n_extra = 0 if w_extra is None else w_extra.shape[1]
    tb = TM if rope_cols else _row_tile(r, 1280)
    in_specs = [pl.BlockSpec((tb, d), lambda i: (i, 0)),
                pl.BlockSpec((1, d), lambda i: (0, 0)),
                pl.BlockSpec((2, 6, d), lambda i: (0, 0, 0)),
                pl.BlockSpec((d, n_main), lambda i: (0, 0))]
    args = [x, g.reshape(1, d), mods, w]
    if n_extra:
        in_specs.append(pl.BlockSpec((d, n_extra), lambda i: (0, 0)))
        args.append(w_extra)
    if rope_cols:
        rpt = TM // GRID_W
        in_specs += [pl.BlockSpec((rpt, 3, LANES), lambda i: (i, 0, 0)),
                     pl.BlockSpec((1, 3, TM, LANES), lambda i: (i // (seq_len // TM), 0, 0, 0))]
        args += list(rope)
    out_shape = [jax.ShapeDtypeStruct((r, n_main), BF16)]
    out_specs = [pl.BlockSpec((tb, n_main), lambda i: (i, 0))]
    if n_extra:
        out_shape.append(jax.ShapeDtypeStruct((r, n_extra), F32))
        out_specs.append(pl.BlockSpec((tb, n_extra), lambda i: (i, 0)))
    out = pl.pallas_call(
        functools.partial(_proj_kernel, tb=tb, seq_len=seq_len, n_chunk=n_chunk, q_cols=q_cols,
                          q_scale=q_scale, rope_cols=rope_cols, has_extra=bool(n_extra)),
        out_shape=out_shape, grid=(r // tb,), in_specs=in_specs, out_specs=out_specs,
        compiler_params=_params("parallel"), name="norm_proj",
    )(*args)
    return out if n_extra else out[0]


def _out_proj_kernel(x_ref, y_ref, mod_ref, w_ref, o_ref, *, tb, seq_len):
    y = jnp.dot(y_ref[...], w_ref[...], preferred_element_type=F32)
    o_ref[...] = x_ref[...] + _mod_rows(mod_ref, tb, seq_len)(2) * y


def _out_proj(x, ypre, mods, w, *, seq_len):
    r, d = x.shape
    tb = _row_tile(r, 1280)
    return pl.pallas_call(
        functools.partial(_out_proj_kernel, tb=tb, seq_len=seq_len),
        out_shape=jax.ShapeDtypeStruct((r, d), F32),
        grid=(r // tb,),
        in_specs=[pl.BlockSpec((tb, d), lambda i: (i, 0)),
                  pl.BlockSpec((tb, d), lambda i: (i, 0)),
                  pl.BlockSpec((2, 6, d), lambda i: (0, 0, 0)),
                  pl.BlockSpec((d, d), lambda i: (0, 0))],
        out_specs=pl.BlockSpec((tb, d), lambda i: (i, 0)),
        compiler_params=_params("parallel"), name="out_proj",
    )(x, ypre, mods, w)


def _conv_kernel(x_ref, bg_ref, cg_ref, v_ref, cgp_ref, vp_ref, cgn_ref, vn_ref, cw_ref, mod_ref, w_ref,
                 o_ref, *, tb, seq_len, n_rows):
    z = cg_ref[...].astype(F32) * v_ref[...].astype(F32)
    z_prev = cgp_ref[SUBLANES - 1:SUBLANES, :].astype(F32) * vp_ref[SUBLANES - 1:SUBLANES, :].astype(F32)
    z_next = cgn_ref[0:1, :].astype(F32) * vn_ref[0:1, :].astype(F32)
    row = lax.broadcasted_iota(jnp.int32, z.shape, 0)
    pos = pl.program_id(0) * tb + row
    z_dn = jnp.where(row == 0, z_prev, pltpu.roll(z, 1, 0))
    z_up = jnp.where(row == tb - 1, z_next, pltpu.roll(z, tb - 1, 0))
    z_dn = jnp.where(jnp.logical_or(pos == 0, pos == seq_len), 0.0, z_dn)
    z_up = jnp.where(jnp.logical_or(pos == seq_len - 1, pos == n_rows - 1), 0.0, z_up)
    conv = z_dn * cw_ref[0:1, :] + z * cw_ref[1:2, :] + z_up * cw_ref[2:3, :]
    ypre = (bg_ref[...].astype(F32) * conv).astype(BF16)
    y = jnp.dot(ypre, w_ref[...], preferred_element_type=F32)
    o_ref[...] = x_ref[...] + _mod_rows(mod_ref, tb, seq_len)(2) * y


def _conv_mixer(x, u, conv_w, mods, w_out, *, seq_len):
    r, d = x.shape
    tb = _row_tile(r, 640)
    n8 = r // SUBLANES
    t8 = tb // SUBLANES
    tile = lambda c: pl.BlockSpec((tb, d), lambda i: (i, c))
    prev = lambda c: pl.BlockSpec((SUBLANES, d), lambda i: (jnp.maximum(i * t8 - 1, 0), c))
    nxt = lambda c: pl.BlockSpec((SUBLANES, d), lambda i: (jnp.minimum((i + 1) * t8, n8 - 1), c))
    return pl.pallas_call(
        functools.partial(_conv_kernel, tb=tb, seq_len=seq_len, n_rows=r),
        out_shape=jax.ShapeDtypeStruct((r, d), F32),
        grid=(r // tb,),
        in_specs=[pl.BlockSpec((tb, d), lambda i: (i, 0)), tile(0), tile(1), tile(2),
                  prev(1), prev(2), nxt(1), nxt(2),
                  pl.BlockSpec((CONV_W, d), lambda i: (0, 0)),
                  pl.BlockSpec((2, 6, d), lambda i: (0, 0, 0)),
                  pl.BlockSpec((d, d), lambda i: (0, 0))],
        out_specs=pl.BlockSpec((tb, d), lambda i: (i, 0)),
        compiler_params=_params("parallel"), name="conv_mixer",
    )(x, u, u, u, u, u, u, u, conv_w, mods, w_out)


def _log_sigmoid(z):
    return jnp.minimum(z, 0.0) - jnp.log1p(jnp.exp(-jnp.abs(z)))


def _gla_chunk(q_ref, k_ref, v_ref, gk, o_ref, st_ref, d, c, reverse):
    C = GLA_CHUNK
    rows = slice(c * C, (c + 1) * C)
    ii = lax.broadcasted_iota(jnp.int32, (C, C), 0)
    jj = lax.broadcasted_iota(jnp.int32, (C, C), 1)
    causal = (jj >= ii) if reverse else (jj <= ii)
    ones = jnp.where(causal, 1.0, 0.0).astype(BF16)
    g_hi, g_lo = _split2(gk[rows])
    bcum = (jnp.dot(ones, g_hi, preferred_element_type=F32)
            + jnp.dot(ones, g_lo, preferred_element_type=F32))
    total = bcum[0:1, :] if reverse else bcum[C - 1:C, :]
    q = q_ref[rows, :].astype(F32) * (GLA_HK ** -0.5)
    k = k_ref[rows, :].astype(F32)
    q_dec = (q * jnp.exp(bcum)).astype(BF16)
    k_inv = (k * jnp.exp(-bcum)).astype(BF16)
    k_end = (k * jnp.exp(total - bcum)).astype(BF16)
    decay = jnp.exp(total)
    for h in range(GLA_HEADS):
        kcols = slice(h * GLA_HK, (h + 1) * GLA_HK)
        vcols = slice(h * GLA_HV, (h + 1) * GLA_HV)
        vh = v_ref[rows, vcols]
        att = lax.dot_general(q_dec[:, kcols], k_inv[:, kcols], _NT, preferred_element_type=F32)
        att = jnp.where(causal, att, 0.0).astype(BF16)
        st = st_ref[d, h]
        o = (jnp.dot(att, vh, preferred_element_type=F32)
             + lax.dot_general(q_dec[:, kcols], st.astype(BF16), _NT, preferred_element_type=F32))
        st_ref[d, h] = st * decay[:, kcols] + lax.dot_general(vh, k_end[:, kcols], _TN,
                                                              preferred_element_type=F32)
        o_ref[rows, vcols] = o.astype(o_ref.dtype)


def _gla_kernel(qf_ref, kf_ref, vf_ref, rf_ref, qb_ref, kb_ref, vb_ref, rb_ref, w2_ref, b_ref,
                of_ref, ob_ref, st_ref):
    @pl.when(pl.program_id(0) == 0)
    def _():
        st_ref[...] = jnp.zeros_like(st_ref)

    def gates(r_ref, d):
        z = jnp.dot(r_ref[...].astype(BF16), w2_ref[d], preferred_element_type=F32) + b_ref[d]
        return _log_sigmoid(z) * (1.0 / GLA_NORMALIZER)

    gk_f, gk_b = gates(rf_ref, 0), gates(rb_ref, 1)
    n_chunks = TM // GLA_CHUNK
    for c in range(n_chunks):
        _gla_chunk(qf_ref, kf_ref, vf_ref, gk_f, of_ref, st_ref, 0, c, False)
        _gla_chunk(qb_ref, kb_ref, vb_ref, gk_b, ob_ref, st_ref, 1, n_chunks - 1 - c, True)


def _gla_scan(u, r_low, w2pad, b_gk):
    rws = u.shape[0]
    nt = rws // TM
    fwd = lambda s: (s + nt - 1) % nt
    bwd = lambda s: nt - 1 - s
    qkvr = lambda order: [pl.BlockSpec((TM, GLA_DK), lambda s: (order(s), 0)),
                          pl.BlockSpec((TM, GLA_DK), lambda s: (order(s), 1)),
                          pl.BlockSpec((TM, GLA_DV), lambda s: (order(s), 1)),
                          pl.BlockSpec((TM, LANES), lambda s: (order(s), 0))]
    out = jax.ShapeDtypeStruct((rws, GLA_DV), BF16)
    return pl.pallas_call(
        _gla_kernel,
        out_shape=[out, out],
        grid=(nt,),
        in_specs=qkvr(fwd) + qkvr(bwd) + [pl.BlockSpec((2, LANES, GLA_DK), lambda s: (0, 0, 0)),
                                          pl.BlockSpec((2, 1, GLA_DK), lambda s: (0, 0, 0))],
        out_specs=[pl.BlockSpec((TM, GLA_DV), lambda s: (fwd(s), 0)),
                   pl.BlockSpec((TM, GLA_DV), lambda s: (bwd(s), 0))],
        scratch_shapes=[pltpu.VMEM((2, GLA_HEADS, GLA_HV, GLA_HK), F32)],
        compiler_params=_params("arbitrary"), name="gla_scan",
    )(u, u, u, r_low, u, u, u, r_low, w2pad, b_gk.reshape(2, 1, GLA_DK))


def _gla_out_kernel(x_ref, of_ref, ob_ref, g_ref, gn_ref, mod_ref, w_ref, o_ref, *, tb, seq_len):
    o = of_ref[...].astype(F32) + ob_ref[...].astype(F32)
    parts = []
    for h in range(GLA_HEADS):
        oh = o[:, h * GLA_HV:(h + 1) * GLA_HV]
        parts.append(oh * lax.rsqrt(jnp.mean(oh * oh, axis=-1, keepdims=True) + EPS) * gn_ref[...])
    ypre = (jnp.concatenate(parts, axis=1) * _silu(g_ref[...].astype(F32))).astype(BF16)
    y = jnp.dot(ypre, w_ref[...], preferred_element_type=F32)
    o_ref[...] = x_ref[...] + _mod_rows(mod_ref, tb, seq_len)(2) * y


def _gla_out(x, o_f, o_b, u, g_norm, mods, w, *, seq_len):
    r, d = x.shape
    tb = _row_tile(r, 640)
    tile = lambda c: pl.BlockSpec((tb, d), lambda i: (i, c))
    return pl.pallas_call(
        functools.partial(_gla_out_kernel, tb=tb, seq_len=seq_len),
        out_shape=jax.ShapeDtypeStruct((r, d), F32),
        grid=(r // tb,),
        in_specs=[tile(0), tile(0), tile(0), tile(2),
                  pl.BlockSpec((1, GLA_HV), lambda i: (0, 0)),
                  pl.BlockSpec((2, 6, d), lambda i: (0, 0, 0)),
                  pl.BlockSpec((d, d), lambda i: (0, 0))],
        out_specs=tile(0),
        compiler_params=_params("parallel"), name="gla_out",
    )(x, o_f, o_b, u, g_norm.reshape(1, GLA_HV), mods, w)


def _swa_kernel(sink_ref, q_ref, kvp_ref, kvo_ref, kvn_ref, kvc_ref, o_ref, kv_buf, *, seq_len):
    i = pl.program_id(0)
    half = TM // 2
    kv_buf[0:half] = kvp_ref[...]
    kv_buf[half:half + TM] = kvo_ref[...]
    kv_buf[half + TM:2 * TM] = kvn_ref[...]
    kv_buf[2 * TM:3 * TM] = kvc_ref[...]
    nk = 3 * TM
    qpos = i * TM + lax.broadcasted_iota(jnp.int32, (TM, nk), 0)
    col = lax.broadcasted_iota(jnp.int32, (TM, nk), 1)
    kpos = i * TM - half + col
    local = ((jnp.abs(kpos - qpos) <= SWA_WINDOW) & (kpos >= 0) & (kpos < seq_len) & (qpos < seq_len))
    valid = local | (col >= 2 * TM)
    kv_w = SWA_KV_HEADS * HEAD_DIM
    group = SWA_HEADS // SWA_KV_HEADS
    for h in range(SWA_HEADS):
        kv = h // group
        qh = q_ref[:, h * HEAD_DIM:(h + 1) * HEAD_DIM]
        kh = kv_buf[:, kv * HEAD_DIM:(kv + 1) * HEAD_DIM]
        vh = kv_buf[:, kv_w + kv * HEAD_DIM:kv_w + (kv + 1) * HEAD_DIM]
        s = lax.dot_general(qh, kh, _NT, preferred_element_type=F32)
        s = jnp.where(valid, s, NEG)
        sink = sink_ref[h] * LOG2E
        m = jnp.maximum(jnp.max(s, axis=-1, keepdims=True), sink)
        p = jnp.exp2(s - m)
        l = jnp.sum(p, axis=-1, keepdims=True) + jnp.exp2(sink - m)
        o = jnp.dot(p.astype(BF16), vh, preferred_element_type=F32) / l
        o_ref[:, h * HEAD_DIM:(h + 1) * HEAD_DIM] = o.astype(o_ref.dtype)


def _swa_attention(u, sinks, *, seq_len):
    rws = u.shape[0]
    nt = rws // TM
    half = TM // 2
    n_half = rws // half
    kv_w = 2 * SWA_KV_HEADS * HEAD_DIM
    qw = SWA_HEADS * HEAD_DIM
    kvc = qw // kv_w
    return pl.pallas_call(
        functools.partial(_swa_kernel, seq_len=seq_len),
        out_shape=jax.ShapeDtypeStruct((rws, qw), BF16),
        grid=(nt,),
        in_specs=[pl.BlockSpec(memory_space=pltpu.SMEM),
                  pl.BlockSpec((TM, qw), lambda i: (i, 0)),
                  pl.BlockSpec((half, kv_w), lambda i: (jnp.maximum(2 * i - 1, 0), kvc)),
                  pl.BlockSpec((TM, kv_w), lambda i: (i, kvc)),
                  pl.BlockSpec((half, kv_w), lambda i: (jnp.minimum(2 * i + 2, n_half - 1), kvc)),
                  pl.BlockSpec((TM, kv_w), lambda i: (nt - 1, kvc))],
        out_specs=pl.BlockSpec((TM, qw), lambda i: (i, 0)),
        scratch_shapes=[pltpu.VMEM((3 * TM, kv_w), BF16)],
        compiler_params=_params("parallel"), name="swa_attention",
    )(sinks, u, u, u, u, u)


def _na_kernel(q_ref, kp_ref, ko_ref, kn_ref, kc_ref, vp_ref, vo_ref, vn_ref, vc_ref, bias_ref, o_ref,
               k_buf, v_buf, *, n_rows):
    i = pl.program_id(0)
    for j, (kr, vr) in enumerate(((kp_ref, vp_ref), (ko_ref, vo_ref), (kn_ref, vn_ref), (kc_ref, vc_ref))):
        k_buf[j * TM:(j + 1) * TM] = kr[...]
        v_buf[j * TM:(j + 1) * TM] = vr[...]
    n_loc = 3 * TM
    rpt = TM // GRID_W
    qi = lax.broadcasted_iota(jnp.int32, (TM, n_loc), 0)
    ki = lax.broadcasted_iota(jnp.int32, (TM, n_loc), 1)
    r = i * rpt + qi // GRID_W
    c = qi % GRID_W
    krow = (i - 1) * rpt + ki // GRID_W
    kcol = ki % GRID_W
    r0 = jnp.clip(r - NA_KH // 2, 0, n_rows - NA_KH)
    c0 = jnp.clip(c - NA_KW // 2, 0, GRID_W - NA_KW)
    valid = ((krow >= r0) & (krow < r0 + NA_KH) & (kcol >= c0) & (kcol < c0 + NA_KW) & (r < n_rows))
    for h in range(NA_HEADS):
        cols = slice(h * HEAD_DIM, (h + 1) * HEAD_DIM)
        qh = q_ref[:, cols]
        s = lax.dot_general(qh, k_buf[:, cols], _NT, preferred_element_type=F32)
        s_loc = jnp.where(valid, s[:, :n_loc] + bias_ref[h], NEG)
        s_ctx = s[:, n_loc:]
        m = jnp.maximum(jnp.max(s_loc, axis=-1, keepdims=True), jnp.max(s_ctx, axis=-1, keepdims=True))
        p_loc = jnp.exp2(s_loc - m)
        p_ctx = jnp.exp2(s_ctx - m)
        l = jnp.sum(p_loc, axis=-1, keepdims=True) + jnp.sum(p_ctx, axis=-1, keepdims=True)
        o = (jnp.dot(p_loc.astype(BF16), v_buf[0:n_loc, cols], preferred_element_type=F32)
             + jnp.dot(p_ctx.astype(BF16), v_buf[n_loc:, cols], preferred_element_type=F32)) / l
        o_ref[:, cols] = o.astype(o_ref.dtype)


def _na_bias_kernel(t_ref, o_ref, *, lo):
    rpt = TM // GRID_W
    for a in range(rpt):
        for b in range(3 * rpt):
            o_ref[0, a * GRID_W:(a + 1) * GRID_W, b * GRID_W:(b + 1) * GRID_W] = t_ref[0, lo - a + b]


def _na_bias_table(rpb):
    n_heads = rpb.shape[0]
    rpt = TM // GRID_W
    lo = NA_KH - 1 - rpt
    assert lo - (rpt - 1) >= 0 and lo + 3 * rpt <= 2 * NA_KH - 1
    c = jnp.arange(GRID_W)
    col_off = c[None, :] - c[:, None] + NA_KW - 1
    onehot = (col_off[None] == jnp.arange(2 * NA_KW - 1)[:, None, None]).astype(F32)
    tcol = jnp.einsum('hrj,jck->hrck', rpb.astype(F32) * LOG2E, onehot, precision=HIGHEST)
    return pl.pallas_call(
        functools.partial(_na_bias_kernel, lo=lo),
        out_shape=jax.ShapeDtypeStruct((n_heads, TM, 3 * TM), F32),
        grid=(n_heads,),
        in_specs=[pl.BlockSpec((1, 2 * NA_KH - 1, GRID_W, GRID_W), lambda h: (h, 0, 0, 0))],
        out_specs=pl.BlockSpec((1, TM, 3 * TM), lambda h: (h, 0, 0)),
        compiler_params=_params("parallel"), name="na_bias_table",
    )(tcol)


def _na_attention(u, bias, *, n_rows):
    rws = u.shape[0]
    nt = rws // TM
    d = NA_HEADS * HEAD_DIM
    blk = lambda f, c: pl.BlockSpec((TM, d), lambda i: (f(i), c))
    prev = lambda i: jnp.maximum(i - 1, 0)
    own = lambda i: i
    nxt = lambda i: jnp.minimum(i + 1, nt - 1)
    ctx = lambda i: nt - 1
    return pl.pallas_call(
        functools.partial(_na_kernel, n_rows=n_rows),
        out_shape=jax.ShapeDtypeStruct((rws, d), BF16),
        grid=(nt,),
        in_specs=[blk(own, 0), blk(prev, 1), blk(own, 1), blk(nxt, 1), blk(ctx, 1),
                  blk(prev, 2), blk(own, 2), blk(nxt, 2), blk(ctx, 2),
                  pl.BlockSpec((NA_HEADS, TM, 3 * TM), lambda i: (0, 0, 0))],
        out_specs=pl.BlockSpec((TM, d), lambda i: (i, 0)),
        scratch_shapes=[pltpu.VMEM((4 * TM, d), BF16), pltpu.VMEM((4 * TM, d), BF16)],
        compiler_params=_params("parallel"), name="na_attention",
    )(u, u, u, u, u, u, u, u, u, bias)


def _router_kernel(x_ref, g_ref, mod_ref, wr_ref, br_ref, e_ref, w_ref, rank_ref, cnt_ref, carry, *,
                   tb, seq_len):
    @pl.when(pl.program_id(0) == 0)
    def _():
        carry[...] = jnp.zeros_like(carry)

    pick = _mod_rows(mod_ref, tb, seq_len)
    h = _norm_mod(x_ref[...], g_ref[...], pick(4), pick(3))
    h_hi, h_lo = _split2(h)
    lt = (jnp.dot(h_hi, wr_ref[0], preferred_element_type=F32) + jnp.dot(h_hi, wr_ref[1], preferred_element_type=F32)
          + jnp.dot(h_lo, wr_ref[0], preferred_element_type=F32))
    logits = jnp.transpose(lt)[:N_EXPERTS]
    scores = jax.nn.sigmoid(logits)
    biased = scores + br_ref[...]
    row = lambda a, e: a[e:e + 1, :]
    best_g = jnp.zeros((1, tb), jnp.int32)
    best_s = None
    for g in range(N_GROUPS):
        v = [row(biased, g * EXPERTS_PER_GROUP + j) for j in range(EXPERTS_PER_GROUP)]
        gs = None
        for a in range(EXPERTS_PER_GROUP):
            for b in range(a + 1, EXPERTS_PER_GROUP):
                pair = v[a] + v[b]
                gs = pair if gs is None else jnp.maximum(gs, pair)
        if best_s is None:
            best_s = gs
        else:
            better = gs > best_s
            best_g = jnp.where(better, g, best_g)
            best_s = jnp.where(better, gs, best_s)
    picks = []
    for _ in range(2):
        top_v = jnp.full((1, tb), -jnp.inf, F32)
        top_i = jnp.full((1, tb), -1, jnp.int32)
        for e in range(N_EXPERTS):
            ok = best_g == (e // EXPERTS_PER_GROUP)
            for p in picks:
                ok = jnp.logical_and(ok, p != e)
            cand = jnp.where(ok, row(biased, e), -jnp.inf)
            better = cand > top_v
            top_i = jnp.where(better, e, top_i)
            top_v = jnp.where(better, cand, top_v)
        picks.append(top_i)
    e_iota = lax.broadcasted_iota(jnp.int32, (N_EXPERTS, tb), 0)
    sel0 = e_iota == picks[0]
    sel1 = e_iota == picks[1]
    w0 = jnp.sum(jnp.where(sel0, scores, 0.0), axis=0, keepdims=True)
    w1 = jnp.sum(jnp.where(sel1, scores, 0.0), axis=0, keepdims=True)
    tot = w0 + w1
    e_ref[0:1, :] = picks[0]
    e_ref[1:2, :] = picks[1]
    w_ref[0:1, :] = w0 / tot
    w_ref[1:2, :] = w1 / tot
    member = jnp.where(jnp.logical_or(sel0, sel1), 1.0, 0.0)
    earlier = (lax.broadcasted_iota(jnp.int32, (TM, TM), 0)
               < lax.broadcasted_iota(jnp.int32, (TM, TM), 1)).astype(BF16)
    seen = carry[:, 0:1]
    before = []
    for c in range(tb // TM):
        part = member[:, c * TM:(c + 1) * TM]
        before.append(seen + jnp.dot(part.astype(BF16), earlier, preferred_element_type=F32))
        seen = seen + jnp.sum(part, axis=1, keepdims=True)
    before = jnp.concatenate(before, axis=1)
    rank_ref[0:1, :] = jnp.sum(jnp.where(sel0, before, 0.0), axis=0, keepdims=True).astype(jnp.int32)
    rank_ref[1:2, :] = jnp.sum(jnp.where(sel1, before, 0.0), axis=0, keepdims=True).astype(jnp.int32)
    carry[...] = jnp.broadcast_to(seen, carry.shape)
    cnt_ref[...] = carry[...].astype(jnp.int32)


def _router(x, g, mods, w_router_t, b_router, *, seq_len):
    r, d = x.shape
    tb = _row_tile(r, 1280)
    pair = pl.BlockSpec((2, tb), lambda i: (0, i))
    return pl.pallas_call(
        functools.partial(_router_kernel, tb=tb, seq_len=seq_len),
        out_shape=[jax.ShapeDtypeStruct((2, r), jnp.int32), jax.ShapeDtypeStruct((2, r), F32),
                   jax.ShapeDtypeStruct((2, r), jnp.int32), jax.ShapeDtypeStruct((N_EXPERTS, LANES), jnp.int32)],
        grid=(r // tb,),
        in_specs=[pl.BlockSpec((tb, d), lambda i: (i, 0)),
                  pl.BlockSpec((1, d), lambda i: (0, 0)),
                  pl.BlockSpec((2, 6, d), lambda i: (0, 0, 0)),
                  pl.BlockSpec((2, d, LANES), lambda i: (0, 0, 0)),
                  pl.BlockSpec((N_EXPERTS, 1), lambda i: (0, 0))],
        out_specs=[pair, pair, pair, pl.BlockSpec((N_EXPERTS, LANES), lambda i: (0, 0))],
        scratch_shapes=[pltpu.VMEM((N_EXPERTS, LANES), F32)],
        compiler_params=_params("arbitrary"), name="router",
    )(x, g.reshape(1, d), mods, w_router_t, b_router.reshape(N_EXPERTS, 1))


def _dispatch_plan(e01, rank01, counts, n_blocks):
    cnt = counts[:, 0]
    nblk = (cnt + MOE_BLOCK - 1) // MOE_BLOCK
    blk_end = jnp.cumsum(nblk)
    off = (blk_end - nblk) * MOE_BLOCK
    eidx = jnp.arange(N_EXPERTS, dtype=jnp.int32)[:, None, None]
    pos = rank01 + jnp.sum(jnp.where(e01[None] == eidx, off[:, None, None], 0), axis=0)
    blk_e = jnp.sum(jnp.arange(n_blocks, dtype=jnp.int32)[:, None] >= blk_end[None, :], axis=1)
    blk_e = jnp.minimum(blk_e, N_EXPERTS - 1).astype(jnp.int32)
    n_used = blk_end[-1:]
    fill = jnp.concatenate([off + cnt, nblk * MOE_BLOCK - cnt, n_used]).astype(jnp.int32)
    slab_row = (pos * CHUNKS).reshape(-1).astype(jnp.int32)
    return slab_row, blk_e, n_used.astype(jnp.int32), fill


def _zero_fill(fill_ref, zbuf, xs_ref, zsem, n_blocks, wait):
    def copy(start_row, n_rows):
        cp = pltpu.make_async_copy(zbuf.at[pl.ds(0, n_rows * CHUNKS)],
                                   xs_ref.at[pl.ds(pl.multiple_of(start_row * CHUNKS, CHUNKS), n_rows * CHUNKS)],
                                   zsem)
        cp.wait() if wait else cp.start()

    for e in range(N_EXPERTS):
        start, length = fill_ref[e], fill_ref[N_EXPERTS + e]
        piece = MOE_BLOCK // 2
        while piece >= 1:
            @pl.when((length & piece) != 0)
            def _():
                copy(start + (length & (-2 * piece)), piece)
            piece //= 2

    def body(j, carry):
        for part in range(MOE_BLOCK // TM):
            copy(j * MOE_BLOCK + part * TM, TM)
        return carry

    lax.fori_loop(fill_ref[2 * N_EXPERTS], n_blocks, body, 0)


def _to_slabs(ref, base, x):
    n = x.shape[0]
    for c in range(CHUNKS):
        ref[pl.ds(base + c, n, stride=CHUNKS), :] = x[:, c * LANES:(c + 1) * LANES]


def _from_slabs(ref, base, n):
    return jnp.concatenate([ref[pl.ds(base + c, n, stride=CHUNKS), :] for c in range(CHUNKS)], axis=1)


def _dispatch_kernel(pos_ref, fill_ref, x_ref, g_ref, mod_ref, xs_ref, h_scr, zbuf, sem, zsem, *,
                     rows, nt, n_blocks):
    i = pl.program_id(0)
    slot = i % 2
    slab_rows = TM * CHUNKS

    @pl.when(i == 0)
    def _():
        zbuf[...] = jnp.zeros_like(zbuf)
        _zero_fill(fill_ref, zbuf, xs_ref, zsem, n_blocks, wait=False)

    def wait_slot(s):
        for _ in range(2):
            pltpu.make_async_copy(h_scr.at[pl.ds(pl.multiple_of(s * slab_rows, slab_rows), slab_rows)],
                                  xs_ref.at[pl.ds(0, slab_rows)], sem.at[s]).wait()

    @pl.when(i >= 2)
    def _():
        wait_slot(slot)

    h = _norm_mod(x_ref[...], g_ref[...], mod_ref[0, 4:5, :], mod_ref[0, 3:4, :])
    _to_slabs(h_scr, slot * slab_rows, h)
    base = i * TM

    def body(grp, carry):
        for u in range(SUBLANES):
            t = grp * SUBLANES + u
            src = h_scr.at[pl.ds(pl.multiple_of(slot * slab_rows + t * CHUNKS, CHUNKS), CHUNKS)]
            for k in range(2):
                p = pl.multiple_of(pos_ref[k * rows + base + t], CHUNKS)
                pltpu.make_async_copy(src, xs_ref.at[pl.ds(p, CHUNKS)], sem.at[slot]).start(priority=k)
        return carry

    lax.fori_loop(0, TM // SUBLANES, body, 0)

    @pl.when(i == nt - 1)
    def _():
        wait_slot(slot)
        if nt > 1:
            wait_slot(1 - slot)
        _zero_fill(fill_ref, zbuf, xs_ref, zsem, n_blocks, wait=True)


def _dispatch(pos, fill, x, g, mods, n_blocks, *, n_x_tiles):
    r, d = x.shape
    nt = r // TM
    assert MOE_BLOCK % TM == 0 and MOE_BLOCK // 2 <= TM and d == CHUNKS * LANES
    n_rows = n_blocks * MOE_BLOCK
    return pl.pallas_call(
        functools.partial(_dispatch_kernel, rows=r, nt=nt, n_blocks=n_blocks),
        out_shape=jax.ShapeDtypeStruct((n_rows * CHUNKS, LANES), F32),
        grid_spec=pltpu.PrefetchScalarGridSpec(
            num_scalar_prefetch=2, grid=(nt,),
            in_specs=[pl.BlockSpec((TM, d), lambda i, p, f: (i, 0)),
                      pl.BlockSpec((1, d), lambda i, p, f: (0, 0)),
                      pl.BlockSpec((1, 6, d), lambda i, p, f: (i // n_x_tiles, 0, 0))],
            out_specs=pl.BlockSpec(memory_space=pl.ANY),
            scratch_shapes=[pltpu.VMEM((2 * TM * CHUNKS, LANES), F32),
                            pltpu.VMEM((TM * CHUNKS, LANES), F32),
                            pltpu.SemaphoreType.DMA((2,)), pltpu.SemaphoreType.DMA]),
        compiler_params=_params("arbitrary"), name="moe_dispatch",
    )(pos, fill, x, g.reshape(1, d), mods)


def _expert_kernel(be_ref, nu_ref, xs_ref, wg_ref, wu_ref, wd_ref, ys_ref, wg_s, wu_s, wd_s):
    j = pl.program_id(0)

    @pl.when(j < nu_ref[0])
    def _():
        @pl.when(jnp.logical_or(j == 0, be_ref[j] != be_ref[jnp.maximum(j - 1, 0)]))
        def _():
            wg_s[...] = wg_ref[0, 0].astype(BF16)
            wu_s[...] = wu_ref[0, 0].astype(BF16)
            wd_s[...] = wd_ref[0, 0].astype(BF16)

        h = _from_slabs(xs_ref, 0, MOE_BLOCK).astype(BF16)
        a = (_silu(jnp.dot(h, wg_s[...], preferred_element_type=F32))
             * jnp.dot(h, wu_s[...], preferred_element_type=F32))
        _to_slabs(ys_ref, 0, jnp.dot(a.astype(BF16), wd_s[...], preferred_element_type=F32))

    @pl.when(j >= nu_ref[0])
    def _():
        ys_ref[...] = jnp.zeros_like(ys_ref)


def _experts(blk_e, n_used, xs, wg, wu, wd, layer):
    d, de = wg.shape[-2:]
    blk = MOE_BLOCK * CHUNKS
    last = lambda j, nu: jnp.minimum(j, jnp.maximum(nu[0] - 1, 0))
    return pl.pallas_call(
        _expert_kernel,
        out_shape=jax.ShapeDtypeStruct(xs.shape, F32),
        grid_spec=pltpu.PrefetchScalarGridSpec(
            num_scalar_prefetch=2, grid=(xs.shape[0] // blk,),
            in_specs=[pl.BlockSpec((blk, LANES), lambda j, be, nu: (last(j, nu), 0)),
                      pl.BlockSpec((1, 1, d, de), lambda j, be, nu: (layer, be[last(j, nu)], 0, 0)),
                      pl.BlockSpec((1, 1, d, de), lambda j, be, nu: (layer, be[last(j, nu)], 0, 0)),
                      pl.BlockSpec((1, 1, de, d), lambda j, be, nu: (layer, be[last(j, nu)], 0, 0))],
            out_specs=pl.BlockSpec((blk, LANES), lambda j, be, nu: (j, 0)),
            scratch_shapes=[pltpu.VMEM((d, de), BF16), pltpu.VMEM((d, de), BF16), pltpu.VMEM((de, d), BF16)]),
        compiler_params=_params("arbitrary"), name="moe_experts",
    )(blk_e, n_used, xs, wg, wu, wd)


def _combine_kernel(pos_ref, x_ref, mod_ref, w_ref, ys_ref, *rest, rows, nt, final):
    if final:
        gf_ref, o_ref, buf, sem = rest
    else:
        o_ref, buf, sem = rest
    i = pl.program_id(0)
    slot = i % 2
    slab_rows = TM * CHUNKS
    where = lambda s, k: (s * 2 + k) * slab_rows

    def issue(row_tile, s):
        base = row_tile * TM

        def body(grp, carry):
            for u in range(SUBLANES):
                t = grp * SUBLANES + u
                for k in range(2):
                    p = pl.multiple_of(pos_ref[k * rows + base + t], CHUNKS)
                    dst = buf.at[pl.ds(pl.multiple_of(where(s, k) + t * CHUNKS, CHUNKS), CHUNKS)]
                    pltpu.make_async_copy(ys_ref.at[pl.ds(p, CHUNKS)], dst, sem.at[s]).start(priority=k)
            return carry

        lax.fori_loop(0, TM // SUBLANES, body, 0)

    @pl.when(i == 0)
    def _():
        issue(0, 0)

    @pl.when(i + 1 < nt)
    def _():
        issue(i + 1, 1 - slot)

    for k in range(2):
        pltpu.make_async_copy(ys_ref.at[pl.ds(0, slab_rows)],
                              buf.at[pl.ds(pl.multiple_of(where(slot, k), slab_rows), slab_rows)],
                              sem.at[slot]).wait()
    w = w_ref[...]
    y = w[:, 0:1] * _from_slabs(buf, where(slot, 0), TM) + w[:, 1:2] * _from_slabs(buf, where(slot, 1), TM)
    x = x_ref[...] + mod_ref[0, 5:6, :] * y
    if final:
        x = x * lax.rsqrt(jnp.mean(x * x, axis=-1, keepdims=True) + EPS) * gf_ref[...]
    o_ref[...] = x


def _combine(pos, x, mods, w_cols, ys, *, n_x_tiles, g_final=None):
    r, d = x.shape
    final = g_final is not None
    nt = n_x_tiles if final else r // TM
    in_specs = [pl.BlockSpec((TM, d), lambda i, p: (i, 0)),
                pl.BlockSpec((1, 6, d), lambda i, p: (i // n_x_tiles, 0, 0)),
                pl.BlockSpec((TM, 2), lambda i, p: (i, 0)),
                pl.BlockSpec(memory_space=pl.ANY)]
    args = [pos, x, mods, w_cols, ys]
    if final:
        in_specs.append(pl.BlockSpec((1, d), lambda i, p: (0, 0)))
        args.append(g_final.reshape(1, d))
    return pl.pallas_call(
        functools.partial(_combine_kernel, rows=r, nt=nt, final=final),
        out_shape=jax.ShapeDtypeStruct((nt * TM, d), F32),
        grid_spec=pltpu.PrefetchScalarGridSpec(
            num_scalar_prefetch=1, grid=(nt,), in_specs=in_specs,
            out_specs=pl.BlockSpec((TM, d), lambda i, p: (i, 0)),
            scratch_shapes=[pltpu.VMEM((2 * 2 * TM * CHUNKS, LANES), F32), pltpu.SemaphoreType.DMA((2,))]),
        compiler_params=_params("arbitrary"), name="moe_combine",
    )(*args)


def _moe(x, g, mods, w_router_t, b_router, wg, wu, wd, layer, *, n_x_tiles, g_final=None):
    r = x.shape[0]
    n_blocks = -(-2 * r // MOE_BLOCK) + N_EXPERTS
    e01, w01, rank01, counts = _router(x, g, mods, w_router_t, b_router, seq_len=n_x_tiles * TM)
    pos, blk_e, n_used, fill = _dispatch_plan(e01, rank01, counts, n_blocks)
    xs = _dispatch(pos, fill, x, g, mods, n_blocks, n_x_tiles=n_x_tiles)
    ys = _experts(blk_e, n_used, xs, wg, wu, wd, layer)
    return _combine(pos, x, mods, w01.T, ys, n_x_tiles=n_x_tiles, g_final=g_final)


def _rope_tables(seq_len, n_rows_total):
    half = HEAD_DIM // 2
    inv_freq = ROPE_BASE ** (-jnp.arange(0, half, 2, dtype=F32) / half)
    first = jnp.arange(half) < (half // 2)

    def tab(n):
        ang = jnp.arange(n, dtype=F32)[:, None] * inv_freq
        ang = jnp.concatenate([ang, ang], axis=-1)
        c, s = jnp.cos(ang), jnp.sin(ang)
        return jnp.stack([c, jnp.where(first, -s, 0.0), jnp.where(first, 0.0, s)], axis=1)

    ident = jnp.stack([jnp.ones((half,), F32), jnp.zeros((half,), F32), jnp.zeros((half,), F32)])
    reps = LANES // HEAD_DIM
    n_ctx_rows = (n_rows_total - seq_len) // GRID_W
    row = jnp.concatenate([tab(seq_len // GRID_W), jnp.broadcast_to(ident, (n_ctx_rows, 3, half))], axis=0)
    row = jnp.tile(jnp.concatenate([row, jnp.zeros_like(row)], axis=-1), (1, 1, reps))
    col = jnp.stack([tab(GRID_W), jnp.broadcast_to(ident, (GRID_W, 3, half))])
    col = jnp.tile(jnp.concatenate([jnp.zeros_like(col), col], axis=-1), (1, TM // GRID_W, 1, reps))
    return row, col.transpose(0, 2, 1, 3)


def kernel(x, c, ctx, c_ctx, w_mod, b_mod, g_mix, g_ffn, g_final, conv_w_in, conv_w, conv_w_out,
           gla_w_proj, gla_w_gk1, gla_w_gk2, gla_b_gk, gla_g_norm, gla_w_out, swa_w_qkv, swa_sinks,
           swa_w_out, na_w_qkv, na_rpb, na_w_out, router_w, router_b, moe_w_gate, moe_w_up, moe_w_down):
    seq_len, d = x.shape[1], x.shape[2]
    ctx_len = ctx.shape[1]
    assert x.shape[0] == 1 and ctx_len == TM and seq_len % TM == 0 and d == D_MODEL
    assert seq_len % GRID_W == 0 and seq_len // GRID_W >= NA_KH
    n_x_tiles = seq_len // TM
    rows = seq_len + ctx_len
    xs = jnp.concatenate([x[0], ctx[0]], axis=0)
    mods_all = _ada_params(c, c_ctx, w_mod, b_mod)
    w_router = jnp.pad(router_w.astype(F32), ((0, 0), (0, LANES - N_EXPERTS)))
    w_router_t = jnp.stack(_split2(w_router))
    n_mixers = 4

    for i in range(DEPTH):
        kind, j = i % n_mixers, i // n_mixers
        mods = mods_all[i]
        if kind == 0:
            u = _project(xs, g_mix[i], mods, conv_w_in[j].astype(BF16), seq_len=seq_len)
            xs = _conv_mixer(xs, u, conv_w[j], mods, conv_w_out[j].astype(BF16), seq_len=seq_len)
        elif kind == 1:
            n_extra = LANES
            w_low = jnp.concatenate([gla_w_gk1[j, 0], gla_w_gk1[j, 1],
                                     jnp.zeros((d, n_extra - 2 * GLA_RANK), F32)], axis=1).astype(BF16)
            u, r_low = _project(xs, g_mix[i], mods, gla_w_proj[j].astype(BF16), seq_len=seq_len,
                                w_extra=w_low)
            w2pad = jnp.stack(
                [jnp.zeros((n_extra, GLA_DK), F32).at[k * GLA_RANK:(k + 1) * GLA_RANK].set(gla_w_gk2[j, k])
                 for k in range(2)]).astype(BF16)
            o_f, o_b = _gla_scan(u, r_low, w2pad, gla_b_gk[j])
            xs = _gla_out(xs, o_f, o_b, u, gla_g_norm[j], mods, gla_w_out[j].astype(BF16), seq_len=seq_len)
        elif kind == 2:
            rope = _rope_tables(seq_len, rows)
            u = _project(xs, g_mix[i], mods, swa_w_qkv[j].astype(BF16), seq_len=seq_len,
                         q_cols=SWA_HEADS * HEAD_DIM, q_scale=HEAD_DIM ** -0.5 * LOG2E, rope=rope,
                         rope_cols=(SWA_HEADS + SWA_KV_HEADS) * HEAD_DIM)
            ypre = _swa_attention(u, swa_sinks[j], seq_len=seq_len)
            xs = _out_proj(xs, ypre, mods, swa_w_out[j].astype(BF16), seq_len=seq_len)
        else:
            u = _project(xs, g_mix[i], mods, na_w_qkv[j].astype(BF16), seq_len=seq_len,
                         q_cols=NA_HEADS * HEAD_DIM, q_scale=HEAD_DIM ** -0.5 * LOG2E)
            ypre = _na_attention(u, _na_bias_table(na_rpb[j]), n_rows=seq_len // GRID_W)
            xs = _out_proj(xs, ypre, mods, na_w_out[j].astype(BF16), seq_len=seq_len)
        xs = _moe(xs, g_ffn[i], mods, w_router_t, router_b, moe_w_gate, moe_w_up, moe_w_down, i,
                  n_x_tiles=n_x_tiles, g_final=g_final if i == DEPTH - 1 else None)
    return xs[None]
```

```python
import functools

import jax
import jax.numpy as jnp
from jax import lax
from jax.experimental import pallas as pl
from jax.experimental.pallas import tpu as pltpu

F32 = jnp.float32
BF16 = jnp.bfloat16
HIGHEST = lax.Precision.HIGHEST

D_MODEL = 1024
DEPTH = 4
GRID_W = 64
EPS = 1e-6
CONV_W = 3
GLA_HEADS = 4
GLA_DK = D_MODEL // 2
GLA_DV = D_MODEL
GLA_HK = GLA_DK // GLA_HEADS
GLA_HV = GLA_DV // GLA_HEADS
GLA_RANK = 16
GLA_NORMALIZER = 16.0
GLA_CHUNK = 64
HEAD_DIM = 64
SWA_HEADS = D_MODEL // HEAD_DIM
SWA_KV_HEADS = SWA_HEADS // 4
SWA_WINDOW = 128
ROPE_BASE = 10000.0
NA_HEADS = D_MODEL // HEAD_DIM
NA_KH = 8
NA_KW = 16
N_EXPERTS = 16
N_GROUPS = 4
EXPERTS_PER_GROUP = N_EXPERTS // N_GROUPS
D_EXPERT = D_MODEL // 2

TM = 256
MOE_BLOCK = 512
LANES = 128
SUBLANES = 8
CHUNKS = D_MODEL // LANES
NEG = -1e30
LOG2E = 1.4426950408889634
VMEM_LIMIT = 56 * 1024 * 1024

_NT = (((1,), (1,)), ((), ()))
_TN = (((0,), (0,)), ((), ()))


def _params(*sem):
    return pltpu.CompilerParams(dimension_semantics=sem, vmem_limit_bytes=VMEM_LIMIT)


def _norm_mod(x, g, scale, shift):
    ms = jnp.mean(x * x, axis=-1, keepdims=True)
    return (x * lax.rsqrt(ms + EPS) * g) * (1.0 + scale) + shift


def _silu(x):
    return x * jax.nn.sigmoid(x)


def _split2(x):
    hi = x.astype(BF16)
    return hi, (x - hi.astype(F32)).astype(BF16)


def _ada_kernel(cond_ref, w_ref, b_ref, o_ref):
    s = _silu(cond_ref[...])
    o_ref[0] = jnp.dot(s, w_ref[0], precision=HIGHEST, preferred_element_type=F32) + b_ref[0]


def _ada_params(c, c_ctx, w_mod, b_mod):
    depth, d, n = w_mod.shape
    nc = n // 4
    cond = jnp.zeros((8, d), F32).at[0].set(c[0]).at[1].set(c_ctx)
    out = pl.pallas_call(
        _ada_kernel,
        out_shape=jax.ShapeDtypeStruct((depth, 8, n), F32),
        grid=(depth, n // nc),
        in_specs=[pl.BlockSpec((8, d), lambda l, j: (0, 0)),
                  pl.BlockSpec((1, d, nc), lambda l, j: (l, 0, j)),
                  pl.BlockSpec((1, 1, nc), lambda l, j: (l, 0, j))],
        out_specs=pl.BlockSpec((1, 8, nc), lambda l, j: (l, 0, j)),
        compiler_params=_params("parallel", "parallel"),
        name="ada_params",
    )(cond, w_mod, b_mod.reshape(depth, 1, n))
    return out[:, :2].reshape(depth, 2, 6, d)


def _row_tile(rows, cap):
    return next(t for t in (1280, 640, TM) if t <= cap and rows % t == 0)


def _mod_rows(mod_ref, tb, seq_len):
    row = pl.program_id(0) * tb + lax.broadcasted_iota(jnp.int32, (tb, 1), 0)
    is_ctx = row >= seq_len
    return lambda j: jnp.where(is_ctx, mod_ref[1, j:j + 1, :], mod_ref[0, j:j + 1, :])


def _proj_kernel(*refs, tb, seq_len, n_rows, n_chunk, q_cols, q_scale, rope_cols, has_extra, fused):
    if fused:
        pos_ref, refs = refs[0], refs[1:]
    x_ref, g_ref, mod_ref, w_ref = refs[:4]
    rest = refs[4:]
    n_main = w_ref.shape[1]
    if has_extra:
        wx_ref, rest = rest[0], rest[1:]
    if rope_cols:
        row_ref, col_ref = rest[:2]
        rest = rest[2:]
        rpt = TM // GRID_W
        cos_t, sa_t, sb_t = [
            jnp.concatenate([jnp.broadcast_to(row_ref[a, k:k + 1, :], (GRID_W, LANES)) for a in range(rpt)],
                            axis=0) + col_ref[0, k]
            for k in range(3)]
    if fused:
        modp_ref, wc_ref, ys_ref = rest[:3]
        rest = rest[3:]
    o_ref = rest[0]
    if fused:
        xo_ref, buf, sem = rest[-3:]
        i = pl.program_id(0)
        nt = pl.num_programs(0)
        slot = i % 2
        slab_rows = TM * CHUNKS
        where = lambda s, k: (s * 2 + k) * slab_rows

        def start(t, base, s):
            for k in range(2):
                p = pl.multiple_of(pos_ref[k * n_rows + base + t], CHUNKS)
                dst = buf.at[pl.ds(pl.multiple_of(where(s, k) + t * CHUNKS, CHUNKS), CHUNKS)]
                pltpu.make_async_copy(ys_ref.at[pl.ds(p, CHUNKS)], dst, sem.at[s]).start(priority=k)

        def wait(s):
            for k in range(2):
                pltpu.make_async_copy(ys_ref.at[pl.ds(0, slab_rows)],
                                      buf.at[pl.ds(pl.multiple_of(where(s, k), slab_rows), slab_rows)],
                                      sem.at[s]).wait()

        @pl.when(i == 0)
        def _():
            def body(t, carry):
                start(t, 0, 0)
                return carry
            lax.fori_loop(0, TM, body, 0)

        wait(slot)
        wc = wc_ref[...]
        y = wc[:, 0:1] * _from_slabs(buf, where(slot, 0), TM) + wc[:, 1:2] * _from_slabs(buf, where(slot, 1), TM)
        x = x_ref[...] + _mod_rows(modp_ref, tb, seq_len)(5) * y
        xo_ref[...] = x
        nxt = jnp.where(i + 1 < nt, i + 1, 0) * TM
        for t in range(TM):
            start(t, nxt, 1 - slot)
    else:
        x = x_ref[...]
    pick = _mod_rows(mod_ref, tb, seq_len)
    h = _norm_mod(x, g_ref[...], pick(1), pick(0)).astype(BF16)
    for c0 in range(0, n_main, n_chunk):
        c1 = min(c0 + n_chunk, n_main)
        y = jnp.dot(h, w_ref[:, c0:c1], preferred_element_type=F32)
        if c0 < q_cols:
            y = y * q_scale
        if c0 >= rope_cols:
            o_ref[:, c0:c1] = y.astype(o_ref.dtype)
            continue
        for s0 in range(c0, c1, LANES):
            ys = y[:, s0 - c0:s0 - c0 + LANES]
            if s0 < rope_cols:
                ys = ys * cos_t + pltpu.roll(ys, LANES - 16, 1) * sa_t + pltpu.roll(ys, 16, 1) * sb_t
            o_ref[:, s0:s0 + LANES] = ys.astype(o_ref.dtype)
    if has_extra:
        rest[1][...] = jnp.dot(h, wx_ref[...], preferred_element_type=F32)
    if fused:
        @pl.when(i == nt - 1)
        def _():
            wait(1 - slot)


def _project(x, g, mods, w, *, seq_len, n_chunk=512, q_cols=0, q_scale=1.0, rope=None, rope_cols=0,
             w_extra=None, pending=None):
    r, d = x.shape
    n_main = w.shape[1]
    n_extra = 0 if w_extra is None else w_extra.shape[1]
    fused = pending is not None
    tb = TM if (rope_cols or fused) else _row_tile(r, 1280)
    im = lambda f: (lambda i, *_: f(i))
    in_specs = [pl.BlockSpec((tb, d), im(lambda i: (i, 0))),
                pl.BlockSpec((1, d), im(lambda i: (0, 0))),
                pl.BlockSpec((2, 6, d), im(lambda i: (0, 0, 0))),
                pl.BlockSpec((d, n_main), im(lambda i: (0, 0)))]
    args = [x, g.reshape(1, d), mods, w]
    if n_extra:
        in_specs.append(pl.BlockSpec((d, n_extra), im(lambda i: (0, 0))))
        args.append(w_extra)
    if rope_cols:
        rpt = TM // GRID_W
        in_specs += [pl.BlockSpec((rpt, 3, LANES), im(lambda i: (i, 0, 0))),
                     pl.BlockSpec((1, 3, TM, LANES), im(lambda i: (i // (seq_len // TM), 0, 0, 0)))]
        args += list(rope)
    out_shape = [jax.ShapeDtypeStruct((r, n_main), BF16)]
    out_specs = [pl.BlockSpec((tb, n_main), im(lambda i: (i, 0)))]
    if n_extra:
        out_shape.append(jax.ShapeDtypeStruct((r, n_extra), F32))
        out_specs.append(pl.BlockSpec((tb, n_extra), im(lambda i: (i, 0))))
    scratch = []
    if fused:
        pos, mods_prev, w_cols, ys = pending
        in_specs += [pl.BlockSpec((2, 6, d), im(lambda i: (0, 0, 0))),
                     pl.BlockSpec((tb, 2), im(lambda i: (i, 0))),
                     pl.BlockSpec(memory_space=pl.ANY)]
        args = [pos] + args + [mods_prev, w_cols, ys]
        out_shape.append(jax.ShapeDtypeStruct((r, d), F32))
        out_specs.append(pl.BlockSpec((tb, d), im(lambda i: (i, 0))))
        scratch = [pltpu.VMEM((2 * 2 * TM * CHUNKS, LANES), F32), pltpu.SemaphoreType.DMA((2,))]
    out = pl.pallas_call(
        functools.partial(_proj_kernel, tb=tb, seq_len=seq_len, n_rows=r, n_chunk=n_chunk, q_cols=q_cols,
                          q_scale=q_scale, rope_cols=rope_cols, has_extra=bool(n_extra), fused=fused),
        out_shape=out_shape,
        grid_spec=pltpu.PrefetchScalarGridSpec(
            num_scalar_prefetch=int(fused), grid=(r // tb,), in_specs=in_specs, out_specs=out_specs,
            scratch_shapes=scratch),
        compiler_params=_params("arbitrary" if fused else "parallel"),
        name="combine_norm_proj" if fused else "norm_proj",
    )(*args)
    if fused:
        return (out[-1],) + tuple(out[:-1])
    return out if n_extra else out[0]


def _out_proj_kernel(x_ref, y_ref, mod_ref, w_ref, o_ref, *, tb, seq_len):
    y = jnp.dot(y_ref[...], w_ref[...], preferred_element_type=F32)
    o_ref[...] = x_ref[...] + _mod_rows(mod_ref, tb, seq_len)(2) * y


def _out_proj(x, ypre, mods, w, *, seq_len):
    r, d = x.shape
    tb = _row_tile(r, 1280)
    return pl.pallas_call(
        functools.partial(_out_proj_kernel, tb=tb, seq_len=seq_len),
        out_shape=jax.ShapeDtypeStruct((r, d), F32),
        grid=(r // tb,),
        in_specs=[pl.BlockSpec((tb, d), lambda i: (i, 0)),
                  pl.BlockSpec((tb, d), lambda i: (i, 0)),
                  pl.BlockSpec((2, 6, d), lambda i: (0, 0, 0)),
                  pl.BlockSpec((d, d), lambda i: (0, 0))],
        out_specs=pl.BlockSpec((tb, d), lambda i: (i, 0)),
        compiler_params=_params("parallel"), name="out_proj",
    )(x, ypre, mods, w)


def _conv_kernel(x_ref, bg_ref, cg_ref, v_ref, cgp_ref, vp_ref, cgn_ref, vn_ref, cw_ref, mod_ref, w_ref,
                 o_ref, *, tb, seq_len, n_rows):
    z = cg_ref[...].astype(F32) * v_ref[...].astype(F32)
    z_prev = cgp_ref[SUBLANES - 1:SUBLANES, :].astype(F32) * vp_ref[SUBLANES - 1:SUBLANES, :].astype(F32)
    z_next = cgn_ref[0:1, :].astype(F32) * vn_ref[0:1, :].astype(F32)
    row = lax.broadcasted_iota(jnp.int32, z.shape, 0)
    pos = pl.program_id(0) * tb + row
    z_dn = jnp.where(row == 0, z_prev, pltpu.roll(z, 1, 0))
    z_up = jnp.where(row == tb - 1, z_next, pltpu.roll(z, tb - 1, 0))
    z_dn = jnp.where(jnp.logical_or(pos == 0, pos == seq_len), 0.0, z_dn)
    z_up = jnp.where(jnp.logical_or(pos == seq_len - 1, pos == n_rows - 1), 0.0, z_up)
    conv = z_dn * cw_ref[0:1, :] + z * cw_ref[1:2, :] + z_up * cw_ref[2:3, :]
    ypre = (bg_ref[...].astype(F32) * conv).astype(BF16)
    y = jnp.dot(ypre, w_ref[...], preferred_element_type=F32)
    o_ref[...] = x_ref[...] + _mod_rows(mod_ref, tb, seq_len)(2) * y


def _conv_mixer(x, u, conv_w, mods, w_out, *, seq_len):
    r, d = x.shape
    tb = _row_tile(r, 640)
    n8 = r // SUBLANES
    t8 = tb // SUBLANES
    tile = lambda c: pl.BlockSpec((tb, d), lambda i: (i, c))
    prev = lambda c: pl.BlockSpec((SUBLANES, d), lambda i: (jnp.maximum(i * t8 - 1, 0), c))
    nxt = lambda c: pl.BlockSpec((SUBLANES, d), lambda i: (jnp.minimum((i + 1) * t8, n8 - 1), c))
    return pl.pallas_call(
        functools.partial(_conv_kernel, tb=tb, seq_len=seq_len, n_rows=r),
        out_shape=jax.ShapeDtypeStruct((r, d), F32),
        grid=(r // tb,),
        in_specs=[pl.BlockSpec((tb, d), lambda i: (i, 0)), tile(0), tile(1), tile(2),
                  prev(1), prev(2), nxt(1), nxt(2),
                  pl.BlockSpec((CONV_W, d), lambda i: (0, 0)),
                  pl.BlockSpec((2, 6, d), lambda i: (0, 0, 0)),
                  pl.BlockSpec((d, d), lambda i: (0, 0))],
        out_specs=pl.BlockSpec((tb, d), lambda i: (i, 0)),
        compiler_params=_params("parallel"), name="conv_mixer",
    )(x, u, u, u, u, u, u, u, conv_w, mods, w_out)


def _log_sigmoid(z):
    return jnp.minimum(z, 0.0) - jnp.log1p(jnp.exp(-jnp.abs(z)))


def _gla_chunk(q_ref, k_ref, v_ref, gk, o_ref, st_ref, d, c, reverse):
    C = GLA_CHUNK
    rows = slice(c * C, (c + 1) * C)
    ii = lax.broadcasted_iota(jnp.int32, (C, C), 0)
    jj = lax.broadcasted_iota(jnp.int32, (C, C), 1)
    causal = (jj >= ii) if reverse else (jj <= ii)
    ones = jnp.where(causal, 1.0, 0.0).astype(BF16)
    g_hi, g_lo = _split2(gk[rows])
    bcum = (jnp.dot(ones, g_hi, preferred_element_type=F32)
            + jnp.dot(ones, g_lo, preferred_element_type=F32))
    total = bcum[0:1, :] if reverse else bcum[C - 1:C, :]
    q = q_ref[rows, :].astype(F32) * (GLA_HK ** -0.5)
    k = k_ref[rows, :].astype(F32)
    q_dec = (q * jnp.exp(bcum)).astype(BF16)
    k_inv = (k * jnp.exp(-bcum)).astype(BF16)
    k_end = (k * jnp.exp(total - bcum)).astype(BF16)
    decay = jnp.exp(total)
    for h in range(GLA_HEADS):
        kcols = slice(h * GLA_HK, (h + 1) * GLA_HK)
        vcols = slice(h * GLA_HV, (h + 1) * GLA_HV)
        vh = v_ref[rows, vcols]
        att = lax.dot_general(q_dec[:, kcols], k_inv[:, kcols], _NT, preferred_element_type=F32)
        att = jnp.where(causal, att, 0.0).astype(BF16)
        st = st_ref[d, h]
        o = (jnp.dot(att, vh, preferred_element_type=F32)
             + lax.dot_general(q_dec[:, kcols], st.astype(BF16), _NT, preferred_element_type=F32))
        st_ref[d, h] = st * decay[:, kcols] + lax.dot_general(vh, k_end[:, kcols], _TN,
                                                              preferred_element_type=F32)
        o_ref[rows, vcols] = o.astype(o_ref.dtype)


def _gla_kernel(qf_ref, kf_ref, vf_ref, rf_ref, qb_ref, kb_ref, vb_ref, rb_ref, w2_ref, b_ref,
                of_ref, ob_ref, st_ref):
    @pl.when(pl.program_id(0) == 0)
    def _():
        st_ref[...] = jnp.zeros_like(st_ref)

    def gates(r_ref, d):
        z = jnp.dot(r_ref[...].astype(BF16), w2_ref[d], preferred_element_type=F32) + b_ref[d]
        return _log_sigmoid(z) * (1.0 / GLA_NORMALIZER)

    gk_f, gk_b = gates(rf_ref, 0), gates(rb_ref, 1)
    n_chunks = TM // GLA_CHUNK
    for c in range(n_chunks):
        _gla_chunk(qf_ref, kf_ref, vf_ref, gk_f, of_ref, st_ref, 0, c, False)
        _gla_chunk(qb_ref, kb_ref, vb_ref, gk_b, ob_ref, st_ref, 1, n_chunks - 1 - c, True)


def _gla_scan(u, r_low, w2pad, b_gk):
    rws = u.shape[0]
    nt = rws // TM
    fwd = lambda s: (s + nt - 1) % nt
    bwd = lambda s: nt - 1 - s
    qkvr = lambda order: [pl.BlockSpec((TM, GLA_DK), lambda s: (order(s), 0)),
                          pl.BlockSpec((TM, GLA_DK), lambda s: (order(s), 1)),
                          pl.BlockSpec((TM, GLA_DV), lambda s: (order(s), 1)),
                          pl.BlockSpec((TM, LANES), lambda s: (order(s), 0))]
    out = jax.ShapeDtypeStruct((rws, GLA_DV), BF16)
    return pl.pallas_call(
        _gla_kernel,
        out_shape=[out, out],
        grid=(nt,),
        in_specs=qkvr(fwd) + qkvr(bwd) + [pl.BlockSpec((2, LANES, GLA_DK), lambda s: (0, 0, 0)),
                                          pl.BlockSpec((2, 1, GLA_DK), lambda s: (0, 0, 0))],
        out_specs=[pl.BlockSpec((TM, GLA_DV), lambda s: (fwd(s), 0)),
                   pl.BlockSpec((TM, GLA_DV), lambda s: (bwd(s), 0))],
        scratch_shapes=[pltpu.VMEM((2, GLA_HEADS, GLA_HV, GLA_HK), F32)],
        compiler_params=_params("arbitrary"), name="gla_scan",
    )(u, u, u, r_low, u, u, u, r_low, w2pad, b_gk.reshape(2, 1, GLA_DK))


def _gla_out_kernel(x_ref, of_ref, ob_ref, g_ref, gn_ref, mod_ref, w_ref, o_ref, *, tb, seq_len):
    o = of_ref[...].astype(F32) + ob_ref[...].astype(F32)
    parts = []
    for h in range(GLA_HEADS):
        oh = o[:, h * GLA_HV:(h + 1) * GLA_HV]
        parts.append(oh * lax.rsqrt(jnp.mean(oh * oh, axis=-1, keepdims=True) + EPS) * gn_ref[...])
    ypre = (jnp.concatenate(parts, axis=1) * _silu(g_ref[...].astype(F32))).astype(BF16)
    y = jnp.dot(ypre, w_ref[...], preferred_element_type=F32)
    o_ref[...] = x_ref[...] + _mod_rows(mod_ref, tb, seq_len)(2) * y


def _gla_out(x, o_f, o_b, u, g_norm, mods, w, *, seq_len):
    r, d = x.shape
    tb = _row_tile(r, 640)
    tile = lambda c: pl.BlockSpec((tb, d), lambda i: (i, c))
    return pl.pallas_call(
        functools.partial(_gla_out_kernel, tb=tb, seq_len=seq_len),
        out_shape=jax.ShapeDtypeStruct((r, d), F32),
        grid=(r // tb,),
        in_specs=[tile(0), tile(0), tile(0), tile(2),
                  pl.BlockSpec((1, GLA_HV), lambda i: (0, 0)),
                  pl.BlockSpec((2, 6, d), lambda i: (0, 0, 0)),
                  pl.BlockSpec((d, d), lambda i: (0, 0))],
        out_specs=tile(0),
        compiler_params=_params("parallel"), name="gla_out",
    )(x, o_f, o_b, u, g_norm.reshape(1, GLA_HV), mods, w)


def _swa_kernel(sink_ref, q_ref, kvp_ref, kvo_ref, kvn_ref, kvc_ref, o_ref, kv_buf, *, seq_len):
    i = pl.program_id(0)
    half = TM // 2
    kv_buf[0:half] = kvp_ref[...]
    kv_buf[half:half + TM] = kvo_ref[...]
    kv_buf[half + TM:2 * TM] = kvn_ref[...]
    kv_buf[2 * TM:3 * TM] = kvc_ref[...]
    nk = 3 * TM
    qpos = i * TM + lax.broadcasted_iota(jnp.int32, (TM, nk), 0)
    col = lax.broadcasted_iota(jnp.int32, (TM, nk), 1)
    kpos = i * TM - half + col
    local = ((jnp.abs(kpos - qpos) <= SWA_WINDOW) & (kpos >= 0) & (kpos < seq_len) & (qpos < seq_len))
    valid = local | (col >= 2 * TM)
    kv_w = SWA_KV_HEADS * HEAD_DIM
    group = SWA_HEADS // SWA_KV_HEADS
    for h in range(SWA_HEADS):
        kv = h // group
        qh = q_ref[:, h * HEAD_DIM:(h + 1) * HEAD_DIM]
        kh = kv_buf[:, kv * HEAD_DIM:(kv + 1) * HEAD_DIM]
        vh = kv_buf[:, kv_w + kv * HEAD_DIM:kv_w + (kv + 1) * HEAD_DIM]
        s = lax.dot_general(qh, kh, _NT, preferred_element_type=F32)
        s = jnp.where(valid, s, NEG)
        sink = sink_ref[h] * LOG2E
        m = jnp.maximum(jnp.max(s, axis=-1, keepdims=True), sink)
        p = jnp.exp2(s - m)
        l = jnp.sum(p, axis=-1, keepdims=True) + jnp.exp2(sink - m)
        o = jnp.dot(p.astype(BF16), vh, preferred_element_type=F32) / l
        o_ref[:, h * HEAD_DIM:(h + 1) * HEAD_DIM] = o.astype(o_ref.dtype)


def _swa_attention(u, sinks, *, seq_len):
    rws = u.shape[0]
    nt = rws // TM
    half = TM // 2
    n_half = rws // half
    kv_w = 2 * SWA_KV_HEADS * HEAD_DIM
    qw = SWA_HEADS * HEAD_DIM
    kvc = qw // kv_w
    return pl.pallas_call(
        functools.partial(_swa_kernel, seq_len=seq_len),
        out_shape=jax.ShapeDtypeStruct((rws, qw), BF16),
        grid=(nt,),
        in_specs=[pl.BlockSpec(memory_space=pltpu.SMEM),
                  pl.BlockSpec((TM, qw), lambda i: (i, 0)),
                  pl.BlockSpec((half, kv_w), lambda i: (jnp.maximum(2 * i - 1, 0), kvc)),
                  pl.BlockSpec((TM, kv_w), lambda i: (i, kvc)),
                  pl.BlockSpec((half, kv_w), lambda i: (jnp.minimum(2 * i + 2, n_half - 1), kvc)),
                  pl.BlockSpec((TM, kv_w), lambda i: (nt - 1, kvc))],
        out_specs=pl.BlockSpec((TM, qw), lambda i: (i, 0)),
        scratch_shapes=[pltpu.VMEM((3 * TM, kv_w), BF16)],
        compiler_params=_params("parallel"), name="swa_attention",
    )(sinks, u, u, u, u, u)


def _na_kernel(q_ref, kp_ref, ko_ref, kn_ref, kc_ref, vp_ref, vo_ref, vn_ref, vc_ref, bias_ref, o_ref,
               k_buf, v_buf, *, n_rows):
    i = pl.program_id(0)
    for j, (kr, vr) in enumerate(((kp_ref, vp_ref), (ko_ref, vo_ref), (kn_ref, vn_ref), (kc_ref, vc_ref))):
        k_buf[j * TM:(j + 1) * TM] = kr[...]
        v_buf[j * TM:(j + 1) * TM] = vr[...]
    n_loc = 3 * TM
    rpt = TM // GRID_W
    qi = lax.broadcasted_iota(jnp.int32, (TM, n_loc), 0)
    ki = lax.broadcasted_iota(jnp.int32, (TM, n_loc), 1)
    r = i * rpt + qi // GRID_W
    c = qi % GRID_W
    krow = (i - 1) * rpt + ki // GRID_W
    kcol = ki % GRID_W
    r0 = jnp.clip(r - NA_KH // 2, 0, n_rows - NA_KH)
    c0 = jnp.clip(c - NA_KW // 2, 0, GRID_W - NA_KW)
    valid = ((krow >= r0) & (krow < r0 + NA_KH) & (kcol >= c0) & (kcol < c0 + NA_KW) & (r < n_rows))
    for h in range(NA_HEADS):
        cols = slice(h * HEAD_DIM, (h + 1) * HEAD_DIM)
        qh = q_ref[:, cols]
        s = lax.dot_general(qh, k_buf[:, cols], _NT, preferred_element_type=F32)
        s_loc = jnp.where(valid, s[:, :n_loc] + bias_ref[h], NEG)
        s_ctx = s[:, n_loc:]
        m = jnp.maximum(jnp.max(s_loc, axis=-1, keepdims=True), jnp.max(s_ctx, axis=-1, keepdims=True))
        p_loc = jnp.exp2(s_loc - m)
        p_ctx = jnp.exp2(s_ctx - m)
        l = jnp.sum(p_loc, axis=-1, keepdims=True) + jnp.sum(p_ctx, axis=-1, keepdims=True)
        o = (jnp.dot(p_loc.astype(BF16), v_buf[0:n_loc, cols], preferred_element_type=F32)
             + jnp.dot(p_ctx.astype(BF16), v_buf[n_loc:, cols], preferred_element_type=F32)) / l
        o_ref[:, cols] = o.astype(o_ref.dtype)


def _na_bias_kernel(t_ref, o_ref, *, lo):
    rpt = TM // GRID_W
    for a in range(rpt):
        for b in range(3 * rpt):
            o_ref[0, a * GRID_W:(a + 1) * GRID_W, b * GRID_W:(b + 1) * GRID_W] = t_ref[0, lo - a + b]


def _na_bias_table(rpb):
    n_heads = rpb.shape[0]
    rpt = TM // GRID_W
    lo = NA_KH - 1 - rpt
    assert lo - (rpt - 1) >= 0 and lo + 3 * rpt <= 2 * NA_KH - 1
    c = jnp.arange(GRID_W)
    col_off = c[None, :] - c[:, None] + NA_KW - 1
    onehot = (col_off[None] == jnp.arange(2 * NA_KW - 1)[:, None, None]).astype(F32)
    tcol = jnp.einsum('hrj,jck->hrck', rpb.astype(F32) * LOG2E, onehot, precision=HIGHEST)
    return pl.pallas_call(
        functools.partial(_na_bias_kernel, lo=lo),
        out_shape=jax.ShapeDtypeStruct((n_heads, TM, 3 * TM), F32),
        grid=(n_heads,),
        in_specs=[pl.BlockSpec((1, 2 * NA_KH - 1, GRID_W, GRID_W), lambda h: (h, 0, 0, 0))],
        out_specs=pl.BlockSpec((1, TM, 3 * TM), lambda h: (h, 0, 0)),
        compiler_params=_params("parallel"), name="na_bias_table",
    )(tcol)


def _na_attention(u, bias, *, n_rows):
    rws = u.shape[0]
    nt = rws // TM
    d = NA_HEADS * HEAD_DIM
    blk = lambda f, c: pl.BlockSpec((TM, d), lambda i: (f(i), c))
    prev = lambda i: jnp.maximum(i - 1, 0)
    own = lambda i: i
    nxt = lambda i: jnp.minimum(i + 1, nt - 1)
    ctx = lambda i: nt - 1
    return pl.pallas_call(
        functools.partial(_na_kernel, n_rows=n_rows),
        out_shape=jax.ShapeDtypeStruct((rws, d), BF16),
        grid=(nt,),
        in_specs=[blk(own, 0), blk(prev, 1), blk(own, 1), blk(nxt, 1), blk(ctx, 1),
                  blk(prev, 2), blk(own, 2), blk(nxt, 2), blk(ctx, 2),
                  pl.BlockSpec((NA_HEADS, TM, 3 * TM), lambda i: (0, 0, 0))],
        out_specs=pl.BlockSpec((TM, d), lambda i: (i, 0)),
        scratch_shapes=[pltpu.VMEM((4 * TM, d), BF16), pltpu.VMEM((4 * TM, d), BF16)],
        compiler_params=_params("parallel"), name="na_attention",
    )(u, u, u, u, u, u, u, u, u, bias)


def _router_kernel(x_ref, g_ref, mod_ref, wr_ref, br_ref, e_ref, w_ref, rank_ref, cnt_ref, carry, *,
                   tb, seq_len):
    @pl.when(pl.program_id(0) == 0)
    def _():
        carry[...] = jnp.zeros_like(carry)

    pick = _mod_rows(mod_ref, tb, seq_len)
    h = _norm_mod(x_ref[...], g_ref[...], pick(4), pick(3))
    h_hi, h_lo = _split2(h)
    lt = (jnp.dot(h_hi, wr_ref[0], preferred_element_type=F32) + jnp.dot(h_hi, wr_ref[1], preferred_element_type=F32)
          + jnp.dot(h_lo, wr_ref[0], preferred_element_type=F32))
    logits = jnp.transpose(lt)[:N_EXPERTS]
    scores = jax.nn.sigmoid(logits)
    biased = scores + br_ref[...]
    row = lambda a, e: a[e:e + 1, :]
    best_g = jnp.zeros((1, tb), jnp.int32)
    best_s = None
    for g in range(N_GROUPS):
        v = [row(biased, g * EXPERTS_PER_GROUP + j) for j in range(EXPERTS_PER_GROUP)]
        gs = None
        for a in range(EXPERTS_PER_GROUP):
            for b in range(a + 1, EXPERTS_PER_GROUP):
                pair = v[a] + v[b]
                gs = pair if gs is None else jnp.maximum(gs, pair)
        if best_s is None:
            best_s = gs
        else:
            better = gs > best_s
            best_g = jnp.where(better, g, best_g)
            best_s = jnp.where(better, gs, best_s)
    picks = []
    for _ in range(2):
        top_v = jnp.full((1, tb), -jnp.inf, F32)
        top_i = jnp.full((1, tb), -1, jnp.int32)
        for e in range(N_EXPERTS):
            ok = best_g == (e // EXPERTS_PER_GROUP)
            for p in picks:
                ok = jnp.logical_and(ok, p != e)
            cand = jnp.where(ok, row(biased, e), -jnp.inf)
            better = cand > top_v
            top_i = jnp.where(better, e, top_i)
            top_v = jnp.where(better, cand, top_v)
        picks.append(top_i)
    e_iota = lax.broadcasted_iota(jnp.int32, (N_EXPERTS, tb), 0)
    sel0 = e_iota == picks[0]
    sel1 = e_iota == picks[1]
    w0 = jnp.sum(jnp.where(sel0, scores, 0.0), axis=0, keepdims=True)
    w1 = jnp.sum(jnp.where(sel1, scores, 0.0), axis=0, keepdims=True)
    tot = w0 + w1
    e_ref[0:1, :] = picks[0]
    e_ref[1:2, :] = picks[1]
    w_ref[0:1, :] = w0 / tot
    w_ref[1:2, :] = w1 / tot
    member = jnp.where(jnp.logical_or(sel0, sel1), 1.0, 0.0)
    earlier = (lax.broadcasted_iota(jnp.int32, (TM, TM), 0)
               < lax.broadcasted_iota(jnp.int32, (TM, TM), 1)).astype(BF16)
    seen = carry[:, 0:1]
    before = []
    for c in range(tb // TM):
        part = member[:, c * TM:(c + 1) * TM]
        before.append(seen + jnp.dot(part.astype(BF16), earlier, preferred_element_type=F32))
        seen = seen + jnp.sum(part, axis=1, keepdims=True)
    before = jnp.concatenate(before, axis=1)
    rank_ref[0:1, :] = jnp.sum(jnp.where(sel0, before, 0.0), axis=0, keepdims=True).astype(jnp.int32)
    rank_ref[1:2, :] = jnp.sum(jnp.where(sel1, before, 0.0), axis=0, keepdims=True).astype(jnp.int32)
    carry[...] = jnp.broadcast_to(seen, carry.shape)
    cnt_ref[...] = carry[...].astype(jnp.int32)


def _router(x, g, mods, w_router_t, b_router, *, seq_len):
    r, d = x.shape
    tb = _row_tile(r, 1280)
    pair = pl.BlockSpec((2, tb), lambda i: (0, i))
    return pl.pallas_call(
        functools.partial(_router_kernel, tb=tb, seq_len=seq_len),
        out_shape=[jax.ShapeDtypeStruct((2, r), jnp.int32), jax.ShapeDtypeStruct((2, r), F32),
                   jax.ShapeDtypeStruct((2, r), jnp.int32), jax.ShapeDtypeStruct((N_EXPERTS, LANES), jnp.int32)],
        grid=(r // tb,),
        in_specs=[pl.BlockSpec((tb, d), lambda i: (i, 0)),
                  pl.BlockSpec((1, d), lambda i: (0, 0)),
                  pl.BlockSpec((2, 6, d), lambda i: (0, 0, 0)),
                  pl.BlockSpec((2, d, LANES), lambda i: (0, 0, 0)),
                  pl.BlockSpec((N_EXPERTS, 1), lambda i: (0, 0))],
        out_specs=[pair, pair, pair, pl.BlockSpec((N_EXPERTS, LANES), lambda i: (0, 0))],
        scratch_shapes=[pltpu.VMEM((N_EXPERTS, LANES), F32)],
        compiler_params=_params("arbitrary"), name="router",
    )(x, g.reshape(1, d), mods, w_router_t, b_router.reshape(N_EXPERTS, 1))


def _dispatch_plan(e01, rank01, counts, n_blocks):
    cnt = counts[:, 0]
    nblk = (cnt + MOE_BLOCK - 1) // MOE_BLOCK
    blk_end = jnp.cumsum(nblk)
    off = (blk_end - nblk) * MOE_BLOCK
    eidx = jnp.arange(N_EXPERTS, dtype=jnp.int32)[:, None, None]
    pos = rank01 + jnp.sum(jnp.where(e01[None] == eidx, off[:, None, None], 0), axis=0)
    blk_e = jnp.sum(jnp.arange(n_blocks, dtype=jnp.int32)[:, None] >= blk_end[None, :], axis=1)
    blk_e = jnp.minimum(blk_e, N_EXPERTS - 1).astype(jnp.int32)
    n_used = blk_end[-1:]
    fill = jnp.concatenate([off + cnt, nblk * MOE_BLOCK - cnt, n_used]).astype(jnp.int32)
    slab_row = (pos * CHUNKS).reshape(-1).astype(jnp.int32)
    return slab_row, blk_e, n_used.astype(jnp.int32), fill


def _zero_fill(fill_ref, zbuf, xs_ref, zsem, n_blocks, wait):
    def copy(start_row, n_rows):
        cp = pltpu.make_async_copy(zbuf.at[pl.ds(0, n_rows * CHUNKS)],
                                   xs_ref.at[pl.ds(pl.multiple_of(start_row * CHUNKS, CHUNKS), n_rows * CHUNKS)],
                                   zsem)
        cp.wait() if wait else cp.start()

    for e in range(N_EXPERTS):
        start, length = fill_ref[e], fill_ref[N_EXPERTS + e]
        piece = MOE_BLOCK // 2
        while piece >= 1:
            @pl.when((length & piece) != 0)
            def _():
                copy(start + (length & (-2 * piece)), piece)
            piece //= 2

    def body(j, carry):
        for part in range(MOE_BLOCK // TM):
            copy(j * MOE_BLOCK + part * TM, TM)
        return carry

    lax.fori_loop(fill_ref[2 * N_EXPERTS], n_blocks, body, 0)


def _to_slabs(ref, base, x):
    n = x.shape[0]
    for c in range(CHUNKS):
        ref[pl.ds(base + c, n, stride=CHUNKS), :] = x[:, c * LANES:(c + 1) * LANES]


def _from_slabs(ref, base, n):
    return jnp.concatenate([ref[pl.ds(base + c, n, stride=CHUNKS), :] for c in range(CHUNKS)], axis=1)


def _dispatch_kernel(pos_ref, fill_ref, x_ref, g_ref, mod_ref, xs_ref, h_scr, zbuf, sem, zsem, *,
                     rows, nt, n_blocks):
    i = pl.program_id(0)
    slot = i % 2
    slab_rows = TM * CHUNKS

    @pl.when(i == 0)
    def _():
        zbuf[...] = jnp.zeros_like(zbuf)
        _zero_fill(fill_ref, zbuf, xs_ref, zsem, n_blocks, wait=False)

    def wait_slot(s):
        for _ in range(2):
            pltpu.make_async_copy(h_scr.at[pl.ds(pl.multiple_of(s * slab_rows, slab_rows), slab_rows)],
                                  xs_ref.at[pl.ds(0, slab_rows)], sem.at[s]).wait()

    @pl.when(i >= 2)
    def _():
        wait_slot(slot)

    h = _norm_mod(x_ref[...], g_ref[...], mod_ref[0, 4:5, :], mod_ref[0, 3:4, :])
    _to_slabs(h_scr, slot * slab_rows, h)
    base = i * TM

    def body(grp, carry):
        for u in range(SUBLANES):
            t = grp * SUBLANES + u
            src = h_scr.at[pl.ds(pl.multiple_of(slot * slab_rows + t * CHUNKS, CHUNKS), CHUNKS)]
            for k in range(2):
                p = pl.multiple_of(pos_ref[k * rows + base + t], CHUNKS)
                pltpu.make_async_copy(src, xs_ref.at[pl.ds(p, CHUNKS)], sem.at[slot]).start(priority=k)
        return carry

    lax.fori_loop(0, TM // SUBLANES, body, 0)

    @pl.when(i == nt - 1)
    def _():
        wait_slot(slot)
        if nt > 1:
            wait_slot(1 - slot)
        _zero_fill(fill_ref, zbuf, xs_ref, zsem, n_blocks, wait=True)


def _dispatch(pos, fill, x, g, mods, n_blocks, *, n_x_tiles):
    r, d = x.shape
    nt = r // TM
    assert MOE_BLOCK % TM == 0 and MOE_BLOCK // 2 <= TM and d == CHUNKS * LANES
    n_rows = n_blocks * MOE_BLOCK
    return pl.pallas_call(
        functools.partial(_dispatch_kernel, rows=r, nt=nt, n_blocks=n_blocks),
        out_shape=jax.ShapeDtypeStruct((n_rows * CHUNKS, LANES), F32),
        grid_spec=pltpu.PrefetchScalarGridSpec(
            num_scalar_prefetch=2, grid=(nt,),
            in_specs=[pl.BlockSpec((TM, d), lambda i, p, f: (i, 0)),
                      pl.BlockSpec((1, d), lambda i, p, f: (0, 0)),
                      pl.BlockSpec((1, 6, d), lambda i, p, f: (i // n_x_tiles, 0, 0))],
            out_specs=pl.BlockSpec(memory_space=pl.ANY),
            scratch_shapes=[pltpu.VMEM((2 * TM * CHUNKS, LANES), F32),
                            pltpu.VMEM((TM * CHUNKS, LANES), F32),
                            pltpu.SemaphoreType.DMA((2,)), pltpu.SemaphoreType.DMA]),
        compiler_params=_params("arbitrary"), name="moe_dispatch",
    )(pos, fill, x, g.reshape(1, d), mods)


def _expert_kernel(be_ref, nu_ref, xs_ref, wg_ref, wu_ref, wd_ref, ys_ref, wg_s, wu_s, wd_s):
    j = pl.program_id(0)

    @pl.when(j < nu_ref[0])
    def _():
        @pl.when(jnp.logical_or(j == 0, be_ref[j] != be_ref[jnp.maximum(j - 1, 0)]))
        def _():
            wg_s[...] = wg_ref[0, 0].astype(BF16)
            wu_s[...] = wu_ref[0, 0].astype(BF16)
            wd_s[...] = wd_ref[0, 0].astype(BF16)

        h = _from_slabs(xs_ref, 0, MOE_BLOCK).astype(BF16)
        a = (_silu(jnp.dot(h, wg_s[...], preferred_element_type=F32))
             * jnp.dot(h, wu_s[...], preferred_element_type=F32))
        _to_slabs(ys_ref, 0, jnp.dot(a.astype(BF16), wd_s[...], preferred_element_type=F32))

    @pl.when(j >= nu_ref[0])
    def _():
        ys_ref[...] = jnp.zeros_like(ys_ref)


def _experts(blk_e, n_used, xs, wg, wu, wd, layer):
    d, de = wg.shape[-2:]
    blk = MOE_BLOCK * CHUNKS
    last = lambda j, nu: jnp.minimum(j, jnp.maximum(nu[0] - 1, 0))
    return pl.pallas_call(
        _expert_kernel,
        out_shape=jax.ShapeDtypeStruct(xs.shape, F32),
        grid_spec=pltpu.PrefetchScalarGridSpec(
            num_scalar_prefetch=2, grid=(xs.shape[0] // blk,),
            in_specs=[pl.BlockSpec((blk, LANES), lambda j, be, nu: (last(j, nu), 0)),
                      pl.BlockSpec((1, 1, d, de), lambda j, be, nu: (layer, be[last(j, nu)], 0, 0)),
                      pl.BlockSpec((1, 1, d, de), lambda j, be, nu: (layer, be[last(j, nu)], 0, 0)),
                      pl.BlockSpec((1, 1, de, d), lambda j, be, nu: (layer, be[last(j, nu)], 0, 0))],
            out_specs=pl.BlockSpec((blk, LANES), lambda j, be, nu: (j, 0)),
            scratch_shapes=[pltpu.VMEM((d, de), BF16), pltpu.VMEM((d, de), BF16), pltpu.VMEM((de, d), BF16)]),
        compiler_params=_params("arbitrary"), name="moe_experts",
    )(blk_e, n_used, xs, wg, wu, wd)


def _combine_kernel(pos_ref, x_ref, mod_ref, w_ref, ys_ref, *rest, rows, nt, final):
    if final:
        gf_ref, o_ref, buf, sem = rest
    else:
        o_ref, buf, sem = rest
    i = pl.program_id(0)
    slot = i % 2
    slab_rows = TM * CHUNKS
    where = lambda s, k: (s * 2 + k) * slab_rows

    def issue(row_tile, s):
        base = row_tile * TM

        def body(grp, carry):
            for u in range(SUBLANES):
                t = grp * SUBLANES + u
                for k in range(2):
                    p = pl.multiple_of(pos_ref[k * rows + base + t], CHUNKS)
                    dst = buf.at[pl.ds(pl.multiple_of(where(s, k) + t * CHUNKS, CHUNKS), CHUNKS)]
                    pltpu.make_async_copy(ys_ref.at[pl.ds(p, CHUNKS)], dst, sem.at[s]).start(priority=k)
            return carry

        lax.fori_loop(0, TM // SUBLANES, body, 0)

    @pl.when(i == 0)
    def _():
        issue(0, 0)

    @pl.when(i + 1 < nt)
    def _():
        issue(i + 1, 1 - slot)

    for k in range(2):
        pltpu.make_async_copy(ys_ref.at[pl.ds(0, slab_rows)],
                              buf.at[pl.ds(pl.multiple_of(where(slot, k), slab_rows), slab_rows)],
                              sem.at[slot]).wait()
    w = w_ref[...]
    y = w[:, 0:1] * _from_slabs(buf, where(slot, 0), TM) + w[:, 1:2] * _from_slabs(buf, where(slot, 1), TM)
    x = x_ref[...] + mod_ref[0, 5:6, :] * y
    if final:
        x = x * lax.rsqrt(jnp.mean(x * x, axis=-1, keepdims=True) + EPS) * gf_ref[...]
    o_ref[...] = x


def _combine(pos, x, mods, w_cols, ys, *, n_x_tiles, g_final=None):
    r, d = x.shape
    final = g_final is not None
    nt = n_x_tiles if final else r // TM
    in_specs = [pl.BlockSpec((TM, d), lambda i, p: (i, 0)),
                pl.BlockSpec((1, 6, d), lambda i, p: (i // n_x_tiles, 0, 0)),
                pl.BlockSpec((TM, 2), lambda i, p: (i, 0)),
                pl.BlockSpec(memory_space=pl.ANY)]
    args = [pos, x, mods, w_cols, ys]
    if final:
        in_specs.append(pl.BlockSpec((1, d), lambda i, p: (0, 0)))
        args.append(g_final.reshape(1, d))
    return pl.pallas_call(
        functools.partial(_combine_kernel, rows=r, nt=nt, final=final),
        out_shape=jax.ShapeDtypeStruct((nt * TM, d), F32),
        grid_spec=pltpu.PrefetchScalarGridSpec(
            num_scalar_prefetch=1, grid=(nt,), in_specs=in_specs,
            out_specs=pl.BlockSpec((TM, d), lambda i, p: (i, 0)),
            scratch_shapes=[pltpu.VMEM((2 * 2 * TM * CHUNKS, LANES), F32), pltpu.SemaphoreType.DMA((2,))]),
        compiler_params=_params("arbitrary"), name="moe_combine",
    )(*args)


def _moe(x, g, mods, w_router_t, b_router, wg, wu, wd, layer, *, n_x_tiles, g_final=None, defer=False):
    r = x.shape[0]
    n_blocks = -(-2 * r // MOE_BLOCK) + N_EXPERTS
    e01, w01, rank01, counts = _router(x, g, mods, w_router_t, b_router, seq_len=n_x_tiles * TM)
    pos, blk_e, n_used, fill = _dispatch_plan(e01, rank01, counts, n_blocks)
    xs = _dispatch(pos, fill, x, g, mods, n_blocks, n_x_tiles=n_x_tiles)
    ys = _experts(blk_e, n_used, xs, wg, wu, wd, layer)
    if defer:
        return pos, mods, w01.T, ys
    return _combine(pos, x, mods, w01.T, ys, n_x_tiles=n_x_tiles, g_final=g_final)


def _rope_tables(seq_len, n_rows_total):
    half = HEAD_DIM // 2
    inv_freq = ROPE_BASE ** (-jnp.arange(0, half, 2, dtype=F32) / half)
    first = jnp.arange(half) < (half // 2)

    def tab(n):
        ang = jnp.arange(n, dtype=F32)[:, None] * inv_freq
        ang = jnp.concatenate([ang, ang], axis=-1)
        c, s = jnp.cos(ang), jnp.sin(ang)
        return jnp.stack([c, jnp.where(first, -s, 0.0), jnp.where(first, 0.0, s)], axis=1)

    ident = jnp.stack([jnp.ones((half,), F32), jnp.zeros((half,), F32), jnp.zeros((half,), F32)])
    reps = LANES // HEAD_DIM
    n_ctx_rows = (n_rows_total - seq_len) // GRID_W
    row = jnp.concatenate([tab(seq_len // GRID_W), jnp.broadcast_to(ident, (n_ctx_rows, 3, half))], axis=0)
    row = jnp.tile(jnp.concatenate([row, jnp.zeros_like(row)], axis=-1), (1, 1, reps))
    col = jnp.stack([tab(GRID_W), jnp.broadcast_to(ident, (GRID_W, 3, half))])
    col = jnp.tile(jnp.concatenate([jnp.zeros_like(col), col], axis=-1), (1, TM // GRID_W, 1, reps))
    return row, col.transpose(0, 2, 1, 3)


def kernel(x, c, ctx, c_ctx, w_mod, b_mod, g_mix, g_ffn, g_final, conv_w_in, conv_w, conv_w_out,
           gla_w_proj, gla_w_gk1, gla_w_gk2, gla_b_gk, gla_g_norm, gla_w_out, swa_w_qkv, swa_sinks,
           swa_w_out, na_w_qkv, na_rpb, na_w_out, router_w, router_b, moe_w_gate, moe_w_up, moe_w_down):
    seq_len, d = x.shape[1], x.shape[2]
    ctx_len = ctx.shape[1]
    assert x.shape[0] == 1 and ctx_len == TM and seq_len % TM == 0 and d == D_MODEL
    assert seq_len % GRID_W == 0 and seq_len // GRID_W >= NA_KH
    n_x_tiles = seq_len // TM
    rows = seq_len + ctx_len
    xs = jnp.concatenate([x[0], ctx[0]], axis=0)
    mods_all = _ada_params(c, c_ctx, w_mod, b_mod)
    w_router = jnp.pad(router_w.astype(F32), ((0, 0), (0, LANES - N_EXPERTS)))
    w_router_t = jnp.stack(_split2(w_router))
    n_mixers = 4

    pending = None
    for i in range(DEPTH):
        kind, j = i % n_mixers, i // n_mixers
        mods = mods_all[i]
        if kind == 0:
            if pending is not None:
                xs = _combine(pending[0], xs, *pending[1:], n_x_tiles=n_x_tiles)
            u = _project(xs, g_mix[i], mods, conv_w_in[j].astype(BF16), seq_len=seq_len)
            xs = _conv_mixer(xs, u, conv_w[j], mods, conv_w_out[j].astype(BF16), seq_len=seq_len)
        elif kind == 1:
            n_extra = LANES
            w_low = jnp.concatenate([gla_w_gk1[j, 0], gla_w_gk1[j, 1],
                                     jnp.zeros((d, n_extra - 2 * GLA_RANK), F32)], axis=1).astype(BF16)
            out = _project(xs, g_mix[i], mods, gla_w_proj[j].astype(BF16), seq_len=seq_len, w_extra=w_low,
                           pending=pending)
            if pending is not None:
                xs, out = out[0], out[1:]
            u, r_low = out
            w2pad = jnp.stack(
                [jnp.zeros((n_extra, GLA_DK), F32).at[k * GLA_RANK:(k + 1) * GLA_RANK].set(gla_w_gk2[j, k])
                 for k in range(2)]).astype(BF16)
            o_f, o_b = _gla_scan(u, r_low, w2pad, gla_b_gk[j])
            xs = _gla_out(xs, o_f, o_b, u, gla_g_norm[j], mods, gla_w_out[j].astype(BF16), seq_len=seq_len)
        elif kind == 2:
            rope = _rope_tables(seq_len, rows)
            out = _project(xs, g_mix[i], mods, swa_w_qkv[j].astype(BF16), seq_len=seq_len,
                           q_cols=SWA_HEADS * HEAD_DIM, q_scale=HEAD_DIM ** -0.5 * LOG2E, rope=rope,
                           rope_cols=(SWA_HEADS + SWA_KV_HEADS) * HEAD_DIM, pending=pending)
            xs, u = out if pending is not None else (xs, out)
            ypre = _swa_attention(u, swa_sinks[j], seq_len=seq_len)
            xs = _out_proj(xs, ypre, mods, swa_w_out[j].astype(BF16), seq_len=seq_len)
        else:
            out = _project(xs, g_mix[i], mods, na_w_qkv[j].astype(BF16), seq_len=seq_len,
                           q_cols=NA_HEADS * HEAD_DIM, q_scale=HEAD_DIM ** -0.5 * LOG2E, pending=pending)
            xs, u = out if pending is not None else (xs, out)
            ypre = _na_attention(u, _na_bias_table(na_rpb[j]), n_rows=seq_len // GRID_W)
            xs = _out_proj(xs, ypre, mods, na_w_out[j].astype(BF16), seq_len=seq_len)
        last = i == DEPTH - 1
        res = _moe(xs, g_ffn[i], mods, w_router_t, router_b, moe_w_gate, moe_w_up, moe_w_down, i,
                   n_x_tiles=n_x_tiles, g_final=g_final if last else None, defer=not last)
        xs, pending = (res, None) if last else (xs, res)
    return xs[None]
```

```python
import functools

import jax
import jax.numpy as jnp
from jax import lax
from jax.experimental import pallas as pl
from jax.experimental.pallas import tpu as pltpu

F32 = jnp.float32
BF16 = jnp.bfloat16
HIGHEST = lax.Precision.HIGHEST

D_MODEL = 1024
DEPTH = 4
GRID_W = 64
EPS = 1e-6
CONV_W = 3
GLA_HEADS = 4
GLA_DK = D_MODEL // 2
GLA_DV = D_MODEL
GLA_HK = GLA_DK // GLA_HEADS
GLA_HV = GLA_DV // GLA_HEADS
GLA_RANK = 16
GLA_NORMALIZER = 16.0
GLA_CHUNK = 64
HEAD_DIM = 64
SWA_HEADS = D_MODEL // HEAD_DIM
SWA_KV_HEADS = SWA_HEADS // 4
SWA_WINDOW = 128
ROPE_BASE = 10000.0
NA_HEADS = D_MODEL // HEAD_DIM
NA_KH = 8
NA_KW = 16
N_EXPERTS = 16
N_GROUPS = 4
EXPERTS_PER_GROUP = N_EXPERTS // N_GROUPS
D_EXPERT = D_MODEL // 2

TM = 256
MOE_BLOCK = 512
LANES = 128
SUBLANES = 8
CHUNKS = D_MODEL // LANES
NEG = -1e30
LOG2E = 1.4426950408889634
VMEM_LIMIT = 56 * 1024 * 1024

_NT = (((1,), (1,)), ((), ()))
_TN = (((0,), (0,)), ((), ()))


def _params(*sem):
    return pltpu.CompilerParams(dimension_semantics=sem, vmem_limit_bytes=VMEM_LIMIT)


def _norm_mod(x, g, scale, shift):
    ms = jnp.mean(x * x, axis=-1, keepdims=True)
    return (x * lax.rsqrt(ms + EPS) * g) * (1.0 + scale) + shift


def _silu(x):
    return x * jax.nn.sigmoid(x)


def _split2(x):
    hi = x.astype(BF16)
    return hi, (x - hi.astype(F32)).astype(BF16)


def _ada_kernel(cond_ref, w_ref, b_ref, o_ref):
    s = _silu(cond_ref[...])
    o_ref[0] = jnp.dot(s, w_ref[0], precision=HIGHEST, preferred_element_type=F32) + b_ref[0]


def _ada_params(c, c_ctx, w_mod, b_mod):
    depth, d, n = w_mod.shape
    nc = n // 4
    cond = jnp.zeros((8, d), F32).at[0].set(c[0]).at[1].set(c_ctx)
    out = pl.pallas_call(
        _ada_kernel,
        out_shape=jax.ShapeDtypeStruct((depth, 8, n), F32),
        grid=(depth, n // nc),
        in_specs=[pl.BlockSpec((8, d), lambda l, j: (0, 0)),
                  pl.BlockSpec((1, d, nc), lambda l, j: (l, 0, j)),
                  pl.BlockSpec((1, 1, nc), lambda l, j: (l, 0, j))],
        out_specs=pl.BlockSpec((1, 8, nc), lambda l, j: (l, 0, j)),
        compiler_params=_params("parallel", "parallel"),
        name="ada_params",
    )(cond, w_mod, b_mod.reshape(depth, 1, n))
    return out[:, :2].reshape(depth, 2, 6, d)


def _row_tile(rows, cap):
    return next(t for t in (1280, 640, TM) if t <= cap and rows % t == 0)


def _mod_rows(mod_ref, tb, seq_len):
    row = pl.program_id(0) * tb + lax.broadcasted_iota(jnp.int32, (tb, 1), 0)
    is_ctx = row >= seq_len
    return lambda j: jnp.where(is_ctx, mod_ref[1, j:j + 1, :], mod_ref[0, j:j + 1, :])


def _proj_kernel(*refs, tb, seq_len, n_rows, n_chunk, q_cols, q_scale, rope_cols, has_extra, fused):
    if fused:
        pos_ref, refs = refs[0], refs[1:]
    x_ref, g_ref, mod_ref, w_ref = refs[:4]
    rest = refs[4:]
    n_main = w_ref.shape[1]
    if has_extra:
        wx_ref, rest = rest[0], rest[1:]
    if rope_cols:
        row_ref, col_ref = rest[:2]
        rest = rest[2:]
        rpt = TM // GRID_W
        cos_t, sa_t, sb_t = [
            jnp.concatenate([jnp.broadcast_to(row_ref[a, k:k + 1, :], (GRID_W, LANES)) for a in range(rpt)],
                            axis=0) + col_ref[0, k]
            for k in range(3)]
    if fused:
        modp_ref, wc_ref, ys_ref = rest[:3]
        rest = rest[3:]
    o_ref = rest[0]
    if fused:
        xo_ref, buf, sem = rest[-3:]
        i = pl.program_id(0)
        nt = pl.num_programs(0)
        slot = i % 2
        slab_rows = TM * CHUNKS
        where = lambda s, k: (s * 2 + k) * slab_rows

        def start(t, base, s):
            for k in range(2):
                p = pl.multiple_of(pos_ref[k * n_rows + base + t], CHUNKS)
                dst = buf.at[pl.ds(pl.multiple_of(where(s, k) + t * CHUNKS, CHUNKS), CHUNKS)]
                pltpu.make_async_copy(ys_ref.at[pl.ds(p, CHUNKS)], dst, sem.at[s]).start(priority=k)

        def wait(s):
            for k in range(2):
                pltpu.make_async_copy(ys_ref.at[pl.ds(0, slab_rows)],
                                      buf.at[pl.ds(pl.multiple_of(where(s, k), slab_rows), slab_rows)],
                                      sem.at[s]).wait()

        @pl.when(i == 0)
        def _():
            def body(t, carry):
                start(t, 0, 0)
                return carry
            lax.fori_loop(0, TM, body, 0)

        wait(slot)
        wc = wc_ref[...]
        y = wc[:, 0:1] * _from_slabs(buf, where(slot, 0), TM) + wc[:, 1:2] * _from_slabs(buf, where(slot, 1), TM)
        x = x_ref[...] + _mod_rows(modp_ref, tb, seq_len)(5) * y
        xo_ref[...] = x
        nxt = jnp.where(i + 1 < nt, i + 1, 0) * TM
        for t in range(TM):
            start(t, nxt, 1 - slot)
    else:
        x = x_ref[...]
    pick = _mod_rows(mod_ref, tb, seq_len)
    h = _norm_mod(x, g_ref[...], pick(1), pick(0)).astype(BF16)
    for c0 in range(0, n_main, n_chunk):
        c1 = min(c0 + n_chunk, n_main)
        y = jnp.dot(h, w_ref[:, c0:c1], preferred_element_type=F32)
        if c0 < q_cols:
            y = y * q_scale
        if c0 >= rope_cols:
            o_ref[:, c0:c1] = y.astype(o_ref.dtype)
            continue
        for s0 in range(c0, c1, LANES):
            ys = y[:, s0 - c0:s0 - c0 + LANES]
            if s0 < rope_cols:
                ys = ys * cos_t + pltpu.roll(ys, LANES - 16, 1) * sa_t + pltpu.roll(ys, 16, 1) * sb_t
            o_ref[:, s0:s0 + LANES] = ys.astype(o_ref.dtype)
    if has_extra:
        rest[1][...] = jnp.dot(h, wx_ref[...], preferred_element_type=F32)
    if fused:
        @pl.when(i == nt - 1)
        def _():
            wait(1 - slot)


def _project(x, g, mods, w, *, seq_len, n_chunk=512, q_cols=0, q_scale=1.0, rope=None, rope_cols=0,
             w_extra=None, pending=None):
    r, d = x.shape
    n_main = w.shape[1]
    n_extra = 0 if w_extra is None else w_extra.shape[1]
    fused = pending is not None
    tb = TM if (rope_cols or fused) else _row_tile(r, 1280)
    im = lambda f: (lambda i, *_: f(i))
    in_specs = [pl.BlockSpec((tb, d), im(lambda i: (i, 0))),
                pl.BlockSpec((1, d), im(lambda i: (0, 0))),
                pl.BlockSpec((2, 6, d), im(lambda i: (0, 0, 0))),
                pl.BlockSpec((d, n_main), im(lambda i: (0, 0)))]
    args = [x, g.reshape(1, d), mods, w]
    if n_extra:
        in_specs.append(pl.BlockSpec((d, n_extra), im(lambda i: (0, 0))))
        args.append(w_extra)
    if rope_cols:
        rpt = TM // GRID_W
        in_specs += [pl.BlockSpec((rpt, 3, LANES), im(lambda i: (i, 0, 0))),
                     pl.BlockSpec((1, 3, TM, LANES), im(lambda i: (i // (seq_len // TM), 0, 0, 0)))]
        args += list(rope)
    out_shape = [jax.ShapeDtypeStruct((r, n_main), BF16)]
    out_specs = [pl.BlockSpec((tb, n_main), im(lambda i: (i, 0)))]
    if n_extra:
        out_shape.append(jax.ShapeDtypeStruct((r, n_extra), F32))
        out_specs.append(pl.BlockSpec((tb, n_extra), im(lambda i: (i, 0))))
    scratch = []
    if fused:
        pos, mods_prev, w_cols, ys = pending
        in_specs += [pl.BlockSpec((2, 6, d), im(lambda i: (0, 0, 0))),
                     pl.BlockSpec((tb, 2), im(lambda i: (i, 0))),
                     pl.BlockSpec(memory_space=pl.ANY)]
        args = [pos] + args + [mods_prev, w_cols, ys]
        out_shape.append(jax.ShapeDtypeStruct((r, d), F32))
        out_specs.append(pl.BlockSpec((tb, d), im(lambda i: (i, 0))))
        scratch = [pltpu.VMEM((2 * 2 * TM * CHUNKS, LANES), F32), pltpu.SemaphoreType.DMA((2,))]
    out = pl.pallas_call(
        functools.partial(_proj_kernel, tb=tb, seq_len=seq_len, n_rows=r, n_chunk=n_chunk, q_cols=q_cols,
                          q_scale=q_scale, rope_cols=rope_cols, has_extra=bool(n_extra), fused=fused),
        out_shape=out_shape,
        grid_spec=pltpu.PrefetchScalarGridSpec(
            num_scalar_prefetch=int(fused), grid=(r // tb,), in_specs=in_specs, out_specs=out_specs,
            scratch_shapes=scratch),
        compiler_params=_params("arbitrary" if fused else "parallel"),
        name="combine_norm_proj" if fused else "norm_proj",
    )(*args)
    if fused:
        return (out[-1],) + tuple(out[:-1])
    return out if n_extra else out[0]


def _out_proj_kernel(x_ref, y_ref, mod_ref, w_ref, o_ref, *, tb, seq_len):
    y = jnp.dot(y_ref[...], w_ref[...], preferred_element_type=F32)
    o_ref[...] = x_ref[...] + _mod_rows(mod_ref, tb, seq_len)(2) * y


def _out_proj(x, ypre, mods, w, *, seq_len):
    r, d = x.shape
    tb = _row_tile(r, 1280)
    return pl.pallas_call(
        functools.partial(_out_proj_kernel, tb=tb, seq_len=seq_len),
        out_shape=jax.ShapeDtypeStruct((r, d), F32),
        grid=(r // tb,),
        in_specs=[pl.BlockSpec((tb, d), lambda i: (i, 0)),
                  pl.BlockSpec((tb, d), lambda i: (i, 0)),
                  pl.BlockSpec((2, 6, d), lambda i: (0, 0, 0)),
                  pl.BlockSpec((d, d), lambda i: (0, 0))],
        out_specs=pl.BlockSpec((tb, d), lambda i: (i, 0)),
        compiler_params=_params("parallel"), name="out_proj",
    )(x, ypre, mods, w)


def _conv_kernel(x_ref, bg_ref, cg_ref, v_ref, cgp_ref, vp_ref, cgn_ref, vn_ref, cw_ref, mod_ref, w_ref,
                 o_ref, *, tb, seq_len, n_rows):
    z = cg_ref[...].astype(F32) * v_ref[...].astype(F32)
    z_prev = cgp_ref[SUBLANES - 1:SUBLANES, :].astype(F32) * vp_ref[SUBLANES - 1:SUBLANES, :].astype(F32)
    z_next = cgn_ref[0:1, :].astype(F32) * vn_ref[0:1, :].astype(F32)
    row = lax.broadcasted_iota(jnp.int32, z.shape, 0)
    pos = pl.program_id(0) * tb + row
    z_dn = jnp.where(row == 0, z_prev, pltpu.roll(z, 1, 0))
    z_up = jnp.where(row == tb - 1, z_next, pltpu.roll(z, tb - 1, 0))
    z_dn = jnp.where(jnp.logical_or(pos == 0, pos == seq_len), 0.0, z_dn)
    z_up = jnp.where(jnp.logical_or(pos == seq_len - 1, pos == n_rows - 1), 0.0, z_up)
    conv = z_dn * cw_ref[0:1, :] + z * cw_ref[1:2, :] + z_up * cw_ref[2:3, :]
    ypre = (bg_ref[...].astype(F32) * conv).astype(BF16)
    y = jnp.dot(ypre, w_ref[...], preferred_element_type=F32)
    o_ref[...] = x_ref[...] + _mod_rows(mod_ref, tb, seq_len)(2) * y


def _conv_mixer(x, u, conv_w, mods, w_out, *, seq_len):
    r, d = x.shape
    tb = _row_tile(r, 640)
    n8 = r // SUBLANES
    t8 = tb // SUBLANES
    tile = lambda c: pl.BlockSpec((tb, d), lambda i: (i, c))
    prev = lambda c: pl.BlockSpec((SUBLANES, d), lambda i: (jnp.maximum(i * t8 - 1, 0), c))
    nxt = lambda c: pl.BlockSpec((SUBLANES, d), lambda i: (jnp.minimum((i + 1) * t8, n8 - 1), c))
    return pl.pallas_call(
        functools.partial(_conv_kernel, tb=tb, seq_len=seq_len, n_rows=r),
        out_shape=jax.ShapeDtypeStruct((r, d), F32),
        grid=(r // tb,),
        in_specs=[pl.BlockSpec((tb, d), lambda i: (i, 0)), tile(0), tile(1), tile(2),
                  prev(1), prev(2), nxt(1), nxt(2),
                  pl.BlockSpec((CONV_W, d), lambda i: (0, 0)),
                  pl.BlockSpec((2, 6, d), lambda i: (0, 0, 0)),
                  pl.BlockSpec((d, d), lambda i: (0, 0))],
        out_specs=pl.BlockSpec((tb, d), lambda i: (i, 0)),
        compiler_params=_params("parallel"), name="conv_mixer",
    )(x, u, u, u, u, u, u, u, conv_w, mods, w_out)


def _log_sigmoid(z):
    return jnp.minimum(z, 0.0) - jnp.log1p(jnp.exp(-jnp.abs(z)))


def _gla_chunk(q_ref, k_ref, v_ref, gk, o_ref, st_ref, d, c, reverse):
    C = GLA_CHUNK
    rows = slice(c * C, (c + 1) * C)
    ii = lax.broadcasted_iota(jnp.int32, (C, C), 0)
    jj = lax.broadcasted_iota(jnp.int32, (C, C), 1)
    causal = (jj >= ii) if reverse else (jj <= ii)
    ones = jnp.where(causal, 1.0, 0.0).astype(BF16)
    g_hi, g_lo = _split2(gk[rows])
    bcum = (jnp.dot(ones, g_hi, preferred_element_type=F32)
            + jnp.dot(ones, g_lo, preferred_element_type=F32))
    total = bcum[0:1, :] if reverse else bcum[C - 1:C, :]
    q = q_ref[rows, :].astype(F32) * (GLA_HK ** -0.5)
    k = k_ref[rows, :].astype(F32)
    q_dec = (q * jnp.exp(bcum)).astype(BF16)
    k_inv = (k * jnp.exp(-bcum)).astype(BF16)
    k_end = (k * jnp.exp(total - bcum)).astype(BF16)
    decay = jnp.exp(total)
    for h in range(GLA_HEADS):
        kcols = slice(h * GLA_HK, (h + 1) * GLA_HK)
        vcols = slice(h * GLA_HV, (h + 1) * GLA_HV)
        vh = v_ref[rows, vcols]
        att = lax.dot_general(q_dec[:, kcols], k_inv[:, kcols], _NT, preferred_element_type=F32)
        att = jnp.where(causal, att, 0.0).astype(BF16)
        st = st_ref[d, h]
        o = (jnp.dot(att, vh, preferred_element_type=F32)
             + lax.dot_general(q_dec[:, kcols], st.astype(BF16), _NT, preferred_element_type=F32))
        st_ref[d, h] = st * decay[:, kcols] + lax.dot_general(vh, k_end[:, kcols], _TN,
                                                              preferred_element_type=F32)
        o_ref[rows, vcols] = o.astype(o_ref.dtype)


def _gla_kernel(qf_ref, kf_ref, vf_ref, rf_ref, qb_ref, kb_ref, vb_ref, rb_ref, w2_ref, b_ref,
                of_ref, ob_ref, st_ref):
    @pl.when(pl.program_id(0) == 0)
    def _():
        st_ref[...] = jnp.zeros_like(st_ref)

    def gates(r_ref, d):
        z = jnp.dot(r_ref[...].astype(BF16), w2_ref[d], preferred_element_type=F32) + b_ref[d]
        return _log_sigmoid(z) * (1.0 / GLA_NORMALIZER)

    gk_f, gk_b = gates(rf_ref, 0), gates(rb_ref, 1)
    n_chunks = TM // GLA_CHUNK
    for c in range(n_chunks):
        _gla_chunk(qf_ref, kf_ref, vf_ref, gk_f, of_ref, st_ref, 0, c, False)
        _gla_chunk(qb_ref, kb_ref, vb_ref, gk_b, ob_ref, st_ref, 1, n_chunks - 1 - c, True)


def _gla_scan(u, r_low, w2pad, b_gk):
    rws = u.shape[0]
    nt = rws // TM
    fwd = lambda s: (s + nt - 1) % nt
    bwd = lambda s: nt - 1 - s
    qkvr = lambda order: [pl.BlockSpec((TM, GLA_DK), lambda s: (order(s), 0)),
                          pl.BlockSpec((TM, GLA_DK), lambda s: (order(s), 1)),
                          pl.BlockSpec((TM, GLA_DV), lambda s: (order(s), 1)),
                          pl.BlockSpec((TM, LANES), lambda s: (order(s), 0))]
    out = jax.ShapeDtypeStruct((rws, GLA_DV), BF16)
    return pl.pallas_call(
        _gla_kernel,
        out_shape=[out, out],
        grid=(nt,),
        in_specs=qkvr(fwd) + qkvr(bwd) + [pl.BlockSpec((2, LANES, GLA_DK), lambda s: (0, 0, 0)),
                                          pl.BlockSpec((2, 1, GLA_DK), lambda s: (0, 0, 0))],
        out_specs=[pl.BlockSpec((TM, GLA_DV), lambda s: (fwd(s), 0)),
                   pl.BlockSpec((TM, GLA_DV), lambda s: (bwd(s), 0))],
        scratch_shapes=[pltpu.VMEM((2, GLA_HEADS, GLA_HV, GLA_HK), F32)],
        compiler_params=_params("arbitrary"), name="gla_scan",
    )(u, u, u, r_low, u, u, u, r_low, w2pad, b_gk.reshape(2, 1, GLA_DK))


def _gla_out_kernel(x_ref, of_ref, ob_ref, g_ref, gn_ref, mod_ref, w_ref, o_ref, *, tb, seq_len):
    o = of_ref[...].astype(F32) + ob_ref[...].astype(F32)
    parts = []
    for h in range(GLA_HEADS):
        oh = o[:, h * GLA_HV:(h + 1) * GLA_HV]
        parts.append(oh * lax.rsqrt(jnp.mean(oh * oh, axis=-1, keepdims=True) + EPS) * gn_ref[...])
    ypre = (jnp.concatenate(parts, axis=1) * _silu(g_ref[...].astype(F32))).astype(BF16)
    y = jnp.dot(ypre, w_ref[...], preferred_element_type=F32)
    o_ref[...] = x_ref[...] + _mod_rows(mod_ref, tb, seq_len)(2) * y


def _gla_out(x, o_f, o_b, u, g_norm, mods, w, *, seq_len):
    r, d = x.shape
    tb = _row_tile(r, 640)
    tile = lambda c: pl.BlockSpec((tb, d), lambda i: (i, c))
    return pl.pallas_call(
        functools.partial(_gla_out_kernel, tb=tb, seq_len=seq_len),
        out_shape=jax.ShapeDtypeStruct((r, d), F32),
        grid=(r // tb,),
        in_specs=[tile(0), tile(0), tile(0), tile(2),
                  pl.BlockSpec((1, GLA_HV), lambda i: (0, 0)),
                  pl.BlockSpec((2, 6, d), lambda i: (0, 0, 0)),
                  pl.BlockSpec((d, d), lambda i: (0, 0))],
        out_specs=tile(0),
        compiler_params=_params("parallel"), name="gla_out",
    )(x, o_f, o_b, u, g_norm.reshape(1, GLA_HV), mods, w)


def _swa_kernel(sink_ref, q_ref, kvp_ref, kvo_ref, kvn_ref, kvc_ref, o_ref, kv_buf, *, seq_len):
    i = pl.program_id(0)
    half = TM // 2
    kv_buf[0:half] = kvp_ref[...]
    kv_buf[half:half + TM] = kvo_ref[...]
    kv_buf[half + TM:2 * TM] = kvn_ref[...]
    kv_buf[2 * TM:3 * TM] = kvc_ref[...]
    nk = 3 * TM
    qpos = i * TM + lax.broadcasted_iota(jnp.int32, (TM, nk), 0)
    col = lax.broadcasted_iota(jnp.int32, (TM, nk), 1)
    kpos = i * TM - half + col
    local = ((jnp.abs(kpos - qpos) <= SWA_WINDOW) & (kpos >= 0) & (kpos < seq_len) & (qpos < seq_len))
    valid = local | (col >= 2 * TM)
    kv_w = SWA_KV_HEADS * HEAD_DIM
    group = SWA_HEADS // SWA_KV_HEADS
    for kv in range(SWA_KV_HEADS):
        heads = range(kv * group, (kv + 1) * group)
        qg = jnp.concatenate([q_ref[:, h * HEAD_DIM:(h + 1) * HEAD_DIM] for h in heads], axis=0)
        kh = kv_buf[:, kv * HEAD_DIM:(kv + 1) * HEAD_DIM]
        vh = kv_buf[:, kv_w + kv * HEAD_DIM:kv_w + (kv + 1) * HEAD_DIM]
        s = lax.dot_general(qg, kh, _NT, preferred_element_type=F32)
        ps, ls = [], []
        for j, h in enumerate(heads):
            sj = jnp.where(valid, s[j * TM:(j + 1) * TM], NEG)
            sink = sink_ref[h] * LOG2E
            m = jnp.maximum(jnp.max(sj, axis=-1, keepdims=True), sink)
            p = jnp.exp2(sj - m)
            ls.append(jnp.sum(p, axis=-1, keepdims=True) + jnp.exp2(sink - m))
            ps.append(p.astype(BF16))
        o = jnp.dot(jnp.concatenate(ps, axis=0), vh, preferred_element_type=F32)
        for j, h in enumerate(heads):
            o_ref[:, h * HEAD_DIM:(h + 1) * HEAD_DIM] = (o[j * TM:(j + 1) * TM] / ls[j]).astype(o_ref.dtype)


def _swa_attention(u, sinks, *, seq_len):
    rws = u.shape[0]
    nt = rws // TM
    half = TM // 2
    n_half = rws // half
    kv_w = 2 * SWA_KV_HEADS * HEAD_DIM
    qw = SWA_HEADS * HEAD_DIM
    kvc = qw // kv_w
    return pl.pallas_call(
        functools.partial(_swa_kernel, seq_len=seq_len),
        out_shape=jax.ShapeDtypeStruct((rws, qw), BF16),
        grid=(nt,),
        in_specs=[pl.BlockSpec(memory_space=pltpu.SMEM),
                  pl.BlockSpec((TM, qw), lambda i: (i, 0)),
                  pl.BlockSpec((half, kv_w), lambda i: (jnp.maximum(2 * i - 1, 0), kvc)),
                  pl.BlockSpec((TM, kv_w), lambda i: (i, kvc)),
                  pl.BlockSpec((half, kv_w), lambda i: (jnp.minimum(2 * i + 2, n_half - 1), kvc)),
                  pl.BlockSpec((TM, kv_w), lambda i: (nt - 1, kvc))],
        out_specs=pl.BlockSpec((TM, qw), lambda i: (i, 0)),
        scratch_shapes=[pltpu.VMEM((3 * TM, kv_w), BF16)],
        compiler_params=_params("parallel"), name="swa_attention",
    )(sinks, u, u, u, u, u)


def _na_kernel(q_ref, kp_ref, ko_ref, kn_ref, kc_ref, vp_ref, vo_ref, vn_ref, vc_ref, bias_ref, o_ref,
               k_buf, v_buf, *, n_rows):
    i = pl.program_id(0)
    for j, (kr, vr) in enumerate(((kp_ref, vp_ref), (ko_ref, vo_ref), (kn_ref, vn_ref), (kc_ref, vc_ref))):
        k_buf[j * TM:(j + 1) * TM] = kr[...]
        v_buf[j * TM:(j + 1) * TM] = vr[...]
    n_loc = 3 * TM
    rpt = TM // GRID_W
    qi = lax.broadcasted_iota(jnp.int32, (TM, n_loc), 0)
    ki = lax.broadcasted_iota(jnp.int32, (TM, n_loc), 1)
    r = i * rpt + qi // GRID_W
    c = qi % GRID_W
    krow = (i - 1) * rpt + ki // GRID_W
    kcol = ki % GRID_W
    r0 = jnp.clip(r - NA_KH // 2, 0, n_rows - NA_KH)
    c0 = jnp.clip(c - NA_KW // 2, 0, GRID_W - NA_KW)
    valid = ((krow >= r0) & (krow < r0 + NA_KH) & (kcol >= c0) & (kcol < c0 + NA_KW) & (r < n_rows))
    for h in range(NA_HEADS):
        cols = slice(h * HEAD_DIM, (h + 1) * HEAD_DIM)
        qh = q_ref[:, cols]
        s = lax.dot_general(qh, k_buf[:, cols], _NT, preferred_element_type=F32)
        s_loc = jnp.where(valid, s[:, :n_loc] + bias_ref[h], NEG)
        s_ctx = s[:, n_loc:]
        m = jnp.maximum(jnp.max(s_loc, axis=-1, keepdims=True), jnp.max(s_ctx, axis=-1, keepdims=True))
        p_loc = jnp.exp2(s_loc - m)
        p_ctx = jnp.exp2(s_ctx - m)
        l = jnp.sum(p_loc, axis=-1, keepdims=True) + jnp.sum(p_ctx, axis=-1, keepdims=True)
        o = (jnp.dot(p_loc.astype(BF16), v_buf[0:n_loc, cols], preferred_element_type=F32)
             + jnp.dot(p_ctx.astype(BF16), v_buf[n_loc:, cols], preferred_element_type=F32)) / l
        o_ref[:, cols] = o.astype(o_ref.dtype)


def _na_bias_kernel(t_ref, o_ref, *, lo):
    rpt = TM // GRID_W
    for a in range(rpt):
        for b in range(3 * rpt):
            o_ref[0, a * GRID_W:(a + 1) * GRID_W, b * GRID_W:(b + 1) * GRID_W] = t_ref[0, lo - a + b]


def _na_bias_table(rpb):
    n_heads = rpb.shape[0]
    rpt = TM // GRID_W
    lo = NA_KH - 1 - rpt
    assert lo - (rpt - 1) >= 0 and lo + 3 * rpt <= 2 * NA_KH - 1
    c = jnp.arange(GRID_W)
    col_off = c[None, :] - c[:, None] + NA_KW - 1
    onehot = (col_off[None] == jnp.arange(2 * NA_KW - 1)[:, None, None]).astype(F32)
    tcol = jnp.einsum('hrj,jck->hrck', rpb.astype(F32) * LOG2E, onehot, precision=HIGHEST)
    return pl.pallas_call(
        functools.partial(_na_bias_kernel, lo=lo),
        out_shape=jax.ShapeDtypeStruct((n_heads, TM, 3 * TM), F32),
        grid=(n_heads,),
        in_specs=[pl.BlockSpec((1, 2 * NA_KH - 1, GRID_W, GRID_W), lambda h: (h, 0, 0, 0))],
        out_specs=pl.BlockSpec((1, TM, 3 * TM), lambda h: (h, 0, 0)),
        compiler_params=_params("parallel"), name="na_bias_table",
    )(tcol)


def _na_attention(u, bias, *, n_rows):
    rws = u.shape[0]
    nt = rws // TM
    d = NA_HEADS * HEAD_DIM
    blk = lambda f, c: pl.BlockSpec((TM, d), lambda i: (f(i), c))
    prev = lambda i: jnp.maximum(i - 1, 0)
    own = lambda i: i
    nxt = lambda i: jnp.minimum(i + 1, nt - 1)
    ctx = lambda i: nt - 1
    return pl.pallas_call(
        functools.partial(_na_kernel, n_rows=n_rows),
        out_shape=jax.ShapeDtypeStruct((rws, d), BF16),
        grid=(nt,),
        in_specs=[blk(own, 0), blk(prev, 1), blk(own, 1), blk(nxt, 1), blk(ctx, 1),
                  blk(prev, 2), blk(own, 2), blk(nxt, 2), blk(ctx, 2),
                  pl.BlockSpec((NA_HEADS, TM, 3 * TM), lambda i: (0, 0, 0))],
        out_specs=pl.BlockSpec((TM, d), lambda i: (i, 0)),
        scratch_shapes=[pltpu.VMEM((4 * TM, d), BF16), pltpu.VMEM((4 * TM, d), BF16)],
        compiler_params=_params("parallel"), name="na_attention",
    )(u, u, u, u, u, u, u, u, u, bias)


def _router_kernel(x_ref, g_ref, mod_ref, wr_ref, br_ref, e_ref, w_ref, rank_ref, cnt_ref, carry, *,
                   tb, seq_len):
    @pl.when(pl.program_id(0) == 0)
    def _():
        carry[...] = jnp.zeros_like(carry)

    pick = _mod_rows(mod_ref, tb, seq_len)
    h = _norm_mod(x_ref[...], g_ref[...], pick(4), pick(3))
    h_hi, h_lo = _split2(h)
    lt = (jnp.dot(h_hi, wr_ref[0], preferred_element_type=F32) + jnp.dot(h_hi, wr_ref[1], preferred_element_type=F32)
          + jnp.dot(h_lo, wr_ref[0], preferred_element_type=F32))
    logits = jnp.transpose(lt)[:N_EXPERTS]
    scores = jax.nn.sigmoid(logits)
    biased = scores + br_ref[...]
    row = lambda a, e: a[e:e + 1, :]
    best_g = jnp.zeros((1, tb), jnp.int32)
    best_s = None
    for g in range(N_GROUPS):
        v = [row(biased, g * EXPERTS_PER_GROUP + j) for j in range(EXPERTS_PER_GROUP)]
        gs = None
        for a in range(EXPERTS_PER_GROUP):
            for b in range(a + 1, EXPERTS_PER_GROUP):
                pair = v[a] + v[b]
                gs = pair if gs is None else jnp.maximum(gs, pair)
        if best_s is None:
            best_s = gs
        else:
            better = gs > best_s
            best_g = jnp.where(better, g, best_g)
            best_s = jnp.where(better, gs, best_s)
    picks = []
    for _ in range(2):
        top_v = jnp.full((1, tb), -jnp.inf, F32)
        top_i = jnp.full((1, tb), -1, jnp.int32)
        for e in range(N_EXPERTS):
            ok = best_g == (e // EXPERTS_PER_GROUP)
            for p in picks:
                ok = jnp.logical_and(ok, p != e)
            cand = jnp.where(ok, row(biased, e), -jnp.inf)
            better = cand > top_v
            top_i = jnp.where(better, e, top_i)
            top_v = jnp.where(better, cand, top_v)
        picks.append(top_i)
    e_iota = lax.broadcasted_iota(jnp.int32, (N_EXPERTS, tb), 0)
    sel0 = e_iota == picks[0]
    sel1 = e_iota == picks[1]
    w0 = jnp.sum(jnp.where(sel0, scores, 0.0), axis=0, keepdims=True)
    w1 = jnp.sum(jnp.where(sel1, scores, 0.0), axis=0, keepdims=True)
    tot = w0 + w1
    e_ref[0:1, :] = picks[0]
    e_ref[1:2, :] = picks[1]
    w_ref[0:1, :] = w0 / tot
    w_ref[1:2, :] = w1 / tot
    member = jnp.where(jnp.logical_or(sel0, sel1), 1.0, 0.0)
    earlier = (lax.broadcasted_iota(jnp.int32, (TM, TM), 0)
               < lax.broadcasted_iota(jnp.int32, (TM, TM), 1)).astype(BF16)
    seen = carry[:, 0:1]
    before = []
    for c in range(tb // TM):
        part = member[:, c * TM:(c + 1) * TM]
        before.append(seen + jnp.dot(part.astype(BF16), earlier, preferred_element_type=F32))
        seen = seen + jnp.sum(part, axis=1, keepdims=True)
    before = jnp.concatenate(before, axis=1)
    rank_ref[0:1, :] = jnp.sum(jnp.where(sel0, before, 0.0), axis=0, keepdims=True).astype(jnp.int32)
    rank_ref[1:2, :] = jnp.sum(jnp.where(sel1, before, 0.0), axis=0, keepdims=True).astype(jnp.int32)
    carry[...] = jnp.broadcast_to(seen, carry.shape)
    cnt_ref[...] = carry[...].astype(jnp.int32)


def _router(x, g, mods, w_router_t, b_router, *, seq_len):
    r, d = x.shape
    tb = _row_tile(r, 1280)
    pair = pl.BlockSpec((2, tb), lambda i: (0, i))
    return pl.pallas_call(
        functools.partial(_router_kernel, tb=tb, seq_len=seq_len),
        out_shape=[jax.ShapeDtypeStruct((2, r), jnp.int32), jax.ShapeDtypeStruct((2, r), F32),
                   jax.ShapeDtypeStruct((2, r), jnp.int32), jax.ShapeDtypeStruct((N_EXPERTS, LANES), jnp.int32)],
        grid=(r // tb,),
        in_specs=[pl.BlockSpec((tb, d), lambda i: (i, 0)),
                  pl.BlockSpec((1, d), lambda i: (0, 0)),
                  pl.BlockSpec((2, 6, d), lambda i: (0, 0, 0)),
                  pl.BlockSpec((2, d, LANES), lambda i: (0, 0, 0)),
                  pl.BlockSpec((N_EXPERTS, 1), lambda i: (0, 0))],
        out_specs=[pair, pair, pair, pl.BlockSpec((N_EXPERTS, LANES), lambda i: (0, 0))],
        scratch_shapes=[pltpu.VMEM((N_EXPERTS, LANES), F32)],
        compiler_params=_params("arbitrary"), name="router",
    )(x, g.reshape(1, d), mods, w_router_t, b_router.reshape(N_EXPERTS, 1))


def _dispatch_plan(e01, rank01, counts, n_blocks):
    cnt = counts[:, 0]
    nblk = (cnt + MOE_BLOCK - 1) // MOE_BLOCK
    blk_end = jnp.cumsum(nblk)
    off = (blk_end - nblk) * MOE_BLOCK
    eidx = jnp.arange(N_EXPERTS, dtype=jnp.int32)[:, None, None]
    pos = rank01 + jnp.sum(jnp.where(e01[None] == eidx, off[:, None, None], 0), axis=0)
    blk_e = jnp.sum(jnp.arange(n_blocks, dtype=jnp.int32)[:, None] >= blk_end[None, :], axis=1)
    blk_e = jnp.minimum(blk_e, N_EXPERTS - 1).astype(jnp.int32)
    n_used = blk_end[-1:]
    fill = jnp.concatenate([off + cnt, nblk * MOE_BLOCK - cnt, n_used]).astype(jnp.int32)
    slab_row = (pos * CHUNKS).reshape(-1).astype(jnp.int32)
    return slab_row, blk_e, n_used.astype(jnp.int32), fill


def _zero_fill(fill_ref, zbuf, xs_ref, zsem, n_blocks, wait):
    def copy(start_row, n_rows):
        cp = pltpu.make_async_copy(zbuf.at[pl.ds(0, n_rows * CHUNKS)],
                                   xs_ref.at[pl.ds(pl.multiple_of(start_row * CHUNKS, CHUNKS), n_rows * CHUNKS)],
                                   zsem)
        cp.wait() if wait else cp.start()

    for e in range(N_EXPERTS):
        start, length = fill_ref[e], fill_ref[N_EXPERTS + e]
        piece = MOE_BLOCK // 2
        while piece >= 1:
            @pl.when((length & piece) != 0)
            def _():
                copy(start + (length & (-2 * piece)), piece)
            piece //= 2

    def body(j, carry):
        for part in range(MOE_BLOCK // TM):
            copy(j * MOE_BLOCK + part * TM, TM)
        return carry

    lax.fori_loop(fill_ref[2 * N_EXPERTS], n_blocks, body, 0)


def _to_slabs(ref, base, x):
    n = x.shape[0]
    for c in range(CHUNKS):
        ref[pl.ds(base + c, n, stride=CHUNKS), :] = x[:, c * LANES:(c + 1) * LANES]


def _from_slabs(ref, base, n):
    return jnp.concatenate([ref[pl.ds(base + c, n, stride=CHUNKS), :] for c in range(CHUNKS)], axis=1)


def _dispatch_kernel(pos_ref, fill_ref, x_ref, g_ref, mod_ref, xs_ref, h_scr, zbuf, sem, zsem, *,
                     rows, nt, n_blocks):
    i = pl.program_id(0)
    slot = i % 2
    slab_rows = TM * CHUNKS

    @pl.when(i == 0)
    def _():
        zbuf[...] = jnp.zeros_like(zbuf)
        _zero_fill(fill_ref, zbuf, xs_ref, zsem, n_blocks, wait=False)

    def wait_slot(s):
        for _ in range(2):
            pltpu.make_async_copy(h_scr.at[pl.ds(pl.multiple_of(s * slab_rows, slab_rows), slab_rows)],
                                  xs_ref.at[pl.ds(0, slab_rows)], sem.at[s]).wait()

    @pl.when(i >= 2)
    def _():
        wait_slot(slot)

    h = _norm_mod(x_ref[...], g_ref[...], mod_ref[0, 4:5, :], mod_ref[0, 3:4, :])
    _to_slabs(h_scr, slot * slab_rows, h)
    base = i * TM

    def body(grp, carry):
        for u in range(SUBLANES):
            t = grp * SUBLANES + u
            src = h_scr.at[pl.ds(pl.multiple_of(slot * slab_rows + t * CHUNKS, CHUNKS), CHUNKS)]
            for k in range(2):
                p = pl.multiple_of(pos_ref[k * rows + base + t], CHUNKS)
                pltpu.make_async_copy(src, xs_ref.at[pl.ds(p, CHUNKS)], sem.at[slot]).start(priority=k)
        return carry

    lax.fori_loop(0, TM // SUBLANES, body, 0)

    @pl.when(i == nt - 1)
    def _():
        wait_slot(slot)
        if nt > 1:
            wait_slot(1 - slot)
        _zero_fill(fill_ref, zbuf, xs_ref, zsem, n_blocks, wait=True)


def _dispatch(pos, fill, x, g, mods, n_blocks, *, n_x_tiles):
    r, d = x.shape
    nt = r // TM
    assert MOE_BLOCK % TM == 0 and MOE_BLOCK // 2 <= TM and d == CHUNKS * LANES
    n_rows = n_blocks * MOE_BLOCK
    return pl.pallas_call(
        functools.partial(_dispatch_kernel, rows=r, nt=nt, n_blocks=n_blocks),
        out_shape=jax.ShapeDtypeStruct((n_rows * CHUNKS, LANES), F32),
        grid_spec=pltpu.PrefetchScalarGridSpec(
            num_scalar_prefetch=2, grid=(nt,),
            in_specs=[pl.BlockSpec((TM, d), lambda i, p, f: (i, 0)),
                      pl.BlockSpec((1, d), lambda i, p, f: (0, 0)),
                      pl.BlockSpec((1, 6, d), lambda i, p, f: (i // n_x_tiles, 0, 0))],
            out_specs=pl.BlockSpec(memory_space=pl.ANY),
            scratch_shapes=[pltpu.VMEM((2 * TM * CHUNKS, LANES), F32),
                            pltpu.VMEM((TM * CHUNKS, LANES), F32),
                            pltpu.SemaphoreType.DMA((2,)), pltpu.SemaphoreType.DMA]),
        compiler_params=_params("arbitrary"), name="moe_dispatch",
    )(pos, fill, x, g.reshape(1, d), mods)


def _expert_kernel(be_ref, nu_ref, xs_ref, wg_ref, wu_ref, wd_ref, ys_ref, wg_s, wu_s, wd_s):
    j = pl.program_id(0)

    @pl.when(j < nu_ref[0])
    def _():
        @pl.when(jnp.logical_or(j == 0, be_ref[j] != be_ref[jnp.maximum(j - 1, 0)]))
        def _():
            wg_s[...] = wg_ref[0, 0].astype(BF16)
            wu_s[...] = wu_ref[0, 0].astype(BF16)
            wd_s[...] = wd_ref[0, 0].astype(BF16)

        h = _from_slabs(xs_ref, 0, MOE_BLOCK).astype(BF16)
        a = (_silu(jnp.dot(h, wg_s[...], preferred_element_type=F32))
             * jnp.dot(h, wu_s[...], preferred_element_type=F32))
        _to_slabs(ys_ref, 0, jnp.dot(a.astype(BF16), wd_s[...], preferred_element_type=F32))

    @pl.when(j >= nu_ref[0])
    def _():
        ys_ref[...] = jnp.zeros_like(ys_ref)


def _experts(blk_e, n_used, xs, wg, wu, wd, layer):
    d, de = wg.shape[-2:]
    blk = MOE_BLOCK * CHUNKS
    last = lambda j, nu: jnp.minimum(j, jnp.maximum(nu[0] - 1, 0))
    return pl.pallas_call(
        _expert_kernel,
        out_shape=jax.ShapeDtypeStruct(xs.shape, F32),
        grid_spec=pltpu.PrefetchScalarGridSpec(
            num_scalar_prefetch=2, grid=(xs.shape[0] // blk,),
            in_specs=[pl.BlockSpec((blk, LANES), lambda j, be, nu: (last(j, nu), 0)),
                      pl.BlockSpec((1, 1, d, de), lambda j, be, nu: (layer, be[last(j, nu)], 0, 0)),
                      pl.BlockSpec((1, 1, d, de), lambda j, be, nu: (layer, be[last(j, nu)], 0, 0)),
                      pl.BlockSpec((1, 1, de, d), lambda j, be, nu: (layer, be[last(j, nu)], 0, 0))],
            out_specs=pl.BlockSpec((blk, LANES), lambda j, be, nu: (j, 0)),
            scratch_shapes=[pltpu.VMEM((d, de), BF16), pltpu.VMEM((d, de), BF16), pltpu.VMEM((de, d), BF16)]),
        compiler_params=_params("arbitrary"), name="moe_experts",
    )(blk_e, n_used, xs, wg, wu, wd)


def _combine_kernel(pos_ref, x_ref, mod_ref, w_ref, ys_ref, *rest, rows, nt, final):
    if final:
        gf_ref, o_ref, buf, sem = rest
    else:
        o_ref, buf, sem = rest
    i = pl.program_id(0)
    slot = i % 2
    slab_rows = TM * CHUNKS
    where = lambda s, k: (s * 2 + k) * slab_rows

    def issue(row_tile, s):
        base = row_tile * TM

        def body(grp, carry):
            for u in range(SUBLANES):
                t = grp * SUBLANES + u
                for k in range(2):
                    p = pl.multiple_of(pos_ref[k * rows + base + t], CHUNKS)
                    dst = buf.at[pl.ds(pl.multiple_of(where(s, k) + t * CHUNKS, CHUNKS), CHUNKS)]
                    pltpu.make_async_copy(ys_ref.at[pl.ds(p, CHUNKS)], dst, sem.at[s]).start(priority=k)
            return carry

        lax.fori_loop(0, TM // SUBLANES, body, 0)

    @pl.when(i == 0)
    def _():
        issue(0, 0)

    @pl.when(i + 1 < nt)
    def _():
        issue(i + 1, 1 - slot)

    for k in range(2):
        pltpu.make_async_copy(ys_ref.at[pl.ds(0, slab_rows)],
                              buf.at[pl.ds(pl.multiple_of(where(slot, k), slab_rows), slab_rows)],
                              sem.at[slot]).wait()
    w = w_ref[...]
    y = w[:, 0:1] * _from_slabs(buf, where(slot, 0), TM) + w[:, 1:2] * _from_slabs(buf, where(slot, 1), TM)
    x = x_ref[...] + mod_ref[0, 5:6, :] * y
    if final:
        x = x * lax.rsqrt(jnp.mean(x * x, axis=-1, keepdims=True) + EPS) * gf_ref[...]
    o_ref[...] = x


def _combine(pos, x, mods, w_cols, ys, *, n_x_tiles, g_final=None):
    r, d = x.shape
    final = g_final is not None
    nt = n_x_tiles if final else r // TM
    in_specs = [pl.BlockSpec((TM, d), lambda i, p: (i, 0)),
                pl.BlockSpec((1, 6, d), lambda i, p: (i // n_x_tiles, 0, 0)),
                pl.BlockSpec((TM, 2), lambda i, p: (i, 0)),
                pl.BlockSpec(memory_space=pl.ANY)]
    args = [pos, x, mods, w_cols, ys]
    if final:
        in_specs.append(pl.BlockSpec((1, d), lambda i, p: (0, 0)))
        args.append(g_final.reshape(1, d))
    return pl.pallas_call(
        functools.partial(_combine_kernel, rows=r, nt=nt, final=final),
        out_shape=jax.ShapeDtypeStruct((nt * TM, d), F32),
        grid_spec=pltpu.PrefetchScalarGridSpec(
            num_scalar_prefetch=1, grid=(nt,), in_specs=in_specs,
            out_specs=pl.BlockSpec((TM, d), lambda i, p: (i, 0)),
            scratch_shapes=[pltpu.VMEM((2 * 2 * TM * CHUNKS, LANES), F32), pltpu.SemaphoreType.DMA((2,))]),
        compiler_params=_params("arbitrary"), name="moe_combine",
    )(*args)


def _moe(x, g, mods, w_router_t, b_router, wg, wu, wd, layer, *, n_x_tiles, g_final=None, defer=False):
    r = x.shape[0]
    n_blocks = -(-2 * r // MOE_BLOCK) + N_EXPERTS
    e01, w01, rank01, counts = _router(x, g, mods, w_router_t, b_router, seq_len=n_x_tiles * TM)
    pos, blk_e, n_used, fill = _dispatch_plan(e01, rank01, counts, n_blocks)
    xs = _dispatch(pos, fill, x, g, mods, n_blocks, n_x_tiles=n_x_tiles)
    ys = _experts(blk_e, n_used, xs, wg, wu, wd, layer)
    if defer:
        return pos, mods, w01.T, ys
    return _combine(pos, x, mods, w01.T, ys, n_x_tiles=n_x_tiles, g_final=g_final)


def _rope_tables(seq_len, n_rows_total):
    half = HEAD_DIM // 2
    inv_freq = ROPE_BASE ** (-jnp.arange(0, half, 2, dtype=F32) / half)
    first = jnp.arange(half) < (half // 2)

    def tab(n):
        ang = jnp.arange(n, dtype=F32)[:, None] * inv_freq
        ang = jnp.concatenate([ang, ang], axis=-1)
        c, s = jnp.cos(ang), jnp.sin(ang)
        return jnp.stack([c, jnp.where(first, -s, 0.0), jnp.where(first, 0.0, s)], axis=1)

    ident = jnp.stack([jnp.ones((half,), F32), jnp.zeros((half,), F32), jnp.zeros((half,), F32)])
    reps = LANES // HEAD_DIM
    n_ctx_rows = (n_rows_total - seq_len) // GRID_W
    row = jnp.concatenate([tab(seq_len // GRID_W), jnp.broadcast_to(ident, (n_ctx_rows, 3, half))], axis=0)
    row = jnp.tile(jnp.concatenate([row, jnp.zeros_like(row)], axis=-1), (1, 1, reps))
    col = jnp.stack([tab(GRID_W), jnp.broadcast_to(ident, (GRID_W, 3, half))])
    col = jnp.tile(jnp.concatenate([jnp.zeros_like(col), col], axis=-1), (1, TM // GRID_W, 1, reps))
    return row, col.transpose(0, 2, 1, 3)


def kernel(x, c, ctx, c_ctx, w_mod, b_mod, g_mix, g_ffn, g_final, conv_w_in, conv_w, conv_w_out,
           gla_w_proj, gla_w_gk1, gla_w_gk2, gla_b_gk, gla_g_norm, gla_w_out, swa_w_qkv, swa_sinks,
           swa_w_out, na_w_qkv, na_rpb, na_w_out, router_w, router_b, moe_w_gate, moe_w_up, moe_w_down):
    seq_len, d = x.shape[1], x.shape[2]
    ctx_len = ctx.shape[1]
    assert x.shape[0] == 1 and ctx_len == TM and seq_len % TM == 0 and d == D_MODEL
    assert seq_len % GRID_W == 0 and seq_len // GRID_W >= NA_KH
    n_x_tiles = seq_len // TM
    rows = seq_len + ctx_len
    xs = jnp.concatenate([x[0], ctx[0]], axis=0)
    mods_all = _ada_params(c, c_ctx, w_mod, b_mod)
    w_router = jnp.pad(router_w.astype(F32), ((0, 0), (0, LANES - N_EXPERTS)))
    w_router_t = jnp.stack(_split2(w_router))
    n_mixers = 4

    pending = None
    for i in range(DEPTH):
        kind, j = i % n_mixers, i // n_mixers
        mods = mods_all[i]
        if kind == 0:
            if pending is not None:
                xs = _combine(pending[0], xs, *pending[1:], n_x_tiles=n_x_tiles)
            u = _project(xs, g_mix[i], mods, conv_w_in[j].astype(BF16), seq_len=seq_len)
            xs = _conv_mixer(xs, u, conv_w[j], mods, conv_w_out[j].astype(BF16), seq_len=seq_len)
        elif kind == 1:
            n_extra = LANES
            w_low = jnp.concatenate([gla_w_gk1[j, 0], gla_w_gk1[j, 1],
                                     jnp.zeros((d, n_extra - 2 * GLA_RANK), F32)], axis=1).astype(BF16)
            out = _project(xs, g_mix[i], mods, gla_w_proj[j].astype(BF16), seq_len=seq_len, w_extra=w_low,
                           pending=pending)
            if pending is not None:
                xs, out = out[0], out[1:]
            u, r_low = out
            w2pad = jnp.stack(
                [jnp.zeros((n_extra, GLA_DK), F32).at[k * GLA_RANK:(k + 1) * GLA_RANK].set(gla_w_gk2[j, k])
                 for k in range(2)]).astype(BF16)
            o_f, o_b = _gla_scan(u, r_low, w2pad, gla_b_gk[j])
            xs = _gla_out(xs, o_f, o_b, u, gla_g_norm[j], mods, gla_w_out[j].astype(BF16), seq_len=seq_len)
        elif kind == 2:
            if pending is not None:
                xs = _combine(pending[0], xs, *pending[1:], n_x_tiles=n_x_tiles)
            rope = _rope_tables(seq_len, rows)
            u = _project(xs, g_mix[i], mods, swa_w_qkv[j].astype(BF16), seq_len=seq_len,
                         q_cols=SWA_HEADS * HEAD_DIM, q_scale=HEAD_DIM ** -0.5 * LOG2E, rope=rope,
                         rope_cols=(SWA_HEADS + SWA_KV_HEADS) * HEAD_DIM)
            ypre = _swa_attention(u, swa_sinks[j], seq_len=seq_len)
            xs = _out_proj(xs, ypre, mods, swa_w_out[j].astype(BF16), seq_len=seq_len)
        else:
            out = _project(xs, g_mix[i], mods, na_w_qkv[j].astype(BF16), seq_len=seq_len,
                           q_cols=NA_HEADS * HEAD_DIM, q_scale=HEAD_DIM ** -0.5 * LOG2E, pending=pending)
            xs, u = out if pending is not None else (xs, out)
            ypre = _na_attention(u, _na_bias_table(na_rpb[j]), n_rows=seq_len // GRID_W)
            xs = _out_proj(xs, ypre, mods, na_w_out[j].astype(BF16), seq_len=seq_len)
        last = i == DEPTH - 1
        res = _moe(xs, g_ffn[i], mods, w_router_t, router_b, moe_w_gate, moe_w_up, moe_w_down, i,
                   n_x_tiles=n_x_tiles, g_final=g_final if last else None, defer=not last)
        xs, pending = (res, None) if last else (xs, res)
    return xs[None]
```

```python
import functools

import jax
import jax.numpy as jnp
from jax import lax
from jax.experimental import pallas as pl
from jax.experimental.pallas import tpu as pltpu

F32 = jnp.float32
BF16 = jnp.bfloat16
HIGHEST = lax.Precision.HIGHEST

D_MODEL = 1024
DEPTH = 4
GRID_W = 64
EPS = 1e-6
CONV_W = 3
GLA_HEADS = 4
GLA_DK = D_MODEL // 2
GLA_DV = D_MODEL
GLA_HK = GLA_DK // GLA_HEADS
GLA_HV = GLA_DV // GLA_HEADS
GLA_RANK = 16
GLA_NORMALIZER = 16.0
GLA_CHUNK = 64
HEAD_DIM = 64
SWA_HEADS = D_MODEL // HEAD_DIM
SWA_KV_HEADS = SWA_HEADS // 4
SWA_WINDOW = 128
ROPE_BASE = 10000.0
NA_HEADS = D_MODEL // HEAD_DIM
NA_KH = 8
NA_KW = 16
N_EXPERTS = 16
N_GROUPS = 4
EXPERTS_PER_GROUP = N_EXPERTS // N_GROUPS
D_EXPERT = D_MODEL // 2

TM = 256
MOE_BLOCK = 512
LANES = 128
SUBLANES = 8
CHUNKS = D_MODEL // LANES
NEG = -1e30
LOG2E = 1.4426950408889634
VMEM_LIMIT = 56 * 1024 * 1024

_NT = (((1,), (1,)), ((), ()))
_TN = (((0,), (0,)), ((), ()))


def _params(*sem):
    return pltpu.CompilerParams(dimension_semantics=sem, vmem_limit_bytes=VMEM_LIMIT)


def _norm_mod(x, g, scale, shift):
    ms = jnp.mean(x * x, axis=-1, keepdims=True)
    return (x * lax.rsqrt(ms + EPS) * g) * (1.0 + scale) + shift


def _silu(x):
    return x * jax.nn.sigmoid(x)


def _split2(x):
    hi = x.astype(BF16)
    return hi, (x - hi.astype(F32)).astype(BF16)


def _ada_kernel(cond_ref, w_ref, b_ref, o_ref):
    s = _silu(cond_ref[...])
    o_ref[0] = jnp.dot(s, w_ref[0], precision=HIGHEST, preferred_element_type=F32) + b_ref[0]


def _ada_params(c, c_ctx, w_mod, b_mod):
    depth, d, n = w_mod.shape
    nc = n // 4
    cond = jnp.zeros((8, d), F32).at[0].set(c[0]).at[1].set(c_ctx)
    out = pl.pallas_call(
        _ada_kernel,
        out_shape=jax.ShapeDtypeStruct((depth, 8, n), F32),
        grid=(depth, n // nc),
        in_specs=[pl.BlockSpec((8, d), lambda l, j: (0, 0)),
                  pl.BlockSpec((1, d, nc), lambda l, j: (l, 0, j)),
                  pl.BlockSpec((1, 1, nc), lambda l, j: (l, 0, j))],
        out_specs=pl.BlockSpec((1, 8, nc), lambda l, j: (l, 0, j)),
        compiler_params=_params("parallel", "parallel"),
        name="ada_params",
    )(cond, w_mod, b_mod.reshape(depth, 1, n))
    return out[:, :2].reshape(depth, 2, 6, d)


def _row_tile(rows, cap):
    return next(t for t in (1280, 640, TM) if t <= cap and rows % t == 0)


def _mod_rows(mod_ref, tb, seq_len):
    row = pl.program_id(0) * tb + lax.broadcasted_iota(jnp.int32, (tb, 1), 0)
    is_ctx = row >= seq_len
    return lambda j: jnp.where(is_ctx, mod_ref[1, j:j + 1, :], mod_ref[0, j:j + 1, :])


def _proj_kernel(*refs, tb, seq_len, n_rows, n_chunk, q_cols, q_scale, rope_cols, has_extra, fused):
    if fused:
        pos_ref, refs = refs[0], refs[1:]
    x_ref, g_ref, mod_ref, w_ref = refs[:4]
    rest = refs[4:]
    n_main = w_ref.shape[1]
    if has_extra:
        wx_ref, rest = rest[0], rest[1:]
    if rope_cols:
        row_ref, col_ref = rest[:2]
        rest = rest[2:]
        rpt = TM // GRID_W
        cos_t, sa_t, sb_t = [
            jnp.concatenate([jnp.broadcast_to(row_ref[a, k:k + 1, :], (GRID_W, LANES)) for a in range(rpt)],
                            axis=0) + col_ref[0, k]
            for k in range(3)]
    if fused:
        modp_ref, wc_ref, ys_ref = rest[:3]
        rest = rest[3:]
    o_ref = rest[0]
    if fused:
        xo_ref, buf, sem = rest[-3:]
        i = pl.program_id(0)
        nt = pl.num_programs(0)
        slot = i % 2
        slab_rows = TM * CHUNKS
        where = lambda s, k: (s * 2 + k) * slab_rows

        def start(t, base, s):
            for k in range(2):
                p = pl.multiple_of(pos_ref[k * n_rows + base + t], CHUNKS)
                dst = buf.at[pl.ds(pl.multiple_of(where(s, k) + t * CHUNKS, CHUNKS), CHUNKS)]
                pltpu.make_async_copy(ys_ref.at[pl.ds(p, CHUNKS)], dst, sem.at[s]).start(priority=k)

        def wait(s):
            for k in range(2):
                pltpu.make_async_copy(ys_ref.at[pl.ds(0, slab_rows)],
                                      buf.at[pl.ds(pl.multiple_of(where(s, k), slab_rows), slab_rows)],
                                      sem.at[s]).wait()

        @pl.when(i == 0)
        def _():
            def body(t, carry):
                start(t, 0, 0)
                return carry
            lax.fori_loop(0, TM, body, 0)

        wait(slot)
        wc = wc_ref[...]
        y = wc[:, 0:1] * _from_slabs(buf, where(slot, 0), TM) + wc[:, 1:2] * _from_slabs(buf, where(slot, 1), TM)
        x = x_ref[...] + _mod_rows(modp_ref, tb, seq_len)(5) * y
        xo_ref[...] = x
        nxt = jnp.where(i + 1 < nt, i + 1, 0) * TM
        for t in range(TM):
            start(t, nxt, 1 - slot)
    else:
        x = x_ref[...]
    pick = _mod_rows(mod_ref, tb, seq_len)
    h = _norm_mod(x, g_ref[...], pick(1), pick(0)).astype(BF16)
    for c0 in range(0, n_main, n_chunk):
        c1 = min(c0 + n_chunk, n_main)
        y = jnp.dot(h, w_ref[:, c0:c1], preferred_element_type=F32)
        if c0 < q_cols:
            y = y * q_scale
        if c0 >= rope_cols:
            o_ref[:, c0:c1] = y.astype(o_ref.dtype)
            continue
        for s0 in range(c0, c1, LANES):
            ys = y[:, s0 - c0:s0 - c0 + LANES]
            if s0 < rope_cols:
                ys = ys * cos_t + pltpu.roll(ys, LANES - 16, 1) * sa_t + pltpu.roll(ys, 16, 1) * sb_t
            o_ref[:, s0:s0 + LANES] = ys.astype(o_ref.dtype)
    if has_extra:
        rest[1][...] = jnp.dot(h, wx_ref[...], preferred_element_type=F32)
    if fused:
        @pl.when(i == nt - 1)
        def _():
            wait(1 - slot)


def _project(x, g, mods, w, *, seq_len, n_chunk=512, q_cols=0, q_scale=1.0, rope=None, rope_cols=0,
             w_extra=None, pending=None):
    r, d = x.shape
    n_main = w.shape[1]
    n_extra = 0 if w_extra is None else w_extra.shape[1]
    fused = pending is not None
    tb = TM if (rope_cols or fused) else _row_tile(r, 1280)
    im = lambda f: (lambda i, *_: f(i))
    in_specs = [pl.BlockSpec((tb, d), im(lambda i: (i, 0))),
                pl.BlockSpec((1, d), im(lambda i: (0, 0))),
                pl.BlockSpec((2, 6, d), im(lambda i: (0, 0, 0))),
                pl.BlockSpec((d, n_main), im(lambda i: (0, 0)))]
    args = [x, g.reshape(1, d), mods, w]
    if n_extra:
        in_specs.append(pl.BlockSpec((d, n_extra), im(lambda i: (0, 0))))
        args.append(w_extra)
    if rope_cols:
        rpt = TM // GRID_W
        in_specs += [pl.BlockSpec((rpt, 3, LANES), im(lambda i: (i, 0, 0))),
                     pl.BlockSpec((1, 3, TM, LANES), im(lambda i: (i // (seq_len // TM), 0, 0, 0)))]
        args += list(rope)
    out_shape = [jax.ShapeDtypeStruct((r, n_main), BF16)]
    out_specs = [pl.BlockSpec((tb, n_main), im(lambda i: (i, 0)))]
    if n_extra:
        out_shape.append(jax.ShapeDtypeStruct((r, n_extra), F32))
        out_specs.append(pl.BlockSpec((tb, n_extra), im(lambda i: (i, 0))))
    scratch = []
    if fused:
        pos, mods_prev, w_cols, ys = pending
        in_specs += [pl.BlockSpec((2, 6, d), im(lambda i: (0, 0, 0))),
                     pl.BlockSpec((tb, 2), im(lambda i: (i, 0))),
                     pl.BlockSpec(memory_space=pl.ANY)]
        args = [pos] + args + [mods_prev, w_cols, ys]
        out_shape.append(jax.ShapeDtypeStruct((r, d), F32))
        out_specs.append(pl.BlockSpec((tb, d), im(lambda i: (i, 0))))
        scratch = [pltpu.VMEM((2 * 2 * TM * CHUNKS, LANES), F32), pltpu.SemaphoreType.DMA((2,))]
    out = pl.pallas_call(
        functools.partial(_proj_kernel, tb=tb, seq_len=seq_len, n_rows=r, n_chunk=n_chunk, q_cols=q_cols,
                          q_scale=q_scale, rope_cols=rope_cols, has_extra=bool(n_extra), fused=fused),
        out_shape=out_shape,
        grid_spec=pltpu.PrefetchScalarGridSpec(
            num_scalar_prefetch=int(fused), grid=(r // tb,), in_specs=in_specs, out_specs=out_specs,
            scratch_shapes=scratch),
        compiler_params=_params("arbitrary" if fused else "parallel"),
        name="combine_norm_proj" if fused else "norm_proj",
    )(*args)
    if fused:
        return (out[-1],) + tuple(out[:-1])
    return out if n_extra else out[0]


def _out_proj_kernel(x_ref, y_ref, mod_ref, w_ref, *route, tb, seq_len):
    o_ref, route = route[3], route[:3] + route[4:]
    y = jnp.dot(y_ref[...], w_ref[...], preferred_element_type=F32)
    x = x_ref[...] + _mod_rows(mod_ref, tb, seq_len)(2) * y
    o_ref[...] = x
    _route_tile(x, mod_ref, *route, tb=tb, seq_len=seq_len)


def _out_proj(x, ypre, mods, w, route, *, seq_len):
    r, d = x.shape
    tb = _row_tile(r, 1280)
    r_args, r_in, r_shape, r_out, r_scratch = _route_operands(r, d, tb, *route)
    out = pl.pallas_call(
        functools.partial(_out_proj_kernel, tb=tb, seq_len=seq_len),
        out_shape=[jax.ShapeDtypeStruct((r, d), F32)] + r_shape,
        grid=(r // tb,),
        in_specs=[pl.BlockSpec((tb, d), lambda i: (i, 0)),
                  pl.BlockSpec((tb, d), lambda i: (i, 0)),
                  pl.BlockSpec((2, 6, d), lambda i: (0, 0, 0)),
                  pl.BlockSpec((d, d), lambda i: (0, 0))] + r_in,
        out_specs=[pl.BlockSpec((tb, d), lambda i: (i, 0))] + r_out,
        scratch_shapes=r_scratch,
        compiler_params=_params("arbitrary"), name="out_proj_route",
    )(x, ypre, mods, w, *r_args)
    return out[0], tuple(out[1:])


def _conv_kernel(x_ref, bg_ref, cg_ref, v_ref, cgp_ref, vp_ref, cgn_ref, vn_ref, cw_ref, mod_ref, w_ref,
                 *route, tb, seq_len, n_rows):
    o_ref, route = route[3], route[:3] + route[4:]
    z = cg_ref[...].astype(F32) * v_ref[...].astype(F32)
    z_prev = cgp_ref[SUBLANES - 1:SUBLANES, :].astype(F32) * vp_ref[SUBLANES - 1:SUBLANES, :].astype(F32)
    z_next = cgn_ref[0:1, :].astype(F32) * vn_ref[0:1, :].astype(F32)
    row = lax.broadcasted_iota(jnp.int32, z.shape, 0)
    pos = pl.program_id(0) * tb + row
    z_dn = jnp.where(row == 0, z_prev, pltpu.roll(z, 1, 0))
    z_up = jnp.where(row == tb - 1, z_next, pltpu.roll(z, tb - 1, 0))
    z_dn = jnp.where(jnp.logical_or(pos == 0, pos == seq_len), 0.0, z_dn)
    z_up = jnp.where(jnp.logical_or(pos == seq_len - 1, pos == n_rows - 1), 0.0, z_up)
    conv = z_dn * cw_ref[0:1, :] + z * cw_ref[1:2, :] + z_up * cw_ref[2:3, :]
    ypre = (bg_ref[...].astype(F32) * conv).astype(BF16)
    y = jnp.dot(ypre, w_ref[...], preferred_element_type=F32)
    x = x_ref[...] + _mod_rows(mod_ref, tb, seq_len)(2) * y
    o_ref[...] = x
    _route_tile(x, mod_ref, *route, tb=tb, seq_len=seq_len)


def _conv_mixer(x, u, conv_w, mods, w_out, route, *, seq_len):
    r, d = x.shape
    tb = _row_tile(r, 640)
    r_args, r_in, r_shape, r_out, r_scratch = _route_operands(r, d, tb, *route)
    n8 = r // SUBLANES
    t8 = tb // SUBLANES
    tile = lambda c: pl.BlockSpec((tb, d), lambda i: (i, c))
    prev = lambda c: pl.BlockSpec((SUBLANES, d), lambda i: (jnp.maximum(i * t8 - 1, 0), c))
    nxt = lambda c: pl.BlockSpec((SUBLANES, d), lambda i: (jnp.minimum((i + 1) * t8, n8 - 1), c))
    out = pl.pallas_call(
        functools.partial(_conv_kernel, tb=tb, seq_len=seq_len, n_rows=r),
        out_shape=[jax.ShapeDtypeStruct((r, d), F32)] + r_shape,
        grid=(r // tb,),
        in_specs=[pl.BlockSpec((tb, d), lambda i: (i, 0)), tile(0), tile(1), tile(2),
                  prev(1), prev(2), nxt(1), nxt(2),
                  pl.BlockSpec((CONV_W, d), lambda i: (0, 0)),
                  pl.BlockSpec((2, 6, d), lambda i: (0, 0, 0)),
                  pl.BlockSpec((d, d), lambda i: (0, 0))] + r_in,
        out_specs=[pl.BlockSpec((tb, d), lambda i: (i, 0))] + r_out,
        scratch_shapes=r_scratch,
        compiler_params=_params("arbitrary"), name="conv_mixer_route",
    )(x, u, u, u, u, u, u, u, conv_w, mods, w_out, *r_args)
    return out[0], tuple(out[1:])


def _log_sigmoid(z):
    return jnp.minimum(z, 0.0) - jnp.log1p(jnp.exp(-jnp.abs(z)))


def _gla_chunk(q_ref, k_ref, v_ref, gk, o_ref, st_ref, d, c, reverse):
    C = GLA_CHUNK
    rows = slice(c * C, (c + 1) * C)
    ii = lax.broadcasted_iota(jnp.int32, (C, C), 0)
    jj = lax.broadcasted_iota(jnp.int32, (C, C), 1)
    causal = (jj >= ii) if reverse else (jj <= ii)
    ones = jnp.where(causal, 1.0, 0.0).astype(BF16)
    g_hi, g_lo = _split2(gk[rows])
    bcum = (jnp.dot(ones, g_hi, preferred_element_type=F32)
            + jnp.dot(ones, g_lo, preferred_element_type=F32))
    total = bcum[0:1, :] if reverse else bcum[C - 1:C, :]
    q = q_ref[rows, :].astype(F32) * (GLA_HK ** -0.5)
    k = k_ref[rows, :].astype(F32)
    q_dec = (q * jnp.exp(bcum)).astype(BF16)
    k_inv = (k * jnp.exp(-bcum)).astype(BF16)
    k_end = (k * jnp.exp(total - bcum)).astype(BF16)
    decay = jnp.exp(total)
    for h in range(GLA_HEADS):
        kcols = slice(h * GLA_HK, (h + 1) * GLA_HK)
        vcols = slice(h * GLA_HV, (h + 1) * GLA_HV)
        vh = v_ref[rows, vcols]
        att = lax.dot_general(q_dec[:, kcols], k_inv[:, kcols], _NT, preferred_element_type=F32)
        att = jnp.where(causal, att, 0.0).astype(BF16)
        st = st_ref[d, h]
        o = (jnp.dot(att, vh, preferred_element_type=F32)
             + lax.dot_general(q_dec[:, kcols], st.astype(BF16), _NT, preferred_element_type=F32))
        st_ref[d, h] = st * decay[:, kcols] + lax.dot_general(vh, k_end[:, kcols], _TN,
                                                              preferred_element_type=F32)
        o_ref[rows, vcols] = o.astype(o_ref.dtype)


def _gla_kernel(qf_ref, kf_ref, vf_ref, rf_ref, qb_ref, kb_ref, vb_ref, rb_ref, w2_ref, b_ref,
                of_ref, ob_ref, st_ref):
    @pl.when(pl.program_id(0) == 0)
    def _():
        st_ref[...] = jnp.zeros_like(st_ref)

    def gates(r_ref, d):
        z = jnp.dot(r_ref[...].astype(BF16), w2_ref[d], preferred_element_type=F32) + b_ref[d]
        return _log_sigmoid(z) * (1.0 / GLA_NORMALIZER)

    gk_f, gk_b = gates(rf_ref, 0), gates(rb_ref, 1)
    n_chunks = TM // GLA_CHUNK
    for c in range(n_chunks):
        _gla_chunk(qf_ref, kf_ref, vf_ref, gk_f, of_ref, st_ref, 0, c, False)
        _gla_chunk(qb_ref, kb_ref, vb_ref, gk_b, ob_ref, st_ref, 1, n_chunks - 1 - c, True)


def _gla_scan(u, r_low, w2pad, b_gk):
    rws = u.shape[0]
    nt = rws // TM
    fwd = lambda s: (s + nt - 1) % nt
    bwd = lambda s: nt - 1 - s
    qkvr = lambda order: [pl.BlockSpec((TM, GLA_DK), lambda s: (order(s), 0)),
                          pl.BlockSpec((TM, GLA_DK), lambda s: (order(s), 1)),
                          pl.BlockSpec((TM, GLA_DV), lambda s: (order(s), 1)),
                          pl.BlockSpec((TM, LANES), lambda s: (order(s), 0))]
    out = jax.ShapeDtypeStruct((rws, GLA_DV), BF16)
    return pl.pallas_call(
        _gla_kernel,
        out_shape=[out, out],
        grid=(nt,),
        in_specs=qkvr(fwd) + qkvr(bwd) + [pl.BlockSpec((2, LANES, GLA_DK), lambda s: (0, 0, 0)),
                                          pl.BlockSpec((2, 1, GLA_DK), lambda s: (0, 0, 0))],
        out_specs=[pl.BlockSpec((TM, GLA_DV), lambda s: (fwd(s), 0)),
                   pl.BlockSpec((TM, GLA_DV), lambda s: (bwd(s), 0))],
        scratch_shapes=[pltpu.VMEM((2, GLA_HEADS, GLA_HV, GLA_HK), F32)],
        compiler_params=_params("arbitrary"), name="gla_scan",
    )(u, u, u, r_low, u, u, u, r_low, w2pad, b_gk.reshape(2, 1, GLA_DK))


def _gla_out_kernel(x_ref, of_ref, ob_ref, g_ref, gn_ref, mod_ref, w_ref, *route, tb, seq_len):
    o_ref, route = route[3], route[:3] + route[4:]
    o = of_ref[...].astype(F32) + ob_ref[...].astype(F32)
    parts = []
    for h in range(GLA_HEADS):
        oh = o[:, h * GLA_HV:(h + 1) * GLA_HV]
        parts.append(oh * lax.rsqrt(jnp.mean(oh * oh, axis=-1, keepdims=True) + EPS) * gn_ref[...])
    ypre = (jnp.concatenate(parts, axis=1) * _silu(g_ref[...].astype(F32))).astype(BF16)
    y = jnp.dot(ypre, w_ref[...], preferred_element_type=F32)
    x = x_ref[...] + _mod_rows(mod_ref, tb, seq_len)(2) * y
    o_ref[...] = x
    _route_tile(x, mod_ref, *route, tb=tb, seq_len=seq_len)


def _gla_out(x, o_f, o_b, u, g_norm, mods, w, route, *, seq_len):
    r, d = x.shape
    tb = _row_tile(r, 640)
    tile = lambda c: pl.BlockSpec((tb, d), lambda i: (i, c))
    r_args, r_in, r_shape, r_out, r_scratch = _route_operands(r, d, tb, *route)
    out = pl.pallas_call(
        functools.partial(_gla_out_kernel, tb=tb, seq_len=seq_len),
        out_shape=[jax.ShapeDtypeStruct((r, d), F32)] + r_shape,
        grid=(r // tb,),
        in_specs=[tile(0), tile(0), tile(0), tile(2),
                  pl.BlockSpec((1, GLA_HV), lambda i: (0, 0)),
                  pl.BlockSpec((2, 6, d), lambda i: (0, 0, 0)),
                  pl.BlockSpec((d, d), lambda i: (0, 0))] + r_in,
        out_specs=[tile(0)] + r_out,
        scratch_shapes=r_scratch,
        compiler_params=_params("arbitrary"), name="gla_out_route",
    )(x, o_f, o_b, u, g_norm.reshape(1, GLA_HV), mods, w, *r_args)
    return out[0], tuple(out[1:])


def _swa_kernel(sink_ref, q_ref, kvp_ref, kvo_ref, kvn_ref, kvc_ref, o_ref, kv_buf, *, seq_len):
    i = pl.program_id(0)
    half = TM // 2
    kv_buf[0:half] = kvp_ref[...]
    kv_buf[half:half + TM] = kvo_ref[...]
    kv_buf[half + TM:2 * TM] = kvn_ref[...]
    kv_buf[2 * TM:3 * TM] = kvc_ref[...]
    nk = 3 * TM
    qpos = i * TM + lax.broadcasted_iota(jnp.int32, (TM, nk), 0)
    col = lax.broadcasted_iota(jnp.int32, (TM, nk), 1)
    kpos = i * TM - half + col
    local = ((jnp.abs(kpos - qpos) <= SWA_WINDOW) & (kpos >= 0) & (kpos < seq_len) & (qpos < seq_len))
    valid = local | (col >= 2 * TM)
    kv_w = SWA_KV_HEADS * HEAD_DIM
    group = SWA_HEADS // SWA_KV_HEADS
    for kv in range(SWA_KV_HEADS):
        heads = range(kv * group, (kv + 1) * group)
        qg = jnp.concatenate([q_ref[:, h * HEAD_DIM:(h + 1) * HEAD_DIM] for h in heads], axis=0)
        kh = kv_buf[:, kv * HEAD_DIM:(kv + 1) * HEAD_DIM]
        vh = kv_buf[:, kv_w + kv * HEAD_DIM:kv_w + (kv + 1) * HEAD_DIM]
        s = lax.dot_general(qg, kh, _NT, preferred_element_type=F32)
        ps, ls = [], []
        for j, h in enumerate(heads):
            sj = jnp.where(valid, s[j * TM:(j + 1) * TM], NEG)
            sink = sink_ref[h] * LOG2E
            m = jnp.maximum(jnp.max(sj, axis=-1, keepdims=True), sink)
            p = jnp.exp2(sj - m)
            ls.append(jnp.sum(p, axis=-1, keepdims=True) + jnp.exp2(sink - m))
            ps.append(p.astype(BF16))
        o = jnp.dot(jnp.concatenate(ps, axis=0), vh, preferred_element_type=F32)
        for j, h in enumerate(heads):
            o_ref[:, h * HEAD_DIM:(h + 1) * HEAD_DIM] = (o[j * TM:(j + 1) * TM] / ls[j]).astype(o_ref.dtype)


def _swa_attention(u, sinks, *, seq_len):
    rws = u.shape[0]
    nt = rws // TM
    half = TM // 2
    n_half = rws // half
    kv_w = 2 * SWA_KV_HEADS * HEAD_DIM
    qw = SWA_HEADS * HEAD_DIM
    kvc = qw // kv_w
    return pl.pallas_call(
        functools.partial(_swa_kernel, seq_len=seq_len),
        out_shape=jax.ShapeDtypeStruct((rws, qw), BF16),
        grid=(nt,),
        in_specs=[pl.BlockSpec(memory_space=pltpu.SMEM),
                  pl.BlockSpec((TM, qw), lambda i: (i, 0)),
                  pl.BlockSpec((half, kv_w), lambda i: (jnp.maximum(2 * i - 1, 0), kvc)),
                  pl.BlockSpec((TM, kv_w), lambda i: (i, kvc)),
                  pl.BlockSpec((half, kv_w), lambda i: (jnp.minimum(2 * i + 2, n_half - 1), kvc)),
                  pl.BlockSpec((TM, kv_w), lambda i: (nt - 1, kvc))],
        out_specs=pl.BlockSpec((TM, qw), lambda i: (i, 0)),
        scratch_shapes=[pltpu.VMEM((3 * TM, kv_w), BF16)],
        compiler_params=_params("parallel"), name="swa_attention",
    )(sinks, u, u, u, u, u)


def _na_kernel(q_ref, kp_ref, ko_ref, kn_ref, kc_ref, vp_ref, vo_ref, vn_ref, vc_ref, bias_ref, o_ref,
               k_buf, v_buf, *, n_rows):
    i = pl.program_id(0)
    for j, (kr, vr) in enumerate(((kp_ref, vp_ref), (ko_ref, vo_ref), (kn_ref, vn_ref), (kc_ref, vc_ref))):
        k_buf[j * TM:(j + 1) * TM] = kr[...]
        v_buf[j * TM:(j + 1) * TM] = vr[...]
    n_loc = 3 * TM
    rpt = TM // GRID_W
    qi = lax.broadcasted_iota(jnp.int32, (TM, n_loc), 0)
    ki = lax.broadcasted_iota(jnp.int32, (TM, n_loc), 1)
    r = i * rpt + qi // GRID_W
    c = qi % GRID_W
    krow = (i - 1) * rpt + ki // GRID_W
    kcol = ki % GRID_W
    r0 = jnp.clip(r - NA_KH // 2, 0, n_rows - NA_KH)
    c0 = jnp.clip(c - NA_KW // 2, 0, GRID_W - NA_KW)
    valid = ((krow >= r0) & (krow < r0 + NA_KH) & (kcol >= c0) & (kcol < c0 + NA_KW) & (r < n_rows))
    for h in range(NA_HEADS):
        cols = slice(h * HEAD_DIM, (h + 1) * HEAD_DIM)
        qh = q_ref[:, cols]
        s = lax.dot_general(qh, k_buf[:, cols], _NT, preferred_element_type=F32)
        s_loc = jnp.where(valid, s[:, :n_loc] + bias_ref[h], NEG)
        s_ctx = s[:, n_loc:]
        m = jnp.maximum(jnp.max(s_loc, axis=-1, keepdims=True), jnp.max(s_ctx, axis=-1, keepdims=True))
        p_loc = jnp.exp2(s_loc - m)
        p_ctx = jnp.exp2(s_ctx - m)
        l = jnp.sum(p_loc, axis=-1, keepdims=True) + jnp.sum(p_ctx, axis=-1, keepdims=True)
        o = (jnp.dot(p_loc.astype(BF16), v_buf[0:n_loc, cols], preferred_element_type=F32)
             + jnp.dot(p_ctx.astype(BF16), v_buf[n_loc:, cols], preferred_element_type=F32)) / l
        o_ref[:, cols] = o.astype(o_ref.dtype)


def _na_bias_kernel(t_ref, o_ref, *, lo):
    rpt = TM // GRID_W
    for a in range(rpt):
        for b in range(3 * rpt):
            o_ref[0, a * GRID_W:(a + 1) * GRID_W, b * GRID_W:(b + 1) * GRID_W] = t_ref[0, lo - a + b]


def _na_bias_table(rpb):
    n_heads = rpb.shape[0]
    rpt = TM // GRID_W
    lo = NA_KH - 1 - rpt
    assert lo - (rpt - 1) >= 0 and lo + 3 * rpt <= 2 * NA_KH - 1
    c = jnp.arange(GRID_W)
    col_off = c[None, :] - c[:, None] + NA_KW - 1
    onehot = (col_off[None] == jnp.arange(2 * NA_KW - 1)[:, None, None]).astype(F32)
    tcol = jnp.einsum('hrj,jck->hrck', rpb.astype(F32) * LOG2E, onehot, precision=HIGHEST)
    return pl.pallas_call(
        functools.partial(_na_bias_kernel, lo=lo),
        out_shape=jax.ShapeDtypeStruct((n_heads, TM, 3 * TM), F32),
        grid=(n_heads,),
        in_specs=[pl.BlockSpec((1, 2 * NA_KH - 1, GRID_W, GRID_W), lambda h: (h, 0, 0, 0))],
        out_specs=pl.BlockSpec((1, TM, 3 * TM), lambda h: (h, 0, 0)),
        compiler_params=_params("parallel"), name="na_bias_table",
    )(tcol)


def _na_attention(u, bias, *, n_rows):
    rws = u.shape[0]
    nt = rws // TM
    d = NA_HEADS * HEAD_DIM
    blk = lambda f, c: pl.BlockSpec((TM, d), lambda i: (f(i), c))
    prev = lambda i: jnp.maximum(i - 1, 0)
    own = lambda i: i
    nxt = lambda i: jnp.minimum(i + 1, nt - 1)
    ctx = lambda i: nt - 1
    return pl.pallas_call(
        functools.partial(_na_kernel, n_rows=n_rows),
        out_shape=jax.ShapeDtypeStruct((rws, d), BF16),
        grid=(nt,),
        in_specs=[blk(own, 0), blk(prev, 1), blk(own, 1), blk(nxt, 1), blk(ctx, 1),
                  blk(prev, 2), blk(own, 2), blk(nxt, 2), blk(ctx, 2),
                  pl.BlockSpec((NA_HEADS, TM, 3 * TM), lambda i: (0, 0, 0))],
        out_specs=pl.BlockSpec((TM, d), lambda i: (i, 0)),
        scratch_shapes=[pltpu.VMEM((4 * TM, d), BF16), pltpu.VMEM((4 * TM, d), BF16)],
        compiler_params=_params("parallel"), name="na_attention",
    )(u, u, u, u, u, u, u, u, u, bias)


def _route_tile(x, mod_ref, g_ref, wr_ref, br_ref, e_ref, w_ref, rank_ref, cnt_ref, carry, *, tb, seq_len):
    @pl.when(pl.program_id(0) == 0)
    def _():
        carry[...] = jnp.zeros_like(carry)

    pick = _mod_rows(mod_ref, tb, seq_len)
    h = _norm_mod(x, g_ref[...], pick(4), pick(3))
    h_hi, h_lo = _split2(h)
    lt = (jnp.dot(h_hi, wr_ref[0], preferred_element_type=F32) + jnp.dot(h_hi, wr_ref[1], preferred_element_type=F32)
          + jnp.dot(h_lo, wr_ref[0], preferred_element_type=F32))
    logits = jnp.transpose(lt)[:N_EXPERTS]
    scores = jax.nn.sigmoid(logits)
    biased = scores + br_ref[...]
    row = lambda a, e: a[e:e + 1, :]
    best_g = jnp.zeros((1, tb), jnp.int32)
    best_s = None
    for g in range(N_GROUPS):
        v = [row(biased, g * EXPERTS_PER_GROUP + j) for j in range(EXPERTS_PER_GROUP)]
        gs = None
        for a in range(EXPERTS_PER_GROUP):
            for b in range(a + 1, EXPERTS_PER_GROUP):
                pair = v[a] + v[b]
                gs = pair if gs is None else jnp.maximum(gs, pair)
        if best_s is None:
            best_s = gs
        else:
            better = gs > best_s
            best_g = jnp.where(better, g, best_g)
            best_s = jnp.where(better, gs, best_s)
    picks = []
    for _ in range(2):
        top_v = jnp.full((1, tb), -jnp.inf, F32)
        top_i = jnp.full((1, tb), -1, jnp.int32)
        for e in range(N_EXPERTS):
            ok = best_g == (e // EXPERTS_PER_GROUP)
            for p in picks:
                ok = jnp.logical_and(ok, p != e)
            cand = jnp.where(ok, row(biased, e), -jnp.inf)
            better = cand > top_v
            top_i = jnp.where(better, e, top_i)
            top_v = jnp.where(better, cand, top_v)
        picks.append(top_i)
    e_iota = lax.broadcasted_iota(jnp.int32, (N_EXPERTS, tb), 0)
    sel0 = e_iota == picks[0]
    sel1 = e_iota == picks[1]
    w0 = jnp.sum(jnp.where(sel0, scores, 0.0), axis=0, keepdims=True)
    w1 = jnp.sum(jnp.where(sel1, scores, 0.0), axis=0, keepdims=True)
    tot = w0 + w1
    e_ref[0:1, :] = picks[0]
    e_ref[1:2, :] = picks[1]
    w_ref[0:1, :] = w0 / tot
    w_ref[1:2, :] = w1 / tot
    member = jnp.where(jnp.logical_or(sel0, sel1), 1.0, 0.0)
    ck = TM if tb % TM == 0 else LANES
    earlier = (lax.broadcasted_iota(jnp.int32, (ck, ck), 0)
               < lax.broadcasted_iota(jnp.int32, (ck, ck), 1)).astype(BF16)
    seen = carry[:, 0:1]
    before = []
    for c in range(tb // ck):
        part = member[:, c * ck:(c + 1) * ck]
        before.append(seen + jnp.dot(part.astype(BF16), earlier, preferred_element_type=F32))
        seen = seen + jnp.sum(part, axis=1, keepdims=True)
    before = jnp.concatenate(before, axis=1)
    rank_ref[0:1, :] = jnp.sum(jnp.where(sel0, before, 0.0), axis=0, keepdims=True).astype(jnp.int32)
    rank_ref[1:2, :] = jnp.sum(jnp.where(sel1, before, 0.0), axis=0, keepdims=True).astype(jnp.int32)
    carry[...] = jnp.broadcast_to(seen, carry.shape)
    cnt_ref[...] = carry[...].astype(jnp.int32)


def _route_operands(r, d, tb, g_ffn, w_router, b_router):
    pair = pl.BlockSpec((2, tb), lambda i: (0, i))
    return ([g_ffn.reshape(1, d), w_router, b_router.reshape(N_EXPERTS, 1)],
            [pl.BlockSpec((1, d), lambda i: (0, 0)),
             pl.BlockSpec((2, d, LANES), lambda i: (0, 0, 0)),
             pl.BlockSpec((N_EXPERTS, 1), lambda i: (0, 0))],
            [jax.ShapeDtypeStruct((2, r), jnp.int32), jax.ShapeDtypeStruct((2, r), F32),
             jax.ShapeDtypeStruct((2, r), jnp.int32), jax.ShapeDtypeStruct((N_EXPERTS, LANES), jnp.int32)],
            [pair, pair, pair, pl.BlockSpec((N_EXPERTS, LANES), lambda i: (0, 0))],
            [pltpu.VMEM((N_EXPERTS, LANES), F32)])


def _dispatch_plan(e01, rank01, counts, n_blocks):
    cnt = counts[:, 0]
    nblk = (cnt + MOE_BLOCK - 1) // MOE_BLOCK
    blk_end = jnp.cumsum(nblk)
    off = (blk_end - nblk) * MOE_BLOCK
    eidx = jnp.arange(N_EXPERTS, dtype=jnp.int32)[:, None, None]
    pos = rank01 + jnp.sum(jnp.where(e01[None] == eidx, off[:, None, None], 0), axis=0)
    blk_e = jnp.sum(jnp.arange(n_blocks, dtype=jnp.int32)[:, None] >= blk_end[None, :], axis=1)
    blk_e = jnp.minimum(blk_e, N_EXPERTS - 1).astype(jnp.int32)
    n_used = blk_end[-1:]
    fill = jnp.concatenate([off + cnt, nblk * MOE_BLOCK - cnt, n_used]).astype(jnp.int32)
    slab_row = (pos * CHUNKS).reshape(-1).astype(jnp.int32)
    return slab_row, blk_e, n_used.astype(jnp.int32), fill


def _zero_fill(fill_ref, zbuf, xs_ref, zsem, n_blocks, wait):
    def copy(start_row, n_rows):
        cp = pltpu.make_async_copy(zbuf.at[pl.ds(0, n_rows * CHUNKS)],
                                   xs_ref.at[pl.ds(pl.multiple_of(start_row * CHUNKS, CHUNKS), n_rows * CHUNKS)],
                                   zsem)
        cp.wait() if wait else cp.start()

    for e in range(N_EXPERTS):
        start, length = fill_ref[e], fill_ref[N_EXPERTS + e]
        piece = MOE_BLOCK // 2
        while piece >= 1:
            @pl.when((length & piece) != 0)
            def _():
                copy(start + (length & (-2 * piece)), piece)
            piece //= 2

    def body(j, carry):
        for part in range(MOE_BLOCK // TM):
            copy(j * MOE_BLOCK + part * TM, TM)
        return carry

    lax.fori_loop(fill_ref[2 * N_EXPERTS], n_blocks, body, 0)


def _to_slabs(ref, base, x):
    n = x.shape[0]
    for c in range(CHUNKS):
        ref[pl.ds(base + c, n, stride=CHUNKS), :] = x[:, c * LANES:(c + 1) * LANES]


def _from_slabs(ref, base, n):
    return jnp.concatenate([ref[pl.ds(base + c, n, stride=CHUNKS), :] for c in range(CHUNKS)], axis=1)


def _dispatch_kernel(pos_ref, fill_ref, x_ref, g_ref, mod_ref, xs_ref, h_scr, zbuf, sem, zsem, *,
                     rows, nt, n_blocks):
    i = pl.program_id(0)
    slot = i % 2
    slab_rows = TM * CHUNKS

    @pl.when(i == 0)
    def _():
        zbuf[...] = jnp.zeros_like(zbuf)
        _zero_fill(fill_ref, zbuf, xs_ref, zsem, n_blocks, wait=False)

    def wait_slot(s):
        for _ in range(2):
            pltpu.make_async_copy(h_scr.at[pl.ds(pl.multiple_of(s * slab_rows, slab_rows), slab_rows)],
                                  xs_ref.at[pl.ds(0, slab_rows)], sem.at[s]).wait()

    @pl.when(i >= 2)
    def _():
        wait_slot(slot)

    h = _norm_mod(x_ref[...], g_ref[...], mod_ref[0, 4:5, :], mod_ref[0, 3:4, :])
    _to_slabs(h_scr, slot * slab_rows, h)
    base = i * TM

    def body(grp, carry):
        for u in range(SUBLANES):
            t = grp * SUBLANES + u
            src = h_scr.at[pl.ds(pl.multiple_of(slot * slab_rows + t * CHUNKS, CHUNKS), CHUNKS)]
            for k in range(2):
                p = pl.multiple_of(pos_ref[k * rows + base + t], CHUNKS)
                pltpu.make_async_copy(src, xs_ref.at[pl.ds(p, CHUNKS)], sem.at[slot]).start(priority=k)
        return carry

    lax.fori_loop(0, TM // SUBLANES, body, 0)

    @pl.when(i == nt - 1)
    def _():
        wait_slot(slot)
        if nt > 1:
            wait_slot(1 - slot)
        _zero_fill(fill_ref, zbuf, xs_ref, zsem, n_blocks, wait=True)


def _dispatch(pos, fill, x, g, mods, n_blocks, *, n_x_tiles):
    r, d = x.shape
    nt = r // TM
    assert MOE_BLOCK % TM == 0 and MOE_BLOCK // 2 <= TM and d == CHUNKS * LANES
    n_rows = n_blocks * MOE_BLOCK
    return pl.pallas_call(
        functools.partial(_dispatch_kernel, rows=r, nt=nt, n_blocks=n_blocks),
        out_shape=jax.ShapeDtypeStruct((n_rows * CHUNKS, LANES), F32),
        grid_spec=pltpu.PrefetchScalarGridSpec(
            num_scalar_prefetch=2, grid=(nt,),
            in_specs=[pl.BlockSpec((TM, d), lambda i, p, f: (i, 0)),
                      pl.BlockSpec((1, d), lambda i, p, f: (0, 0)),
                      pl.BlockSpec((1, 6, d), lambda i, p, f: (i // n_x_tiles, 0, 0))],
            out_specs=pl.BlockSpec(memory_space=pl.ANY),
            scratch_shapes=[pltpu.VMEM((2 * TM * CHUNKS, LANES), F32),
                            pltpu.VMEM((TM * CHUNKS, LANES), F32),
                            pltpu.SemaphoreType.DMA((2,)), pltpu.SemaphoreType.DMA]),
        compiler_params=_params("arbitrary"), name="moe_dispatch",
    )(pos, fill, x, g.reshape(1, d), mods)


def _expert_kernel(be_ref, nu_ref, xs_ref, wg_ref, wu_ref, wd_ref, ys_ref, wg_s, wu_s, wd_s):
    j = pl.program_id(0)

    @pl.when(j < nu_ref[0])
    def _():
        @pl.when(jnp.logical_or(j == 0, be_ref[j] != be_ref[jnp.maximum(j - 1, 0)]))
        def _():
            wg_s[...] = wg_ref[0, 0].astype(BF16)
            wu_s[...] = wu_ref[0, 0].astype(BF16)
            wd_s[...] = wd_ref[0, 0].astype(BF16)

        h = _from_slabs(xs_ref, 0, MOE_BLOCK).astype(BF16)
        a = (_silu(jnp.dot(h, wg_s[...], preferred_element_type=F32))
             * jnp.dot(h, wu_s[...], preferred_element_type=F32))
        _to_slabs(ys_ref, 0, jnp.dot(a.astype(BF16), wd_s[...], preferred_element_type=F32))

    @pl.when(j >= nu_ref[0])
    def _():
        ys_ref[...] = jnp.zeros_like(ys_ref)


def _experts(blk_e, n_used, xs, wg, wu, wd, layer):
    d, de = wg.shape[-2:]
    blk = MOE_BLOCK * CHUNKS
    last = lambda j, nu: jnp.minimum(j, jnp.maximum(nu[0] - 1, 0))
    return pl.pallas_call(
        _expert_kernel,
        out_shape=jax.ShapeDtypeStruct(xs.shape, F32),
        grid_spec=pltpu.PrefetchScalarGridSpec(
            num_scalar_prefetch=2, grid=(xs.shape[0] // blk,),
            in_specs=[pl.BlockSpec((blk, LANES), lambda j, be, nu: (last(j, nu), 0)),
                      pl.BlockSpec((1, 1, d, de), lambda j, be, nu: (layer, be[last(j, nu)], 0, 0)),
                      pl.BlockSpec((1, 1, d, de), lambda j, be, nu: (layer, be[last(j, nu)], 0, 0)),
                      pl.BlockSpec((1, 1, de, d), lambda j, be, nu: (layer, be[last(j, nu)], 0, 0))],
            out_specs=pl.BlockSpec((blk, LANES), lambda j, be, nu: (j, 0)),
            scratch_shapes=[pltpu.VMEM((d, de), BF16), pltpu.VMEM((d, de), BF16), pltpu.VMEM((de, d), BF16)]),
        compiler_params=_params("arbitrary"), name="moe_experts",
    )(blk_e, n_used, xs, wg, wu, wd)


def _combine_kernel(pos_ref, x_ref, mod_ref, w_ref, ys_ref, *rest, rows, nt, final):
    if final:
        gf_ref, o_ref, buf, sem = rest
    else:
        o_ref, buf, sem = rest
    i = pl.program_id(0)
    slot = i % 2
    slab_rows = TM * CHUNKS
    where = lambda s, k: (s * 2 + k) * slab_rows

    def issue(row_tile, s):
        base = row_tile * TM

        def body(grp, carry):
            for u in range(SUBLANES):
                t = grp * SUBLANES + u
                for k in range(2):
                    p = pl.multiple_of(pos_ref[k * rows + base + t], CHUNKS)
                    dst = buf.at[pl.ds(pl.multiple_of(where(s, k) + t * CHUNKS, CHUNKS), CHUNKS)]
                    pltpu.make_async_copy(ys_ref.at[pl.ds(p, CHUNKS)], dst, sem.at[s]).start(priority=k)
            return carry

        lax.fori_loop(0, TM // SUBLANES, body, 0)

    @pl.when(i == 0)
    def _():
        issue(0, 0)

    @pl.when(i + 1 < nt)
    def _():
        issue(i + 1, 1 - slot)

    for k in range(2):
        pltpu.make_async_copy(ys_ref.at[pl.ds(0, slab_rows)],
                              buf.at[pl.ds(pl.multiple_of(where(slot, k), slab_rows), slab_rows)],
                              sem.at[slot]).wait()
    w = w_ref[...]
    y = w[:, 0:1] * _from_slabs(buf, where(slot, 0), TM) + w[:, 1:2] * _from_slabs(buf, where(slot, 1), TM)
    x = x_ref[...] + mod_ref[0, 5:6, :] * y
    if final:
        x = x * lax.rsqrt(jnp.mean(x * x, axis=-1, keepdims=True) + EPS) * gf_ref[...]
    o_ref[...] = x


def _combine(pos, x, mods, w_cols, ys, *, n_x_tiles, g_final=None):
    r, d = x.shape
    final = g_final is not None
    nt = n_x_tiles if final else r // TM
    in_specs = [pl.BlockSpec((TM, d), lambda i, p: (i, 0)),
                pl.BlockSpec((1, 6, d), lambda i, p: (i // n_x_tiles, 0, 0)),
                pl.BlockSpec((TM, 2), lambda i, p: (i, 0)),
                pl.BlockSpec(memory_space=pl.ANY)]
    args = [pos, x, mods, w_cols, ys]
    if final:
        in_specs.append(pl.BlockSpec((1, d), lambda i, p: (0, 0)))
        args.append(g_final.reshape(1, d))
    return pl.pallas_call(
        functools.partial(_combine_kernel, rows=r, nt=nt, final=final),
        out_shape=jax.ShapeDtypeStruct((nt * TM, d), F32),
        grid_spec=pltpu.PrefetchScalarGridSpec(
            num_scalar_prefetch=1, grid=(nt,), in_specs=in_specs,
            out_specs=pl.BlockSpec((TM, d), lambda i, p: (i, 0)),
            scratch_shapes=[pltpu.VMEM((2 * 2 * TM * CHUNKS, LANES), F32), pltpu.SemaphoreType.DMA((2,))]),
        compiler_params=_params("arbitrary"), name="moe_combine",
    )(*args)


def _moe(x, routing, g, mods, wg, wu, wd, layer, *, n_x_tiles, g_final=None, defer=False):
    r = x.shape[0]
    n_blocks = -(-2 * r // MOE_BLOCK) + N_EXPERTS
    e01, w01, rank01, counts = routing
    pos, blk_e, n_used, fill = _dispatch_plan(e01, rank01, counts, n_blocks)
    xs = _dispatch(pos, fill, x, g, mods, n_blocks, n_x_tiles=n_x_tiles)
    ys = _experts(blk_e, n_used, xs, wg, wu, wd, layer)
    if defer:
        return pos, mods, w01.T, ys
    return _combine(pos, x, mods, w01.T, ys, n_x_tiles=n_x_tiles, g_final=g_final)


def _rope_tables(seq_len, n_rows_total):
    half = HEAD_DIM // 2
    inv_freq = ROPE_BASE ** (-jnp.arange(0, half, 2, dtype=F32) / half)
    first = jnp.arange(half) < (half // 2)

    def tab(n):
        ang = jnp.arange(n, dtype=F32)[:, None] * inv_freq
        ang = jnp.concatenate([ang, ang], axis=-1)
        c, s = jnp.cos(ang), jnp.sin(ang)
        return jnp.stack([c, jnp.where(first, -s, 0.0), jnp.where(first, 0.0, s)], axis=1)

    ident = jnp.stack([jnp.ones((half,), F32), jnp.zeros((half,), F32), jnp.zeros((half,), F32)])
    reps = LANES // HEAD_DIM
    n_ctx_rows = (n_rows_total - seq_len) // GRID_W
    row = jnp.concatenate([tab(seq_len // GRID_W), jnp.broadcast_to(ident, (n_ctx_rows, 3, half))], axis=0)
    row = jnp.tile(jnp.concatenate([row, jnp.zeros_like(row)], axis=-1), (1, 1, reps))
    col = jnp.stack([tab(GRID_W), jnp.broadcast_to(ident, (GRID_W, 3, half))])
    col = jnp.tile(jnp.concatenate([jnp.zeros_like(col), col], axis=-1), (1, TM // GRID_W, 1, reps))
    return row, col.transpose(0, 2, 1, 3)


def kernel(x, c, ctx, c_ctx, w_mod, b_mod, g_mix, g_ffn, g_final, conv_w_in, conv_w, conv_w_out,
           gla_w_proj, gla_w_gk1, gla_w_gk2, gla_b_gk, gla_g_norm, gla_w_out, swa_w_qkv, swa_sinks,
           swa_w_out, na_w_qkv, na_rpb, na_w_out, router_w, router_b, moe_w_gate, moe_w_up, moe_w_down):
    seq_len, d = x.shape[1], x.shape[2]
    ctx_len = ctx.shape[1]
    assert x.shape[0] == 1 and ctx_len == TM and seq_len % TM == 0 and d == D_MODEL
    assert seq_len % GRID_W == 0 and seq_len // GRID_W >= NA_KH
    n_x_tiles = seq_len // TM
    rows = seq_len + ctx_len
    xs = jnp.concatenate([x[0], ctx[0]], axis=0)
    mods_all = _ada_params(c, c_ctx, w_mod, b_mod)
    w_router = jnp.pad(router_w.astype(F32), ((0, 0), (0, LANES - N_EXPERTS)))
    w_router_t = jnp.stack(_split2(w_router))
    n_mixers = 4
    route = lambda i: (g_ffn[i], w_router_t, router_b)

    pending = None
    for i in range(DEPTH):
        kind, j = i % n_mixers, i // n_mixers
        mods = mods_all[i]
        if kind == 0:
            if pending is not None:
                xs = _combine(pending[0], xs, *pending[1:], n_x_tiles=n_x_tiles)
            u = _project(xs, g_mix[i], mods, conv_w_in[j].astype(BF16), seq_len=seq_len)
            xs, routing = _conv_mixer(xs, u, conv_w[j], mods, conv_w_out[j].astype(BF16), route(i), seq_len=seq_len)
        elif kind == 1:
            n_extra = LANES
            w_low = jnp.concatenate([gla_w_gk1[j, 0], gla_w_gk1[j, 1],
                                     jnp.zeros((d, n_extra - 2 * GLA_RANK), F32)], axis=1).astype(BF16)
            out = _project(xs, g_mix[i], mods, gla_w_proj[j].astype(BF16), seq_len=seq_len, w_extra=w_low,
                           pending=pending)
            if pending is not None:
                xs, out = out[0], out[1:]
            u, r_low = out
            w2pad = jnp.stack(
                [jnp.zeros((n_extra, GLA_DK), F32).at[k * GLA_RANK:(k + 1) * GLA_RANK].set(gla_w_gk2[j, k])
                 for k in range(2)]).astype(BF16)
            o_f, o_b = _gla_scan(u, r_low, w2pad, gla_b_gk[j])
            xs, routing = _gla_out(xs, o_f, o_b, u, gla_g_norm[j], mods, gla_w_out[j].astype(BF16), route(i),
                                   seq_len=seq_len)
        elif kind == 2:
            if pending is not None:
                xs = _combine(pending[0], xs, *pending[1:], n_x_tiles=n_x_tiles)
            rope = _rope_tables(seq_len, rows)
            u = _project(xs, g_mix[i], mods, swa_w_qkv[j].astype(BF16), seq_len=seq_len,
                         q_cols=SWA_HEADS * HEAD_DIM, q_scale=HEAD_DIM ** -0.5 * LOG2E, rope=rope,
                         rope_cols=(SWA_HEADS + SWA_KV_HEADS) * HEAD_DIM)
            ypre = _swa_attention(u, swa_sinks[j], seq_len=seq_len)
            xs, routing = _out_proj(xs, ypre, mods, swa_w_out[j].astype(BF16), route(i), seq_len=seq_len)
        else:
            out = _project(xs, g_mix[i], mods, na_w_qkv[j].astype(BF16), seq_len=seq_len,
                           q_cols=NA_HEADS * HEAD_DIM, q_scale=HEAD_DIM ** -0.5 * LOG2E, pending=pending)
            xs, u = out if pending is not None else (xs, out)
            ypre = _na_attention(u, _na_bias_table(na_rpb[j]), n_rows=seq_len // GRID_W)
            xs, routing = _out_proj(xs, ypre, mods, na_w_out[j].astype(BF16), route(i), seq_len=seq_len)
        last = i == DEPTH - 1
        res = _moe(xs, routing, g_ffn[i], mods, moe_w_gate, moe_w_up, moe_w_down, i,
                   n_x_tiles=n_x_tiles, g_final=g_final if last else None, defer=not last)
        xs, pending = (res, None) if last else (xs, res)
    return xs[None]
```

```python
import functools

import jax
import jax.numpy as jnp
from jax import lax
from jax.experimental import pallas as pl
from jax.experimental.pallas import tpu as pltpu

F32 = jnp.float32
BF16 = jnp.bfloat16
HIGHEST = lax.Precision.HIGHEST

D_MODEL = 1024
DEPTH = 4
GRID_W = 64
EPS = 1e-6
CONV_W = 3
GLA_HEADS = 4
GLA_DK = D_MODEL // 2
GLA_DV = D_MODEL
GLA_HK = GLA_DK // GLA_HEADS
GLA_HV = GLA_DV // GLA_HEADS
GLA_RANK = 16
GLA_NORMALIZER = 16.0
GLA_CHUNK = 64
HEAD_DIM = 64
SWA_HEADS = D_MODEL // HEAD_DIM
SWA_KV_HEADS = SWA_HEADS // 4
SWA_WINDOW = 128
ROPE_BASE = 10000.0
NA_HEADS = D_MODEL // HEAD_DIM
NA_KH = 8
NA_KW = 16
N_EXPERTS = 16
N_GROUPS = 4
EXPERTS_PER_GROUP = N_EXPERTS // N_GROUPS
D_EXPERT = D_MODEL // 2

TM = 256
MOE_BLOCK = 512
LANES = 128
SUBLANES = 8
CHUNKS = D_MODEL // LANES
NEG = -1e30
LOG2E = 1.4426950408889634
VMEM_LIMIT = 56 * 1024 * 1024

_NT = (((1,), (1,)), ((), ()))
_TN = (((0,), (0,)), ((), ()))


def _params(*sem):
    return pltpu.CompilerParams(dimension_semantics=sem, vmem_limit_bytes=VMEM_LIMIT)


def _norm_mod(x, g, scale, shift):
    ms = jnp.mean(x * x, axis=-1, keepdims=True)
    return (x * lax.rsqrt(ms + EPS) * g) * (1.0 + scale) + shift


def _silu(x):
    return x * jax.nn.sigmoid(x)


def _split2(x):
    hi = x.astype(BF16)
    return hi, (x - hi.astype(F32)).astype(BF16)


def _ada_kernel(cond_ref, w_ref, b_ref, o_ref):
    s_hi, s_lo = _split2(_silu(cond_ref[...]))
    w_hi, w_lo = _split2(w_ref[0])
    o_ref[0] = (jnp.dot(s_hi, w_hi, preferred_element_type=F32) + jnp.dot(s_hi, w_lo, preferred_element_type=F32)
                + jnp.dot(s_lo, w_hi, preferred_element_type=F32) + b_ref[0])


def _ada_params(c, c_ctx, w_mod, b_mod):
    depth, d, n = w_mod.shape
    nc = n // 4
    cond = jnp.zeros((8, d), F32).at[0].set(c[0]).at[1].set(c_ctx)
    out = pl.pallas_call(
        _ada_kernel,
        out_shape=jax.ShapeDtypeStruct((depth, 8, n), F32),
        grid=(depth, n // nc),
        in_specs=[pl.BlockSpec((8, d), lambda l, j: (0, 0)),
                  pl.BlockSpec((1, d, nc), lambda l, j: (l, 0, j)),
                  pl.BlockSpec((1, 1, nc), lambda l, j: (l, 0, j))],
        out_specs=pl.BlockSpec((1, 8, nc), lambda l, j: (l, 0, j)),
        compiler_params=_params("parallel", "parallel"),
        name="ada_params",
    )(cond, w_mod, b_mod.reshape(depth, 1, n))
    return out[:, :2].reshape(depth, 2, 6, d)


def _row_tile(rows, cap):
    return next(t for t in (1280, 640, TM) if t <= cap and rows % t == 0)


def _mod_rows(mod_ref, tb, seq_len):
    row = pl.program_id(0) * tb + lax.broadcasted_iota(jnp.int32, (tb, 1), 0)
    is_ctx = row >= seq_len
    return lambda j: jnp.where(is_ctx, mod_ref[1, j:j + 1, :], mod_ref[0, j:j + 1, :])


def _proj_kernel(*refs, tb, seq_len, n_rows, n_chunk, q_cols, q_scale, rope_cols, has_extra, fused):
    if fused:
        pos_ref, refs = refs[0], refs[1:]
    x_ref, g_ref, mod_ref, w_ref = refs[:4]
    rest = refs[4:]
    n_main = w_ref.shape[1]
    if has_extra:
        wx_ref, rest = rest[0], rest[1:]
    if rope_cols:
        row_ref, col_ref = rest[:2]
        rest = rest[2:]
        rpt = TM // GRID_W
        cos_t, sa_t, sb_t = [
            jnp.concatenate([jnp.broadcast_to(row_ref[a, k:k + 1, :], (GRID_W, LANES)) for a in range(rpt)],
                            axis=0) + col_ref[0, k]
            for k in range(3)]
    if fused:
        modp_ref, wc_ref, ys_ref = rest[:3]
        rest = rest[3:]
    o_ref = rest[0]
    if fused:
        xo_ref, buf, sem = rest[-3:]
        i = pl.program_id(0)
        nt = pl.num_programs(0)
        slot = i % 2
        slab_rows = TM * CHUNKS
        where = lambda s, k: (s * 2 + k) * slab_rows

        def start(t, base, s):
            for k in range(2):
                p = pl.multiple_of(pos_ref[k * n_rows + base + t], CHUNKS)
                dst = buf.at[pl.ds(pl.multiple_of(where(s, k) + t * CHUNKS, CHUNKS), CHUNKS)]
                pltpu.make_async_copy(ys_ref.at[pl.ds(p, CHUNKS)], dst, sem.at[s]).start(priority=k)

        def wait(s):
            for k in range(2):
                pltpu.make_async_copy(ys_ref.at[pl.ds(0, slab_rows)],
                                      buf.at[pl.ds(pl.multiple_of(where(s, k), slab_rows), slab_rows)],
                                      sem.at[s]).wait()

        @pl.when(i == 0)
        def _():
            def body(t, carry):
                start(t, 0, 0)
                return carry
            lax.fori_loop(0, TM, body, 0)

        wait(slot)
        wc = wc_ref[...]
        y = wc[:, 0:1] * _from_slabs(buf, where(slot, 0), TM) + wc[:, 1:2] * _from_slabs(buf, where(slot, 1), TM)
        x = x_ref[...] + _mod_rows(modp_ref, tb, seq_len)(5) * y
        xo_ref[...] = x
        nxt = jnp.where(i + 1 < nt, i + 1, 0) * TM
        for t in range(TM):
            start(t, nxt, 1 - slot)
    else:
        x = x_ref[...]
    pick = _mod_rows(mod_ref, tb, seq_len)
    h = _norm_mod(x, g_ref[...], pick(1), pick(0)).astype(BF16)
    for c0 in range(0, n_main, n_chunk):
        c1 = min(c0 + n_chunk, n_main)
        y = jnp.dot(h, w_ref[:, c0:c1], preferred_element_type=F32)
        if c0 < q_cols:
            y = y * q_scale
        if c0 >= rope_cols:
            o_ref[:, c0:c1] = y.astype(o_ref.dtype)
            continue
        for s0 in range(c0, c1, LANES):
            ys = y[:, s0 - c0:s0 - c0 + LANES]
            if s0 < rope_cols:
                ys = ys * cos_t + pltpu.roll(ys, LANES - 16, 1) * sa_t + pltpu.roll(ys, 16, 1) * sb_t
            o_ref[:, s0:s0 + LANES] = ys.astype(o_ref.dtype)
    if has_extra:
        rest[1][...] = jnp.dot(h, wx_ref[...], preferred_element_type=F32)
    if fused:
        @pl.when(i == nt - 1)
        def _():
            wait(1 - slot)


def _project(x, g, mods, w, *, seq_len, n_chunk=512, q_cols=0, q_scale=1.0, rope=None, rope_cols=0,
             w_extra=None, pending=None):
    r, d = x.shape
    n_main = w.shape[1]
    n_extra = 0 if w_extra is None else w_extra.shape[1]
    fused = pending is not None
    tb = TM if (rope_cols or fused) else _row_tile(r, 1280)
    im = lambda f: (lambda i, *_: f(i))
    in_specs = [pl.BlockSpec((tb, d), im(lambda i: (i, 0))),
                pl.BlockSpec((1, d), im(lambda i: (0, 0))),
                pl.BlockSpec((2, 6, d), im(lambda i: (0, 0, 0))),
                pl.BlockSpec((d, n_main), im(lambda i: (0, 0)))]
    args = [x, g.reshape(1, d), mods, w]
    if n_extra:
        in_specs.append(pl.BlockSpec((d, n_extra), im(lambda i: (0, 0))))
        args.append(w_extra)
    if rope_cols:
        rpt = TM // GRID_W
        in_specs += [pl.BlockSpec((rpt, 3, LANES), im(lambda i: (i, 0, 0))),
                     pl.BlockSpec((1, 3, TM, LANES), im(lambda i: (i // (seq_len // TM), 0, 0, 0)))]
        args += list(rope)
    out_shape = [jax.ShapeDtypeStruct((r, n_main), BF16)]
    out_specs = [pl.BlockSpec((tb, n_main), im(lambda i: (i, 0)))]
    if n_extra:
        out_shape.append(jax.ShapeDtypeStruct((r, n_extra), F32))
        out_specs.append(pl.BlockSpec((tb, n_extra), im(lambda i: (i, 0))))
    scratch = []
    if fused:
        pos, mods_prev, w_cols, ys = pending
        in_specs += [pl.BlockSpec((2, 6, d), im(lambda i: (0, 0, 0))),
                     pl.BlockSpec((tb, 2), im(lambda i: (i, 0))),
                     pl.BlockSpec(memory_space=pl.ANY)]
        args = [pos] + args + [mods_prev, w_cols, ys]
        out_shape.append(jax.ShapeDtypeStruct((r, d), F32))
        out_specs.append(pl.BlockSpec((tb, d), im(lambda i: (i, 0))))
        scratch = [pltpu.VMEM((2 * 2 * TM * CHUNKS, LANES), F32), pltpu.SemaphoreType.DMA((2,))]
    out = pl.pallas_call(
        functools.partial(_proj_kernel, tb=tb, seq_len=seq_len, n_rows=r, n_chunk=n_chunk, q_cols=q_cols,
                          q_scale=q_scale, rope_cols=rope_cols, has_extra=bool(n_extra), fused=fused),
        out_shape=out_shape,
        grid_spec=pltpu.PrefetchScalarGridSpec(
            num_scalar_prefetch=int(fused), grid=(r // tb,), in_specs=in_specs, out_specs=out_specs,
            scratch_shapes=scratch),
        compiler_params=_params("arbitrary" if fused else "parallel"),
        name="combine_norm_proj" if fused else "norm_proj",
    )(*args)
    if fused:
        return (out[-1],) + tuple(out[:-1])
    return out if n_extra else out[0]


def _out_proj_kernel(x_ref, y_ref, mod_ref, w_ref, o_ref, *, tb, seq_len):
    y = jnp.dot(y_ref[...], w_ref[...], preferred_element_type=F32)
    o_ref[...] = x_ref[...] + _mod_rows(mod_ref, tb, seq_len)(2) * y


def _out_proj(x, ypre, mods, w, *, seq_len):
    r, d = x.shape
    tb = _row_tile(r, 1280)
    return pl.pallas_call(
        functools.partial(_out_proj_kernel, tb=tb, seq_len=seq_len),
        out_shape=jax.ShapeDtypeStruct((r, d), F32),
        grid=(r // tb,),
        in_specs=[pl.BlockSpec((tb, d), lambda i: (i, 0)),
                  pl.BlockSpec((tb, d), lambda i: (i, 0)),
                  pl.BlockSpec((2, 6, d), lambda i: (0, 0, 0)),
                  pl.BlockSpec((d, d), lambda i: (0, 0))],
        out_specs=pl.BlockSpec((tb, d), lambda i: (i, 0)),
        compiler_params=_params("parallel"), name="out_proj",
    )(x, ypre, mods, w)


def _conv_kernel(x_ref, bg_ref, cg_ref, v_ref, cgp_ref, vp_ref, cgn_ref, vn_ref, cw_ref, mod_ref, w_ref,
                 o_ref, *, tb, seq_len, n_rows):
    z = cg_ref[...].astype(F32) * v_ref[...].astype(F32)
    z_prev = cgp_ref[SUBLANES - 1:SUBLANES, :].astype(F32) * vp_ref[SUBLANES - 1:SUBLANES, :].astype(F32)
    z_next = cgn_ref[0:1, :].astype(F32) * vn_ref[0:1, :].astype(F32)
    row = lax.broadcasted_iota(jnp.int32, z.shape, 0)
    pos = pl.program_id(0) * tb + row
    z_dn = jnp.where(row == 0, z_prev, pltpu.roll(z, 1, 0))
    z_up = jnp.where(row == tb - 1, z_next, pltpu.roll(z, tb - 1, 0))
    z_dn = jnp.where(jnp.logical_or(pos == 0, pos == seq_len), 0.0, z_dn)
    z_up = jnp.where(jnp.logical_or(pos == seq_len - 1, pos == n_rows - 1), 0.0, z_up)
    conv = z_dn * cw_ref[0:1, :] + z * cw_ref[1:2, :] + z_up * cw_ref[2:3, :]
    ypre = (bg_ref[...].astype(F32) * conv).astype(BF16)
    y = jnp.dot(ypre, w_ref[...], preferred_element_type=F32)
    o_ref[...] = x_ref[...] + _mod_rows(mod_ref, tb, seq_len)(2) * y


def _conv_mixer(x, u, conv_w, mods, w_out, *, seq_len):
    r, d = x.shape
    tb = _row_tile(r, 640)
    n8 = r // SUBLANES
    t8 = tb // SUBLANES
    tile = lambda c: pl.BlockSpec((tb, d), lambda i: (i, c))
    prev = lambda c: pl.BlockSpec((SUBLANES, d), lambda i: (jnp.maximum(i * t8 - 1, 0), c))
    nxt = lambda c: pl.BlockSpec((SUBLANES, d), lambda i: (jnp.minimum((i + 1) * t8, n8 - 1), c))
    return pl.pallas_call(
        functools.partial(_conv_kernel, tb=tb, seq_len=seq_len, n_rows=r),
        out_shape=jax.ShapeDtypeStruct((r, d), F32),
        grid=(r // tb,),
        in_specs=[pl.BlockSpec((tb, d), lambda i: (i, 0)), tile(0), tile(1), tile(2),
                  prev(1), prev(2), nxt(1), nxt(2),
                  pl.BlockSpec((CONV_W, d), lambda i: (0, 0)),
                  pl.BlockSpec((2, 6, d), lambda i: (0, 0, 0)),
                  pl.BlockSpec((d, d), lambda i: (0, 0))],
        out_specs=pl.BlockSpec((tb, d), lambda i: (i, 0)),
        compiler_params=_params("parallel"), name="conv_mixer",
    )(x, u, u, u, u, u, u, u, conv_w, mods, w_out)


def _log_sigmoid(z):
    return jnp.minimum(z, 0.0) - jnp.log1p(jnp.exp(-jnp.abs(z)))


def _gla_chunk(q_ref, k_ref, v_ref, gk, o_ref, st_ref, d, c, reverse):
    C = GLA_CHUNK
    rows = slice(c * C, (c + 1) * C)
    ii = lax.broadcasted_iota(jnp.int32, (C, C), 0)
    jj = lax.broadcasted_iota(jnp.int32, (C, C), 1)
    causal = (jj >= ii) if reverse else (jj <= ii)
    ones = jnp.where(causal, 1.0, 0.0).astype(BF16)
    g_hi, g_lo = _split2(gk[rows])
    bcum = (jnp.dot(ones, g_hi, preferred_element_type=F32)
            + jnp.dot(ones, g_lo, preferred_element_type=F32))
    total = bcum[0:1, :] if reverse else bcum[C - 1:C, :]
    q = q_ref[rows, :].astype(F32) * (GLA_HK ** -0.5)
    k = k_ref[rows, :].astype(F32)
    q_dec = (q * jnp.exp(bcum)).astype(BF16)
    k_inv = (k * jnp.exp(-bcum)).astype(BF16)
    k_end = (k * jnp.exp(total - bcum)).astype(BF16)
    decay = jnp.exp(total)
    for h in range(GLA_HEADS):
        kcols = slice(h * GLA_HK, (h + 1) * GLA_HK)
        vcols = slice(h * GLA_HV, (h + 1) * GLA_HV)
        vh = v_ref[rows, vcols]
        att = lax.dot_general(q_dec[:, kcols], k_inv[:, kcols], _NT, preferred_element_type=F32)
        att = jnp.where(causal, att, 0.0).astype(BF16)
        st = st_ref[d, h]
        o = (jnp.dot(att, vh, preferred_element_type=F32)
             + lax.dot_general(q_dec[:, kcols], st.astype(BF16), _NT, preferred_element_type=F32))
        st_ref[d, h] = st * decay[:, kcols] + lax.dot_general(vh, k_end[:, kcols], _TN,
                                                              preferred_element_type=F32)
        o_ref[rows, vcols] = o.astype(o_ref.dtype)


def _gla_kernel(qf_ref, kf_ref, vf_ref, rf_ref, qb_ref, kb_ref, vb_ref, rb_ref, w2_ref, b_ref,
                of_ref, ob_ref, st_ref):
    @pl.when(pl.program_id(0) == 0)
    def _():
        st_ref[...] = jnp.zeros_like(st_ref)

    def gates(r_ref, d):
        z = jnp.dot(r_ref[...].astype(BF16), w2_ref[d], preferred_element_type=F32) + b_ref[d]
        return _log_sigmoid(z) * (1.0 / GLA_NORMALIZER)

    gk_f, gk_b = gates(rf_ref, 0), gates(rb_ref, 1)
    n_chunks = TM // GLA_CHUNK
    for c in range(n_chunks):
        _gla_chunk(qf_ref, kf_ref, vf_ref, gk_f, of_ref, st_ref, 0, c, False)
        _gla_chunk(qb_ref, kb_ref, vb_ref, gk_b, ob_ref, st_ref, 1, n_chunks - 1 - c, True)


def _gla_scan(u, r_low, w2pad, b_gk):
    rws = u.shape[0]
    nt = rws // TM
    fwd = lambda s: (s + nt - 1) % nt
    bwd = lambda s: nt - 1 - s
    qkvr = lambda order: [pl.BlockSpec((TM, GLA_DK), lambda s: (order(s), 0)),
                          pl.BlockSpec((TM, GLA_DK), lambda s: (order(s), 1)),
                          pl.BlockSpec((TM, GLA_DV), lambda s: (order(s), 1)),
                          pl.BlockSpec((TM, LANES), lambda s: (order(s), 0))]
    out = jax.ShapeDtypeStruct((rws, GLA_DV), BF16)
    return pl.pallas_call(
        _gla_kernel,
        out_shape=[out, out],
        grid=(nt,),
        in_specs=qkvr(fwd) + qkvr(bwd) + [pl.BlockSpec((2, LANES, GLA_DK), lambda s: (0, 0, 0)),
                                          pl.BlockSpec((2, 1, GLA_DK), lambda s: (0, 0, 0))],
        out_specs=[pl.BlockSpec((TM, GLA_DV), lambda s: (fwd(s), 0)),
                   pl.BlockSpec((TM, GLA_DV), lambda s: (bwd(s), 0))],
        scratch_shapes=[pltpu.VMEM((2, GLA_HEADS, GLA_HV, GLA_HK), F32)],
        compiler_params=_params("arbitrary"), name="gla_scan",
    )(u, u, u, r_low, u, u, u, r_low, w2pad, b_gk.reshape(2, 1, GLA_DK))


def _gla_out_kernel(x_ref, of_ref, ob_ref, g_ref, gn_ref, mod_ref, w_ref, o_ref, *, tb, seq_len):
    o = of_ref[...].astype(F32) + ob_ref[...].astype(F32)
    parts = []
    for h in range(GLA_HEADS):
        oh = o[:, h * GLA_HV:(h + 1) * GLA_HV]
        parts.append(oh * lax.rsqrt(jnp.mean(oh * oh, axis=-1, keepdims=True) + EPS) * gn_ref[...])
    ypre = (jnp.concatenate(parts, axis=1) * _silu(g_ref[...].astype(F32))).astype(BF16)
    y = jnp.dot(ypre, w_ref[...], preferred_element_type=F32)
    o_ref[...] = x_ref[...] + _mod_rows(mod_ref, tb, seq_len)(2) * y


def _gla_out(x, o_f, o_b, u, g_norm, mods, w, *, seq_len):
    r, d = x.shape
    tb = _row_tile(r, 640)
    tile = lambda c: pl.BlockSpec((tb, d), lambda i: (i, c))
    return pl.pallas_call(
        functools.partial(_gla_out_kernel, tb=tb, seq_len=seq_len),
        out_shape=jax.ShapeDtypeStruct((r, d), F32),
        grid=(r // tb,),
        in_specs=[tile(0), tile(0), tile(0), tile(2),
                  pl.BlockSpec((1, GLA_HV), lambda i: (0, 0)),
                  pl.BlockSpec((2, 6, d), lambda i: (0, 0, 0)),
                  pl.BlockSpec((d, d), lambda i: (0, 0))],
        out_specs=tile(0),
        compiler_params=_params("parallel"), name="gla_out",
    )(x, o_f, o_b, u, g_norm.reshape(1, GLA_HV), mods, w)


def _swa_kernel(sink_ref, q_ref, kvp_ref, kvo_ref, kvn_ref, kvc_ref, o_ref, kv_buf, *, seq_len):
    i = pl.program_id(0)
    half = TM // 2
    kv_buf[0:half] = kvp_ref[...]
    kv_buf[half:half + TM] = kvo_ref[...]
    kv_buf[half + TM:2 * TM] = kvn_ref[...]
    kv_buf[2 * TM:3 * TM] = kvc_ref[...]
    nk = 3 * TM
    qpos = i * TM + lax.broadcasted_iota(jnp.int32, (TM, nk), 0)
    col = lax.broadcasted_iota(jnp.int32, (TM, nk), 1)
    kpos = i * TM - half + col
    local = ((jnp.abs(kpos - qpos) <= SWA_WINDOW) & (kpos >= 0) & (kpos < seq_len) & (qpos < seq_len))
    valid = local | (col >= 2 * TM)
    kv_w = SWA_KV_HEADS * HEAD_DIM
    group = SWA_HEADS // SWA_KV_HEADS
    for kv in range(SWA_KV_HEADS):
        heads = range(kv * group, (kv + 1) * group)
        qg = jnp.concatenate([q_ref[:, h * HEAD_DIM:(h + 1) * HEAD_DIM] for h in heads], axis=0)
        kh = kv_buf[:, kv * HEAD_DIM:(kv + 1) * HEAD_DIM]
        vh = kv_buf[:, kv_w + kv * HEAD_DIM:kv_w + (kv + 1) * HEAD_DIM]
        s = lax.dot_general(qg, kh, _NT, preferred_element_type=F32)
        ps, ls = [], []
        for j, h in enumerate(heads):
            sj = jnp.where(valid, s[j * TM:(j + 1) * TM], NEG)
            sink = sink_ref[h] * LOG2E
            m = jnp.maximum(jnp.max(sj, axis=-1, keepdims=True), sink)
            p = jnp.exp2(sj - m)
            ls.append(jnp.sum(p, axis=-1, keepdims=True) + jnp.exp2(sink - m))
            ps.append(p.astype(BF16))
        o = jnp.dot(jnp.concatenate(ps, axis=0), vh, preferred_element_type=F32)
        for j, h in enumerate(heads):
            o_ref[:, h * HEAD_DIM:(h + 1) * HEAD_DIM] = (o[j * TM:(j + 1) * TM] / ls[j]).astype(o_ref.dtype)


def _swa_attention(u, sinks, *, seq_len):
    rws = u.shape[0]
    nt = rws // TM
    half = TM // 2
    n_half = rws // half
    kv_w = 2 * SWA_KV_HEADS * HEAD_DIM
    qw = SWA_HEADS * HEAD_DIM
    kvc = qw // kv_w
    return pl.pallas_call(
        functools.partial(_swa_kernel, seq_len=seq_len),
        out_shape=jax.ShapeDtypeStruct((rws, qw), BF16),
        grid=(nt,),
        in_specs=[pl.BlockSpec(memory_space=pltpu.SMEM),
                  pl.BlockSpec((TM, qw), lambda i: (i, 0)),
                  pl.BlockSpec((half, kv_w), lambda i: (jnp.maximum(2 * i - 1, 0), kvc)),
                  pl.BlockSpec((TM, kv_w), lambda i: (i, kvc)),
                  pl.BlockSpec((half, kv_w), lambda i: (jnp.minimum(2 * i + 2, n_half - 1), kvc)),
                  pl.BlockSpec((TM, kv_w), lambda i: (nt - 1, kvc))],
        out_specs=pl.BlockSpec((TM, qw), lambda i: (i, 0)),
        scratch_shapes=[pltpu.VMEM((3 * TM, kv_w), BF16)],
        compiler_params=_params("parallel"), name="swa_attention",
    )(sinks, u, u, u, u, u)


def _na_kernel(q_ref, kp_ref, ko_ref, kn_ref, kc_ref, vp_ref, vo_ref, vn_ref, vc_ref, bias_ref, o_ref,
               k_buf, v_buf, *, n_rows):
    i = pl.program_id(0)
    for j, (kr, vr) in enumerate(((kp_ref, vp_ref), (ko_ref, vo_ref), (kn_ref, vn_ref), (kc_ref, vc_ref))):
        k_buf[j * TM:(j + 1) * TM] = kr[...]
        v_buf[j * TM:(j + 1) * TM] = vr[...]
    n_loc = 3 * TM
    rpt = TM // GRID_W
    qi = lax.broadcasted_iota(jnp.int32, (TM, n_loc), 0)
    ki = lax.broadcasted_iota(jnp.int32, (TM, n_loc), 1)
    r = i * rpt + qi // GRID_W
    c = qi % GRID_W
    krow = (i - 1) * rpt + ki // GRID_W
    kcol = ki % GRID_W
    r0 = jnp.clip(r - NA_KH // 2, 0, n_rows - NA_KH)
    c0 = jnp.clip(c - NA_KW // 2, 0, GRID_W - NA_KW)
    valid = ((krow >= r0) & (krow < r0 + NA_KH) & (kcol >= c0) & (kcol < c0 + NA_KW) & (r < n_rows))
    for h in range(NA_HEADS):
        cols = slice(h * HEAD_DIM, (h + 1) * HEAD_DIM)
        qh = q_ref[:, cols]
        s = lax.dot_general(qh, k_buf[:, cols], _NT, preferred_element_type=F32)
        s_loc = jnp.where(valid, s[:, :n_loc] + bias_ref[h], NEG)
        s_ctx = s[:, n_loc:]
        m = jnp.maximum(jnp.max(s_loc, axis=-1, keepdims=True), jnp.max(s_ctx, axis=-1, keepdims=True))
        p_loc = jnp.exp2(s_loc - m)
        p_ctx = jnp.exp2(s_ctx - m)
        l = jnp.sum(p_loc, axis=-1, keepdims=True) + jnp.sum(p_ctx, axis=-1, keepdims=True)
        o = (jnp.dot(p_loc.astype(BF16), v_buf[0:n_loc, cols], preferred_element_type=F32)
             + jnp.dot(p_ctx.astype(BF16), v_buf[n_loc:, cols], preferred_element_type=F32)) / l
        o_ref[:, cols] = o.astype(o_ref.dtype)


def _na_bias_kernel(t_ref, o_ref, *, lo):
    rpt = TM // GRID_W
    for a in range(rpt):
        for b in range(3 * rpt):
            o_ref[0, a * GRID_W:(a + 1) * GRID_W, b * GRID_W:(b + 1) * GRID_W] = t_ref[0, lo - a + b]


def _na_bias_table(rpb):
    n_heads = rpb.shape[0]
    rpt = TM // GRID_W
    lo = NA_KH - 1 - rpt
    assert lo - (rpt - 1) >= 0 and lo + 3 * rpt <= 2 * NA_KH - 1
    c = jnp.arange(GRID_W)
    col_off = c[None, :] - c[:, None] + NA_KW - 1
    onehot = (col_off[None] == jnp.arange(2 * NA_KW - 1)[:, None, None]).astype(F32)
    tcol = jnp.einsum('hrj,jck->hrck', rpb.astype(F32) * LOG2E, onehot, precision=HIGHEST)
    return pl.pallas_call(
        functools.partial(_na_bias_kernel, lo=lo),
        out_shape=jax.ShapeDtypeStruct((n_heads, TM, 3 * TM), F32),
        grid=(n_heads,),
        in_specs=[pl.BlockSpec((1, 2 * NA_KH - 1, GRID_W, GRID_W), lambda h: (h, 0, 0, 0))],
        out_specs=pl.BlockSpec((1, TM, 3 * TM), lambda h: (h, 0, 0)),
        compiler_params=_params("parallel"), name="na_bias_table",
    )(tcol)


def _na_attention(u, bias, *, n_rows):
    rws = u.shape[0]
    nt = rws // TM
    d = NA_HEADS * HEAD_DIM
    blk = lambda f, c: pl.BlockSpec((TM, d), lambda i: (f(i), c))
    prev = lambda i: jnp.maximum(i - 1, 0)
    own = lambda i: i
    nxt = lambda i: jnp.minimum(i + 1, nt - 1)
    ctx = lambda i: nt - 1
    return pl.pallas_call(
        functools.partial(_na_kernel, n_rows=n_rows),
        out_shape=jax.ShapeDtypeStruct((rws, d), BF16),
        grid=(nt,),
        in_specs=[blk(own, 0), blk(prev, 1), blk(own, 1), blk(nxt, 1), blk(ctx, 1),
                  blk(prev, 2), blk(own, 2), blk(nxt, 2), blk(ctx, 2),
                  pl.BlockSpec((NA_HEADS, TM, 3 * TM), lambda i: (0, 0, 0))],
        out_specs=pl.BlockSpec((TM, d), lambda i: (i, 0)),
        scratch_shapes=[pltpu.VMEM((4 * TM, d), BF16), pltpu.VMEM((4 * TM, d), BF16)],
        compiler_params=_params("parallel"), name="na_attention",
    )(u, u, u, u, u, u, u, u, u, bias)


def _router_kernel(x_ref, g_ref, mod_ref, wr_ref, br_ref, e_ref, w_ref, rank_ref, cnt_ref, carry, *,
                   tb, seq_len):
    @pl.when(pl.program_id(0) == 0)
    def _():
        carry[...] = jnp.zeros_like(carry)

    pick = _mod_rows(mod_ref, tb, seq_len)
    h = _norm_mod(x_ref[...], g_ref[...], pick(4), pick(3))
    h_hi, h_lo = _split2(h)
    lt = (jnp.dot(h_hi, wr_ref[0], preferred_element_type=F32) + jnp.dot(h_hi, wr_ref[1], preferred_element_type=F32)
          + jnp.dot(h_lo, wr_ref[0], preferred_element_type=F32))
    logits = jnp.transpose(lt)[:N_EXPERTS]
    scores = jax.nn.sigmoid(logits)
    biased = scores + br_ref[...]
    row = lambda a, e: a[e:e + 1, :]
    best_g = jnp.zeros((1, tb), jnp.int32)
    best_s = None
    for g in range(N_GROUPS):
        v = [row(biased, g * EXPERTS_PER_GROUP + j) for j in range(EXPERTS_PER_GROUP)]
        gs = None
        for a in range(EXPERTS_PER_GROUP):
            for b in range(a + 1, EXPERTS_PER_GROUP):
                pair = v[a] + v[b]
                gs = pair if gs is None else jnp.maximum(gs, pair)
        if best_s is None:
            best_s = gs
        else:
            better = gs > best_s
            best_g = jnp.where(better, g, best_g)
            best_s = jnp.where(better, gs, best_s)
    picks = []
    for _ in range(2):
        top_v = jnp.full((1, tb), -jnp.inf, F32)
        top_i = jnp.full((1, tb), -1, jnp.int32)
        for e in range(N_EXPERTS):
            ok = best_g == (e // EXPERTS_PER_GROUP)
            for p in picks:
                ok = jnp.logical_and(ok, p != e)
            cand = jnp.where(ok, row(biased, e), -jnp.inf)
            better = cand > top_v
            top_i = jnp.where(better, e, top_i)
            top_v = jnp.where(better, cand, top_v)
        picks.append(top_i)
    e_iota = lax.broadcasted_iota(jnp.int32, (N_EXPERTS, tb), 0)
    sel0 = e_iota == picks[0]
    sel1 = e_iota == picks[1]
    w0 = jnp.sum(jnp.where(sel0, scores, 0.0), axis=0, keepdims=True)
    w1 = jnp.sum(jnp.where(sel1, scores, 0.0), axis=0, keepdims=True)
    tot = w0 + w1
    e_ref[0:1, :] = picks[0]
    e_ref[1:2, :] = picks[1]
    w_ref[0:1, :] = w0 / tot
    w_ref[1:2, :] = w1 / tot
    member = jnp.where(jnp.logical_or(sel0, sel1), 1.0, 0.0)
    earlier = (lax.broadcasted_iota(jnp.int32, (TM, TM), 0)
               < lax.broadcasted_iota(jnp.int32, (TM, TM), 1)).astype(BF16)
    seen = carry[:, 0:1]
    before = []
    for c in range(tb // TM):
        part = member[:, c * TM:(c + 1) * TM]
        before.append(seen + jnp.dot(part.astype(BF16), earlier, preferred_element_type=F32))
        seen = seen + jnp.sum(part, axis=1, keepdims=True)
    before = jnp.concatenate(before, axis=1)
    rank_ref[0:1, :] = jnp.sum(jnp.where(sel0, before, 0.0), axis=0, keepdims=True).astype(jnp.int32)
    rank_ref[1:2, :] = jnp.sum(jnp.where(sel1, before, 0.0), axis=0, keepdims=True).astype(jnp.int32)
    carry[...] = jnp.broadcast_to(seen, carry.shape)
    cnt_ref[...] = carry[...].astype(jnp.int32)


def _router(x, g, mods, w_router_t, b_router, *, seq_len):
    r, d = x.shape
    tb = _row_tile(r, 1280)
    pair = pl.BlockSpec((2, tb), lambda i: (0, i))
    return pl.pallas_call(
        functools.partial(_router_kernel, tb=tb, seq_len=seq_len),
        out_shape=[jax.ShapeDtypeStruct((2, r), jnp.int32), jax.ShapeDtypeStruct((2, r), F32),
                   jax.ShapeDtypeStruct((2, r), jnp.int32), jax.ShapeDtypeStruct((N_EXPERTS, LANES), jnp.int32)],
        grid=(r // tb,),
        in_specs=[pl.BlockSpec((tb, d), lambda i: (i, 0)),
                  pl.BlockSpec((1, d), lambda i: (0, 0)),
                  pl.BlockSpec((2, 6, d), lambda i: (0, 0, 0)),
                  pl.BlockSpec((2, d, LANES), lambda i: (0, 0, 0)),
                  pl.BlockSpec((N_EXPERTS, 1), lambda i: (0, 0))],
        out_specs=[pair, pair, pair, pl.BlockSpec((N_EXPERTS, LANES), lambda i: (0, 0))],
        scratch_shapes=[pltpu.VMEM((N_EXPERTS, LANES), F32)],
        compiler_params=_params("arbitrary"), name="router",
    )(x, g.reshape(1, d), mods, w_router_t, b_router.reshape(N_EXPERTS, 1))


def _dispatch_plan(e01, rank01, counts, n_blocks):
    cnt = counts[:, 0]
    nblk = (cnt + MOE_BLOCK - 1) // MOE_BLOCK
    blk_end = jnp.cumsum(nblk)
    off = (blk_end - nblk) * MOE_BLOCK
    eidx = jnp.arange(N_EXPERTS, dtype=jnp.int32)[:, None, None]
    pos = rank01 + jnp.sum(jnp.where(e01[None] == eidx, off[:, None, None], 0), axis=0)
    blk_e = jnp.sum(jnp.arange(n_blocks, dtype=jnp.int32)[:, None] >= blk_end[None, :], axis=1)
    blk_e = jnp.minimum(blk_e, N_EXPERTS - 1).astype(jnp.int32)
    n_used = blk_end[-1:]
    fill = jnp.concatenate([off + cnt, nblk * MOE_BLOCK - cnt, n_used]).astype(jnp.int32)
    slab_row = (pos * CHUNKS).reshape(-1).astype(jnp.int32)
    return slab_row, blk_e, n_used.astype(jnp.int32), fill


def _zero_fill(fill_ref, zbuf, xs_ref, zsem, n_blocks, wait):
    def copy(start_row, n_rows):
        cp = pltpu.make_async_copy(zbuf.at[pl.ds(0, n_rows * CHUNKS)],
                                   xs_ref.at[pl.ds(pl.multiple_of(start_row * CHUNKS, CHUNKS), n_rows * CHUNKS)],
                                   zsem)
        cp.wait() if wait else cp.start()

    for e in range(N_EXPERTS):
        start, length = fill_ref[e], fill_ref[N_EXPERTS + e]
        piece = MOE_BLOCK // 2
        while piece >= 1:
            @pl.when((length & piece) != 0)
            def _():
                copy(start + (length & (-2 * piece)), piece)
            piece //= 2

    def body(j, carry):
        for part in range(MOE_BLOCK // TM):
            copy(j * MOE_BLOCK + part * TM, TM)
        return carry

    lax.fori_loop(fill_ref[2 * N_EXPERTS], n_blocks, body, 0)


def _to_slabs(ref, base, x):
    n = x.shape[0]
    for c in range(CHUNKS):
        ref[pl.ds(base + c, n, stride=CHUNKS), :] = x[:, c * LANES:(c + 1) * LANES]


def _from_slabs(ref, base, n):
    return jnp.concatenate([ref[pl.ds(base + c, n, stride=CHUNKS), :] for c in range(CHUNKS)], axis=1)


def _dispatch_kernel(pos_ref, fill_ref, x_ref, g_ref, mod_ref, xs_ref, h_scr, zbuf, sem, zsem, *,
                     rows, nt, n_blocks):
    i = pl.program_id(0)
    slot = i % 2
    slab_rows = TM * CHUNKS

    @pl.when(i == 0)
    def _():
        zbuf[...] = jnp.zeros_like(zbuf)
        _zero_fill(fill_ref, zbuf, xs_ref, zsem, n_blocks, wait=False)

    def wait_slot(s):
        for _ in range(2):
            pltpu.make_async_copy(h_scr.at[pl.ds(pl.multiple_of(s * slab_rows, slab_rows), slab_rows)],
                                  xs_ref.at[pl.ds(0, slab_rows)], sem.at[s]).wait()

    @pl.when(i >= 2)
    def _():
        wait_slot(slot)

    h = _norm_mod(x_ref[...], g_ref[...], mod_ref[0, 4:5, :], mod_ref[0, 3:4, :])
    _to_slabs(h_scr, slot * slab_rows, h)
    base = i * TM

    def body(grp, carry):
        for u in range(SUBLANES):
            t = grp * SUBLANES + u
            src = h_scr.at[pl.ds(pl.multiple_of(slot * slab_rows + t * CHUNKS, CHUNKS), CHUNKS)]
            for k in range(2):
                p = pl.multiple_of(pos_ref[k * rows + base + t], CHUNKS)
                pltpu.make_async_copy(src, xs_ref.at[pl.ds(p, CHUNKS)], sem.at[slot]).start(priority=k)
        return carry

    lax.fori_loop(0, TM // SUBLANES, body, 0)

    @pl.when(i == nt - 1)
    def _():
        wait_slot(slot)
        if nt > 1:
            wait_slot(1 - slot)
        _zero_fill(fill_ref, zbuf, xs_ref, zsem, n_blocks, wait=True)


def _dispatch(pos, fill, x, g, mods, n_blocks, *, n_x_tiles):
    r, d = x.shape
    nt = r // TM
    assert MOE_BLOCK % TM == 0 and MOE_BLOCK // 2 <= TM and d == CHUNKS * LANES
    n_rows = n_blocks * MOE_BLOCK
    return pl.pallas_call(
        functools.partial(_dispatch_kernel, rows=r, nt=nt, n_blocks=n_blocks),
        out_shape=jax.ShapeDtypeStruct((n_rows * CHUNKS, LANES), F32),
        grid_spec=pltpu.PrefetchScalarGridSpec(
            num_scalar_prefetch=2, grid=(nt,),
            in_specs=[pl.BlockSpec((TM, d), lambda i, p, f: (i, 0)),
                      pl.BlockSpec((1, d), lambda i, p, f: (0, 0)),
                      pl.BlockSpec((1, 6, d), lambda i, p, f: (i // n_x_tiles, 0, 0))],
            out_specs=pl.BlockSpec(memory_space=pl.ANY),
            scratch_shapes=[pltpu.VMEM((2 * TM * CHUNKS, LANES), F32),
                            pltpu.VMEM((TM * CHUNKS, LANES), F32),
                            pltpu.SemaphoreType.DMA((2,)), pltpu.SemaphoreType.DMA]),
        compiler_params=_params("arbitrary"), name="moe_dispatch",
    )(pos, fill, x, g.reshape(1, d), mods)


def _expert_kernel(be_ref, nu_ref, xs_ref, wg_ref, wu_ref, wd_ref, ys_ref, wg_s, wu_s, wd_s):
    j = pl.program_id(0)

    @pl.when(j < nu_ref[0])
    def _():
        @pl.when(jnp.logical_or(j == 0, be_ref[j] != be_ref[jnp.maximum(j - 1, 0)]))
        def _():
            wg_s[...] = wg_ref[0, 0].astype(BF16)
            wu_s[...] = wu_ref[0, 0].astype(BF16)
            wd_s[...] = wd_ref[0, 0].astype(BF16)

        h = _from_slabs(xs_ref, 0, MOE_BLOCK).astype(BF16)
        a = (_silu(jnp.dot(h, wg_s[...], preferred_element_type=F32))
             * jnp.dot(h, wu_s[...], preferred_element_type=F32))
        _to_slabs(ys_ref, 0, jnp.dot(a.astype(BF16), wd_s[...], preferred_element_type=F32))

    @pl.when(j >= nu_ref[0])
    def _():
        ys_ref[...] = jnp.zeros_like(ys_ref)


def _experts(blk_e, n_used, xs, wg, wu, wd, layer):
    d, de = wg.shape[-2:]
    blk = MOE_BLOCK * CHUNKS
    last = lambda j, nu: jnp.minimum(j, jnp.maximum(nu[0] - 1, 0))
    return pl.pallas_call(
        _expert_kernel,
        out_shape=jax.ShapeDtypeStruct(xs.shape, F32),
        grid_spec=pltpu.PrefetchScalarGridSpec(
            num_scalar_prefetch=2, grid=(xs.shape[0] // blk,),
            in_specs=[pl.BlockSpec((blk, LANES), lambda j, be, nu: (last(j, nu), 0)),
                      pl.BlockSpec((1, 1, d, de), lambda j, be, nu: (layer, be[last(j, nu)], 0, 0)),
                      pl.BlockSpec((1, 1, d, de), lambda j, be, nu: (layer, be[last(j, nu)], 0, 0)),
                      pl.BlockSpec((1, 1, de, d), lambda j, be, nu: (layer, be[last(j, nu)], 0, 0))],
            out_specs=pl.BlockSpec((blk, LANES), lambda j, be, nu: (j, 0)),
            scratch_shapes=[pltpu.VMEM((d, de), BF16), pltpu.VMEM((d, de), BF16), pltpu.VMEM((de, d), BF16)]),
        compiler_params=_params("arbitrary"), name="moe_experts",
    )(blk_e, n_used, xs, wg, wu, wd)


def _combine_kernel(pos_ref, x_ref, mod_ref, w_ref, ys_ref, *rest, rows, nt, final):
    if final:
        gf_ref, o_ref, buf, sem = rest
    else:
        o_ref, buf, sem = rest
    i = pl.program_id(0)
    slot = i % 2
    slab_rows = TM * CHUNKS
    where = lambda s, k: (s * 2 + k) * slab_rows

    def issue(row_tile, s):
        base = row_tile * TM

        def body(grp, carry):
            for u in range(SUBLANES):
                t = grp * SUBLANES + u
                for k in range(2):
                    p = pl.multiple_of(pos_ref[k * rows + base + t], CHUNKS)
                    dst = buf.at[pl.ds(pl.multiple_of(where(s, k) + t * CHUNKS, CHUNKS), CHUNKS)]
                    pltpu.make_async_copy(ys_ref.at[pl.ds(p, CHUNKS)], dst, sem.at[s]).start(priority=k)
            return carry

        lax.fori_loop(0, TM // SUBLANES, body, 0)

    @pl.when(i == 0)
    def _():
        issue(0, 0)

    @pl.when(i + 1 < nt)
    def _():
        issue(i + 1, 1 - slot)

    for k in range(2):
        pltpu.make_async_copy(ys_ref.at[pl.ds(0, slab_rows)],
                              buf.at[pl.ds(pl.multiple_of(where(slot, k), slab_rows), slab_rows)],
                              sem.at[slot]).wait()
    w = w_ref[...]
    y = w[:, 0:1] * _from_slabs(buf, where(slot, 0), TM) + w[:, 1:2] * _from_slabs(buf, where(slot, 1), TM)
    x = x_ref[...] + mod_ref[0, 5:6, :] * y
    if final:
        x = x * lax.rsqrt(jnp.mean(x * x, axis=-1, keepdims=True) + EPS) * gf_ref[...]
    o_ref[...] = x


def _combine(pos, x, mods, w_cols, ys, *, n_x_tiles, g_final=None):
    r, d = x.shape
    final = g_final is not None
    nt = n_x_tiles if final else r // TM
    in_specs = [pl.BlockSpec((TM, d), lambda i, p: (i, 0)),
                pl.BlockSpec((1, 6, d), lambda i, p: (i // n_x_tiles, 0, 0)),
                pl.BlockSpec((TM, 2), lambda i, p: (i, 0)),
                pl.BlockSpec(memory_space=pl.ANY)]
    args = [pos, x, mods, w_cols, ys]
    if final:
        in_specs.append(pl.BlockSpec((1, d), lambda i, p: (0, 0)))
        args.append(g_final.reshape(1, d))
    return pl.pallas_call(
        functools.partial(_combine_kernel, rows=r, nt=nt, final=final),
        out_shape=jax.ShapeDtypeStruct((nt * TM, d), F32),
        grid_spec=pltpu.PrefetchScalarGridSpec(
            num_scalar_prefetch=1, grid=(nt,), in_specs=in_specs,
            out_specs=pl.BlockSpec((TM, d), lambda i, p: (i, 0)),
            scratch_shapes=[pltpu.VMEM((2 * 2 * TM * CHUNKS, LANES), F32), pltpu.SemaphoreType.DMA((2,))]),
        compiler_params=_params("arbitrary"), name="moe_combine",
    )(*args)


def _moe(x, g, mods, w_router_t, b_router, wg, wu, wd, layer, *, n_x_tiles, g_final=None, defer=False):
    r = x.shape[0]
    n_blocks = -(-2 * r // MOE_BLOCK) + N_EXPERTS
    e01, w01, rank01, counts = _router(x, g, mods, w_router_t, b_router, seq_len=n_x_tiles * TM)
    pos, blk_e, n_used, fill = _dispatch_plan(e01, rank01, counts, n_blocks)
    xs = _dispatch(pos, fill, x, g, mods, n_blocks, n_x_tiles=n_x_tiles)
    ys = _experts(blk_e, n_used, xs, wg, wu, wd, layer)
    if defer:
        return pos, mods, w01.T, ys
    return _combine(pos, x, mods, w01.T, ys, n_x_tiles=n_x_tiles, g_final=g_final)


def _rope_tables(seq_len, n_rows_total):
    half = HEAD_DIM // 2
    inv_freq = ROPE_BASE ** (-jnp.arange(0, half, 2, dtype=F32) / half)
    first = jnp.arange(half) < (half // 2)

    def tab(n):
        ang = jnp.arange(n, dtype=F32)[:, None] * inv_freq
        ang = jnp.concatenate([ang, ang], axis=-1)
        c, s = jnp.cos(ang), jnp.sin(ang)
        return jnp.stack([c, jnp.where(first, -s, 0.0), jnp.where(first, 0.0, s)], axis=1)

    ident = jnp.stack([jnp.ones((half,), F32), jnp.zeros((half,), F32), jnp.zeros((half,), F32)])
    reps = LANES // HEAD_DIM
    n_ctx_rows = (n_rows_total - seq_len) // GRID_W
    row = jnp.concatenate([tab(seq_len // GRID_W), jnp.broadcast_to(ident, (n_ctx_rows, 3, half))], axis=0)
    row = jnp.tile(jnp.concatenate([row, jnp.zeros_like(row)], axis=-1), (1, 1, reps))
    col = jnp.stack([tab(GRID_W), jnp.broadcast_to(ident, (GRID_W, 3, half))])
    col = jnp.tile(jnp.concatenate([jnp.zeros_like(col), col], axis=-1), (1, TM // GRID_W, 1, reps))
    return row, col.transpose(0, 2, 1, 3)


def kernel(x, c, ctx, c_ctx, w_mod, b_mod, g_mix, g_ffn, g_final, conv_w_in, conv_w, conv_w_out,
           gla_w_proj, gla_w_gk1, gla_w_gk2, gla_b_gk, gla_g_norm, gla_w_out, swa_w_qkv, swa_sinks,
           swa_w_out, na_w_qkv, na_rpb, na_w_out, router_w, router_b, moe_w_gate, moe_w_up, moe_w_down):
    seq_len, d = x.shape[1], x.shape[2]
    ctx_len = ctx.shape[1]
    assert x.shape[0] == 1 and ctx_len == TM and seq_len % TM == 0 and d == D_MODEL
    assert seq_len % GRID_W == 0 and seq_len // GRID_W >= NA_KH
    n_x_tiles = seq_len // TM
    rows = seq_len + ctx_len
    xs = jnp.concatenate([x[0], ctx[0]], axis=0)
    mods_all = _ada_params(c, c_ctx, w_mod, b_mod)
    w_router = jnp.pad(router_w.astype(F32), ((0, 0), (0, LANES - N_EXPERTS)))
    w_router_t = jnp.stack(_split2(w_router))
    n_mixers = 4

    pending = None
    for i in range(DEPTH):
        kind, j = i % n_mixers, i // n_mixers
        mods = mods_all[i]
        if kind == 0:
            if pending is not None:
                xs = _combine(pending[0], xs, *pending[1:], n_x_tiles=n_x_tiles)
            u = _project(xs, g_mix[i], mods, conv_w_in[j].astype(BF16), seq_len=seq_len)
            xs = _conv_mixer(xs, u, conv_w[j], mods, conv_w_out[j].astype(BF16), seq_len=seq_len)
        elif kind == 1:
            n_extra = LANES
            w_low = jnp.concatenate([gla_w_gk1[j, 0], gla_w_gk1[j, 1],
                                     jnp.zeros((d, n_extra - 2 * GLA_RANK), F32)], axis=1).astype(BF16)
            out = _project(xs, g_mix[i], mods, gla_w_proj[j].astype(BF16), seq_len=seq_len, w_extra=w_low,
                           pending=pending)
            if pending is not None:
                xs, out = out[0], out[1:]
            u, r_low = out
            w2pad = jnp.stack(
                [jnp.zeros((n_extra, GLA_DK), F32).at[k * GLA_RANK:(k + 1) * GLA_RANK].set(gla_w_gk2[j, k])
                 for k in range(2)]).astype(BF16)
            o_f, o_b = _gla_scan(u, r_low, w2pad, gla_b_gk[j])
            xs = _gla_out(xs, o_f, o_b, u, gla_g_norm[j], mods, gla_w_out[j].astype(BF16), seq_len=seq_len)
        elif kind == 2:
            if pending is not None:
                xs = _combine(pending[0], xs, *pending[1:], n_x_tiles=n_x_tiles)
            rope = _rope_tables(seq_len, rows)
            u = _project(xs, g_mix[i], mods, swa_w_qkv[j].astype(BF16), seq_len=seq_len,
                         q_cols=SWA_HEADS * HEAD_DIM, q_scale=HEAD_DIM ** -0.5 * LOG2E, rope=rope,
                         rope_cols=(SWA_HEADS + SWA_KV_HEADS) * HEAD_DIM)
            ypre = _swa_attention(u, swa_sinks[j], seq_len=seq_len)
            xs = _out_proj(xs, ypre, mods, swa_w_out[j].astype(BF16), seq_len=seq_len)
        else:
            out = _project(xs, g_mix[i], mods, na_w_qkv[j].astype(BF16), seq_len=seq_len,
                           q_cols=NA_HEADS * HEAD_DIM, q_scale=HEAD_DIM ** -0.5 * LOG2E, pending=pending)
            xs, u = out if pending is not None else (xs, out)
            ypre = _na_attention(u, _na_bias_table(na_rpb[j]), n_rows=seq_len // GRID_W)
            xs = _out_proj(xs, ypre, mods, na_w_out[j].astype(BF16), seq_len=seq_len)
        last = i == DEPTH - 1
        res = _moe(xs, g_ffn[i], mods, w_router_t, router_b, moe_w_gate, moe_w_up, moe_w_down, i,
                   n_x_tiles=n_x_tiles, g_final=g_final if last else None, defer=not last)
        xs, pending = (res, None) if last else (xs, res)
    return xs[None]
```

```python
import functools

import jax
import jax.numpy as jnp
from jax import lax
from jax.experimental import pallas as pl
from jax.experimental.pallas import tpu as pltpu

F32 = jnp.float32
BF16 = jnp.bfloat16
HIGHEST = lax.Precision.HIGHEST

D_MODEL = 1024
DEPTH = 4
GRID_W = 64
EPS = 1e-6
CONV_W = 3
GLA_HEADS = 4
GLA_DK = D_MODEL // 2
GLA_DV = D_MODEL
GLA_HK = GLA_DK // GLA_HEADS
GLA_HV = GLA_DV // GLA_HEADS
GLA_RANK = 16
GLA_NORMALIZER = 16.0
GLA_CHUNK = 64
HEAD_DIM = 64
SWA_HEADS = D_MODEL // HEAD_DIM
SWA_KV_HEADS = SWA_HEADS // 4
SWA_WINDOW = 128
ROPE_BASE = 10000.0
NA_HEADS = D_MODEL // HEAD_DIM
NA_KH = 8
NA_KW = 16
N_EXPERTS = 16
N_GROUPS = 4
EXPERTS_PER_GROUP = N_EXPERTS // N_GROUPS
D_EXPERT = D_MODEL // 2

TM = 256
MOE_BLOCK = 512
LANES = 128
SUBLANES = 8
CHUNKS = D_MODEL // LANES
NEG = -1e30
LOG2E = 1.4426950408889634
VMEM_LIMIT = 56 * 1024 * 1024

_NT = (((1,), (1,)), ((), ()))
_TN = (((0,), (0,)), ((), ()))


def _params(*sem):
    return pltpu.CompilerParams(dimension_semantics=sem, vmem_limit_bytes=VMEM_LIMIT)


def _norm_mod(x, g, scale, shift):
    ms = jnp.mean(x * x, axis=-1, keepdims=True)
    return (x * lax.rsqrt(ms + EPS) * g) * (1.0 + scale) + shift


def _silu(x):
    return x * jax.nn.sigmoid(x)


def _split2(x):
    hi = x.astype(BF16)
    return hi, (x - hi.astype(F32)).astype(BF16)


def _ada_kernel(cond_ref, w_ref, b_ref, o_ref):
    s_hi, s_lo = _split2(_silu(cond_ref[...]))
    w_hi, w_lo = _split2(w_ref[0])
    o_ref[0] = (jnp.dot(s_hi, w_hi, preferred_element_type=F32) + jnp.dot(s_hi, w_lo, preferred_element_type=F32)
                + jnp.dot(s_lo, w_hi, preferred_element_type=F32) + b_ref[0])


def _ada_params(c, c_ctx, w_mod, b_mod):
    depth, d, n = w_mod.shape
    nc = n // 4
    cond = jnp.zeros((8, d), F32).at[0].set(c[0]).at[1].set(c_ctx)
    out = pl.pallas_call(
        _ada_kernel,
        out_shape=jax.ShapeDtypeStruct((depth, 8, n), F32),
        grid=(depth, n // nc),
        in_specs=[pl.BlockSpec((8, d), lambda l, j: (0, 0)),
                  pl.BlockSpec((1, d, nc), lambda l, j: (l, 0, j)),
                  pl.BlockSpec((1, 1, nc), lambda l, j: (l, 0, j))],
        out_specs=pl.BlockSpec((1, 8, nc), lambda l, j: (l, 0, j)),
        compiler_params=_params("parallel", "parallel"),
        name="ada_params",
    )(cond, w_mod, b_mod.reshape(depth, 1, n))
    return out[:, :2].reshape(depth, 2, 6, d)


def _row_tile(rows, cap):
    return next(t for t in (1280, 640, TM) if t <= cap and rows % t == 0)


def _mod_rows(mod_ref, tb, seq_len):
    row = pl.program_id(0) * tb + lax.broadcasted_iota(jnp.int32, (tb, 1), 0)
    is_ctx = row >= seq_len
    return lambda j: jnp.where(is_ctx, mod_ref[1, j:j + 1, :], mod_ref[0, j:j + 1, :])


def _proj_kernel(*refs, tb, seq_len, n_rows, n_chunk, q_cols, q_scale, rope_cols, has_extra, fused):
    if fused:
        pos_ref, refs = refs[0], refs[1:]
    x_ref, g_ref, mod_ref, w_ref = refs[:4]
    rest = refs[4:]
    n_main = w_ref.shape[1]
    if has_extra:
        wx_ref, rest = rest[0], rest[1:]
    if rope_cols:
        row_ref, col_ref = rest[:2]
        rest = rest[2:]
        rpt = TM // GRID_W
        cos_t, sa_t, sb_t = [
            jnp.concatenate([jnp.broadcast_to(row_ref[a, k:k + 1, :], (GRID_W, LANES)) for a in range(rpt)],
                            axis=0) + col_ref[0, k]
            for k in range(3)]
    if fused:
        modp_ref, wc_ref, ys_ref = rest[:3]
        rest = rest[3:]
    o_ref = rest[0]
    if fused:
        xo_ref, buf, sem = rest[-3:]
        i = pl.program_id(0)
        nt = pl.num_programs(0)
        slot = i % 2
        slab_rows = TM * CHUNKS
        where = lambda s, k: (s * 2 + k) * slab_rows

        def start(t, base, s):
            for k in range(2):
                p = pl.multiple_of(pos_ref[k * n_rows + base + t], CHUNKS)
                dst = buf.at[pl.ds(pl.multiple_of(where(s, k) + t * CHUNKS, CHUNKS), CHUNKS)]
                pltpu.make_async_copy(ys_ref.at[pl.ds(p, CHUNKS)], dst, sem.at[s]).start(priority=k)

        def wait(s):
            for k in range(2):
                pltpu.make_async_copy(ys_ref.at[pl.ds(0, slab_rows)],
                                      buf.at[pl.ds(pl.multiple_of(where(s, k), slab_rows), slab_rows)],
                                      sem.at[s]).wait()

        @pl.when(i == 0)
        def _():
            def body(t, carry):
                start(t, 0, 0)
                return carry
            lax.fori_loop(0, TM, body, 0)

        wait(slot)
        wc = wc_ref[...]
        y = wc[:, 0:1] * _from_slabs(buf, where(slot, 0), TM) + wc[:, 1:2] * _from_slabs(buf, where(slot, 1), TM)
        x = x_ref[...] + _mod_rows(modp_ref, tb, seq_len)(5) * y
        xo_ref[...] = x
        nxt = jnp.where(i + 1 < nt, i + 1, 0) * TM
        for t in range(TM):
            start(t, nxt, 1 - slot)
    else:
        x = x_ref[...]
    pick = _mod_rows(mod_ref, tb, seq_len)
    h = _norm_mod(x, g_ref[...], pick(1), pick(0)).astype(BF16)
    for c0 in range(0, n_main, n_chunk):
        c1 = min(c0 + n_chunk, n_main)
        y = jnp.dot(h, w_ref[:, c0:c1], preferred_element_type=F32)
        if c0 < q_cols:
            y = y * q_scale
        if c0 >= rope_cols:
            o_ref[:, c0:c1] = y.astype(o_ref.dtype)
            continue
        for s0 in range(c0, c1, LANES):
            ys = y[:, s0 - c0:s0 - c0 + LANES]
            if s0 < rope_cols:
                ys = ys * cos_t + pltpu.roll(ys, LANES - 16, 1) * sa_t + pltpu.roll(ys, 16, 1) * sb_t
            o_ref[:, s0:s0 + LANES] = ys.astype(o_ref.dtype)
    if has_extra:
        rest[1][...] = jnp.dot(h, wx_ref[...], preferred_element_type=F32)
    if fused:
        @pl.when(i == nt - 1)
        def _():
            wait(1 - slot)


def _project(x, g, mods, w, *, seq_len, n_chunk=512, q_cols=0, q_scale=1.0, rope=None, rope_cols=0,
             w_extra=None, pending=None):
    r, d = x.shape
    n_main = w.shape[1]
    n_extra = 0 if w_extra is None else w_extra.shape[1]
    fused = pending is not None
    tb = TM if (rope_cols or fused) else _row_tile(r, 1280)
    im = lambda f: (lambda i, *_: f(i))
    in_specs = [pl.BlockSpec((tb, d), im(lambda i: (i, 0))),
                pl.BlockSpec((1, d), im(lambda i: (0, 0))),
                pl.BlockSpec((2, 6, d), im(lambda i: (0, 0, 0))),
                pl.BlockSpec((d, n_main), im(lambda i: (0, 0)))]
    args = [x, g.reshape(1, d), mods, w]
    if n_extra:
        in_specs.append(pl.BlockSpec((d, n_extra), im(lambda i: (0, 0))))
        args.append(w_extra)
    if rope_cols:
        rpt = TM // GRID_W
        in_specs += [pl.BlockSpec((rpt, 3, LANES), im(lambda i: (i, 0, 0))),
                     pl.BlockSpec((1, 3, TM, LANES), im(lambda i: (i // (seq_len // TM), 0, 0, 0)))]
        args += list(rope)
    out_shape = [jax.ShapeDtypeStruct((r, n_main), BF16)]
    out_specs = [pl.BlockSpec((tb, n_main), im(lambda i: (i, 0)))]
    if n_extra:
        out_shape.append(jax.ShapeDtypeStruct((r, n_extra), F32))
        out_specs.append(pl.BlockSpec((tb, n_extra), im(lambda i: (i, 0))))
    scratch = []
    if fused:
        pos, mods_prev, w_cols, ys = pending
        in_specs += [pl.BlockSpec((2, 6, d), im(lambda i: (0, 0, 0))),
                     pl.BlockSpec((tb, 2), im(lambda i: (i, 0))),
                     pl.BlockSpec(memory_space=pl.ANY)]
        args = [pos] + args + [mods_prev, w_cols, ys]
        out_shape.append(jax.ShapeDtypeStruct((r, d), F32))
        out_specs.append(pl.BlockSpec((tb, d), im(lambda i: (i, 0))))
        scratch = [pltpu.VMEM((2 * 2 * TM * CHUNKS, LANES), F32), pltpu.SemaphoreType.DMA((2,))]
    out = pl.pallas_call(
        functools.partial(_proj_kernel, tb=tb, seq_len=seq_len, n_rows=r, n_chunk=n_chunk, q_cols=q_cols,
                          q_scale=q_scale, rope_cols=rope_cols, has_extra=bool(n_extra), fused=fused),
        out_shape=out_shape,
        grid_spec=pltpu.PrefetchScalarGridSpec(
            num_scalar_prefetch=int(fused), grid=(r // tb,), in_specs=in_specs, out_specs=out_specs,
            scratch_shapes=scratch),
        compiler_params=_params("arbitrary" if fused else "parallel"),
        name="combine_norm_proj" if fused else "norm_proj",
    )(*args)
    if fused:
        return (out[-1],) + tuple(out[:-1])
    return out if n_extra else out[0]


def _out_proj_kernel(x_ref, y_ref, mod_ref, w_ref, o_ref, *, tb, seq_len):
    y = jnp.dot(y_ref[...], w_ref[...], preferred_element_type=F32)
    o_ref[...] = x_ref[...] + _mod_rows(mod_ref, tb, seq_len)(2) * y


def _out_proj(x, ypre, mods, w, *, seq_len):
    r, d = x.shape
    tb = _row_tile(r, 1280)
    return pl.pallas_call(
        functools.partial(_out_proj_kernel, tb=tb, seq_len=seq_len),
        out_shape=jax.ShapeDtypeStruct((r, d), F32),
        grid=(r // tb,),
        in_specs=[pl.BlockSpec((tb, d), lambda i: (i, 0)),
                  pl.BlockSpec((tb, d), lambda i: (i, 0)),
                  pl.BlockSpec((2, 6, d), lambda i: (0, 0, 0)),
                  pl.BlockSpec((d, d), lambda i: (0, 0))],
        out_specs=pl.BlockSpec((tb, d), lambda i: (i, 0)),
        compiler_params=_params("parallel"), name="out_proj",
    )(x, ypre, mods, w)


def _conv_kernel(x_ref, bg_ref, cg_ref, v_ref, cgp_ref, vp_ref, cgn_ref, vn_ref, cw_ref, mod_ref, w_ref,
                 o_ref, *, tb, seq_len, n_rows):
    z = cg_ref[...].astype(F32) * v_ref[...].astype(F32)
    z_prev = cgp_ref[SUBLANES - 1:SUBLANES, :].astype(F32) * vp_ref[SUBLANES - 1:SUBLANES, :].astype(F32)
    z_next = cgn_ref[0:1, :].astype(F32) * vn_ref[0:1, :].astype(F32)
    row = lax.broadcasted_iota(jnp.int32, z.shape, 0)
    pos = pl.program_id(0) * tb + row
    z_dn = jnp.where(row == 0, z_prev, pltpu.roll(z, 1, 0))
    z_up = jnp.where(row == tb - 1, z_next, pltpu.roll(z, tb - 1, 0))
    z_dn = jnp.where(jnp.logical_or(pos == 0, pos == seq_len), 0.0, z_dn)
    z_up = jnp.where(jnp.logical_or(pos == seq_len - 1, pos == n_rows - 1), 0.0, z_up)
    conv = z_dn * cw_ref[0:1, :] + z * cw_ref[1:2, :] + z_up * cw_ref[2:3, :]
    ypre = (bg_ref[...].astype(F32) * conv).astype(BF16)
    y = jnp.dot(ypre, w_ref[...], preferred_element_type=F32)
    o_ref[...] = x_ref[...] + _mod_rows(mod_ref, tb, seq_len)(2) * y


def _conv_mixer(x, u, conv_w, mods, w_out, *, seq_len):
    r, d = x.shape
    tb = _row_tile(r, 640)
    n8 = r // SUBLANES
    t8 = tb // SUBLANES
    tile = lambda c: pl.BlockSpec((tb, d), lambda i: (i, c))
    prev = lambda c: pl.BlockSpec((SUBLANES, d), lambda i: (jnp.maximum(i * t8 - 1, 0), c))
    nxt = lambda c: pl.BlockSpec((SUBLANES, d), lambda i: (jnp.minimum((i + 1) * t8, n8 - 1), c))
    return pl.pallas_call(
        functools.partial(_conv_kernel, tb=tb, seq_len=seq_len, n_rows=r),
        out_shape=jax.ShapeDtypeStruct((r, d), F32),
        grid=(r // tb,),
        in_specs=[pl.BlockSpec((tb, d), lambda i: (i, 0)), tile(0), tile(1), tile(2),
                  prev(1), prev(2), nxt(1), nxt(2),
                  pl.BlockSpec((CONV_W, d), lambda i: (0, 0)),
                  pl.BlockSpec((2, 6, d), lambda i: (0, 0, 0)),
                  pl.BlockSpec((d, d), lambda i: (0, 0))],
        out_specs=pl.BlockSpec((tb, d), lambda i: (i, 0)),
        compiler_params=_params("parallel"), name="conv_mixer",
    )(x, u, u, u, u, u, u, u, conv_w, mods, w_out)


def _log_sigmoid(z):
    return jnp.minimum(z, 0.0) - jnp.log1p(jnp.exp(-jnp.abs(z)))


def _gla_chunk(q_ref, k_ref, v_ref, gk, o_ref, st_ref, d, c, reverse):
    C = GLA_CHUNK
    rows = slice(c * C, (c + 1) * C)
    ii = lax.broadcasted_iota(jnp.int32, (C, C), 0)
    jj = lax.broadcasted_iota(jnp.int32, (C, C), 1)
    causal = (jj >= ii) if reverse else (jj <= ii)
    ones = jnp.where(causal, 1.0, 0.0).astype(BF16)
    g_hi, g_lo = _split2(gk[rows])
    bcum = (jnp.dot(ones, g_hi, preferred_element_type=F32)
            + jnp.dot(ones, g_lo, preferred_element_type=F32))
    total = bcum[0:1, :] if reverse else bcum[C - 1:C, :]
    q = q_ref[rows, :].astype(F32) * (GLA_HK ** -0.5)
    k = k_ref[rows, :].astype(F32)
    q_dec = (q * jnp.exp(bcum)).astype(BF16)
    k_inv = (k * jnp.exp(-bcum)).astype(BF16)
    k_end = (k * jnp.exp(total - bcum)).astype(BF16)
    decay = jnp.exp(total)
    for h in range(GLA_HEADS):
        kcols = slice(h * GLA_HK, (h + 1) * GLA_HK)
        vcols = slice(h * GLA_HV, (h + 1) * GLA_HV)
        vh = v_ref[rows, vcols]
        att = lax.dot_general(q_dec[:, kcols], k_inv[:, kcols], _NT, preferred_element_type=F32)
        att = jnp.where(causal, att, 0.0).astype(BF16)
        st = st_ref[d, h]
        o = (jnp.dot(att, vh, preferred_element_type=F32)
             + lax.dot_general(q_dec[:, kcols], st.astype(BF16), _NT, preferred_element_type=F32))
        st_ref[d, h] = st * decay[:, kcols] + lax.dot_general(vh, k_end[:, kcols], _TN,
                                                              preferred_element_type=F32)
        o_ref[rows, vcols] = o.astype(o_ref.dtype)


def _gla_kernel(qf_ref, kf_ref, vf_ref, rf_ref, qb_ref, kb_ref, vb_ref, rb_ref, w2_ref, b_ref,
                of_ref, ob_ref, st_ref):
    @pl.when(pl.program_id(0) == 0)
    def _():
        st_ref[...] = jnp.zeros_like(st_ref)

    def gates(r_ref, d):
        z = jnp.dot(r_ref[...].astype(BF16), w2_ref[d], preferred_element_type=F32) + b_ref[d]
        return _log_sigmoid(z) * (1.0 / GLA_NORMALIZER)

    gk_f, gk_b = gates(rf_ref, 0), gates(rb_ref, 1)
    n_chunks = TM // GLA_CHUNK
    for c in range(n_chunks):
        _gla_chunk(qf_ref, kf_ref, vf_ref, gk_f, of_ref, st_ref, 0, c, False)
        _gla_chunk(qb_ref, kb_ref, vb_ref, gk_b, ob_ref, st_ref, 1, n_chunks - 1 - c, True)


def _gla_scan(u, r_low, w2pad, b_gk):
    rws = u.shape[0]
    nt = rws // TM
    fwd = lambda s: (s + nt - 1) % nt
    bwd = lambda s: nt - 1 - s
    qkvr = lambda order: [pl.BlockSpec((TM, GLA_DK), lambda s: (order(s), 0)),
                          pl.BlockSpec((TM, GLA_DK), lambda s: (order(s), 1)),
                          pl.BlockSpec((TM, GLA_DV), lambda s: (order(s), 1)),
                          pl.BlockSpec((TM, LANES), lambda s: (order(s), 0))]
    out = jax.ShapeDtypeStruct((rws, GLA_DV), BF16)
    return pl.pallas_call(
        _gla_kernel,
        out_shape=[out, out],
        grid=(nt,),
        in_specs=qkvr(fwd) + qkvr(bwd) + [pl.BlockSpec((2, LANES, GLA_DK), lambda s: (0, 0, 0)),
                                          pl.BlockSpec((2, 1, GLA_DK), lambda s: (0, 0, 0))],
        out_specs=[pl.BlockSpec((TM, GLA_DV), lambda s: (fwd(s), 0)),
                   pl.BlockSpec((TM, GLA_DV), lambda s: (bwd(s), 0))],
        scratch_shapes=[pltpu.VMEM((2, GLA_HEADS, GLA_HV, GLA_HK), F32)],
        compiler_params=_params("arbitrary"), name="gla_scan",
    )(u, u, u, r_low, u, u, u, r_low, w2pad, b_gk.reshape(2, 1, GLA_DK))


def _gla_out_kernel(x_ref, of_ref, ob_ref, g_ref, gn_ref, mod_ref, w_ref, o_ref, *, tb, seq_len):
    o = of_ref[...].astype(F32) + ob_ref[...].astype(F32)
    parts = []
    for h in range(GLA_HEADS):
        oh = o[:, h * GLA_HV:(h + 1) * GLA_HV]
        parts.append(oh * lax.rsqrt(jnp.mean(oh * oh, axis=-1, keepdims=True) + EPS) * gn_ref[...])
    ypre = (jnp.concatenate(parts, axis=1) * _silu(g_ref[...].astype(F32))).astype(BF16)
    y = jnp.dot(ypre, w_ref[...], preferred_element_type=F32)
    o_ref[...] = x_ref[...] + _mod_rows(mod_ref, tb, seq_len)(2) * y


def _gla_out(x, o_f, o_b, u, g_norm, mods, w, *, seq_len):
    r, d = x.shape
    tb = _row_tile(r, 640)
    tile = lambda c: pl.BlockSpec((tb, d), lambda i: (i, c))
    return pl.pallas_call(
        functools.partial(_gla_out_kernel, tb=tb, seq_len=seq_len),
        out_shape=jax.ShapeDtypeStruct((r, d), F32),
        grid=(r // tb,),
        in_specs=[tile(0), tile(0), tile(0), tile(2),
                  pl.BlockSpec((1, GLA_HV), lambda i: (0, 0)),
                  pl.BlockSpec((2, 6, d), lambda i: (0, 0, 0)),
                  pl.BlockSpec((d, d), lambda i: (0, 0))],
        out_specs=tile(0),
        compiler_params=_params("parallel"), name="gla_out",
    )(x, o_f, o_b, u, g_norm.reshape(1, GLA_HV), mods, w)


def _attn_out(x_ref, mod_ref, w_ref, y_scr, o_ref, seq_len):
    y = jnp.dot(y_scr[...], w_ref[...], preferred_element_type=F32)
    o_ref[...] = x_ref[...] + _mod_rows(mod_ref, TM, seq_len)(2) * y


def _swa_kernel(sink_ref, q_ref, kvp_ref, kvo_ref, kvn_ref, kvc_ref, x_ref, mod_ref, w_ref, o_ref, kv_buf, y_scr,
                *, seq_len):
    i = pl.program_id(0)
    half = TM // 2
    kv_buf[0:half] = kvp_ref[...]
    kv_buf[half:half + TM] = kvo_ref[...]
    kv_buf[half + TM:2 * TM] = kvn_ref[...]
    kv_buf[2 * TM:3 * TM] = kvc_ref[...]
    nk = 3 * TM
    qpos = i * TM + lax.broadcasted_iota(jnp.int32, (TM, nk), 0)
    col = lax.broadcasted_iota(jnp.int32, (TM, nk), 1)
    kpos = i * TM - half + col
    local = ((jnp.abs(kpos - qpos) <= SWA_WINDOW) & (kpos >= 0) & (kpos < seq_len) & (qpos < seq_len))
    valid = local | (col >= 2 * TM)
    kv_w = SWA_KV_HEADS * HEAD_DIM
    group = SWA_HEADS // SWA_KV_HEADS
    for kv in range(SWA_KV_HEADS):
        heads = range(kv * group, (kv + 1) * group)
        qg = jnp.concatenate([q_ref[:, h * HEAD_DIM:(h + 1) * HEAD_DIM] for h in heads], axis=0)
        kh = kv_buf[:, kv * HEAD_DIM:(kv + 1) * HEAD_DIM]
        vh = kv_buf[:, kv_w + kv * HEAD_DIM:kv_w + (kv + 1) * HEAD_DIM]
        s = lax.dot_general(qg, kh, _NT, preferred_element_type=F32)
        ps, ls = [], []
        for j, h in enumerate(heads):
            sj = jnp.where(valid, s[j * TM:(j + 1) * TM], NEG)
            sink = sink_ref[h] * LOG2E
            m = jnp.maximum(jnp.max(sj, axis=-1, keepdims=True), sink)
            p = jnp.exp2(sj - m)
            ls.append(jnp.sum(p, axis=-1, keepdims=True) + jnp.exp2(sink - m))
            ps.append(p.astype(BF16))
        o = jnp.dot(jnp.concatenate(ps, axis=0), vh, preferred_element_type=F32)
        for j, h in enumerate(heads):
            y_scr[:, h * HEAD_DIM:(h + 1) * HEAD_DIM] = (o[j * TM:(j + 1) * TM] / ls[j]).astype(y_scr.dtype)
    _attn_out(x_ref, mod_ref, w_ref, y_scr, o_ref, seq_len)


def _swa_attention(u, sinks, x, mods, w_out, *, seq_len):
    rws = u.shape[0]
    nt = rws // TM
    half = TM // 2
    n_half = rws // half
    kv_w = 2 * SWA_KV_HEADS * HEAD_DIM
    qw = SWA_HEADS * HEAD_DIM
    kvc = qw // kv_w
    return pl.pallas_call(
        functools.partial(_swa_kernel, seq_len=seq_len),
        out_shape=jax.ShapeDtypeStruct((rws, qw), F32),
        grid=(nt,),
        in_specs=[pl.BlockSpec(memory_space=pltpu.SMEM),
                  pl.BlockSpec((TM, qw), lambda i: (i, 0)),
                  pl.BlockSpec((half, kv_w), lambda i: (jnp.maximum(2 * i - 1, 0), kvc)),
                  pl.BlockSpec((TM, kv_w), lambda i: (i, kvc)),
                  pl.BlockSpec((half, kv_w), lambda i: (jnp.minimum(2 * i + 2, n_half - 1), kvc)),
                  pl.BlockSpec((TM, kv_w), lambda i: (nt - 1, kvc)),
                  pl.BlockSpec((TM, qw), lambda i: (i, 0)),
                  pl.BlockSpec((2, 6, qw), lambda i: (0, 0, 0)),
                  pl.BlockSpec((qw, qw), lambda i: (0, 0))],
        out_specs=pl.BlockSpec((TM, qw), lambda i: (i, 0)),
        scratch_shapes=[pltpu.VMEM((3 * TM, kv_w), BF16), pltpu.VMEM((TM, qw), BF16)],
        compiler_params=_params("parallel"), name="swa_attention",
    )(sinks, u, u, u, u, u, x, mods, w_out)


def _na_kernel(q_ref, kp_ref, ko_ref, kn_ref, kc_ref, vp_ref, vo_ref, vn_ref, vc_ref, bias_ref, o_ref,
               k_buf, v_buf, *, n_rows):
    i = pl.program_id(0)
    for j, (kr, vr) in enumerate(((kp_ref, vp_ref), (ko_ref, vo_ref), (kn_ref, vn_ref), (kc_ref, vc_ref))):
        k_buf[j * TM:(j + 1) * TM] = kr[...]
        v_buf[j * TM:(j + 1) * TM] = vr[...]
    n_loc = 3 * TM
    rpt = TM // GRID_W
    qi = lax.broadcasted_iota(jnp.int32, (TM, n_loc), 0)
    ki = lax.broadcasted_iota(jnp.int32, (TM, n_loc), 1)
    r = i * rpt + qi // GRID_W
    c = qi % GRID_W
    krow = (i - 1) * rpt + ki // GRID_W
    kcol = ki % GRID_W
    r0 = jnp.clip(r - NA_KH // 2, 0, n_rows - NA_KH)
    c0 = jnp.clip(c - NA_KW // 2, 0, GRID_W - NA_KW)
    valid = ((krow >= r0) & (krow < r0 + NA_KH) & (kcol >= c0) & (kcol < c0 + NA_KW) & (r < n_rows))
    for h in range(NA_HEADS):
        cols = slice(h * HEAD_DIM, (h + 1) * HEAD_DIM)
        qh = q_ref[:, cols]
        s = lax.dot_general(qh, k_buf[:, cols], _NT, preferred_element_type=F32)
        s_loc = jnp.where(valid, s[:, :n_loc] + bias_ref[h], NEG)
        s_ctx = s[:, n_loc:]
        m = jnp.maximum(jnp.max(s_loc, axis=-1, keepdims=True), jnp.max(s_ctx, axis=-1, keepdims=True))
        p_loc = jnp.exp2(s_loc - m)
        p_ctx = jnp.exp2(s_ctx - m)
        l = jnp.sum(p_loc, axis=-1, keepdims=True) + jnp.sum(p_ctx, axis=-1, keepdims=True)
        o = (jnp.dot(p_loc.astype(BF16), v_buf[0:n_loc, cols], preferred_element_type=F32)
             + jnp.dot(p_ctx.astype(BF16), v_buf[n_loc:, cols], preferred_element_type=F32)) / l
        o_ref[:, cols] = o.astype(o_ref.dtype)


def _na_bias_kernel(t_ref, o_ref, *, lo):
    rpt = TM // GRID_W
    for a in range(rpt):
        for b in range(3 * rpt):
            o_ref[0, a * GRID_W:(a + 1) * GRID_W, b * GRID_W:(b + 1) * GRID_W] = t_ref[0, lo - a + b]


def _na_bias_table(rpb):
    n_heads = rpb.shape[0]
    rpt = TM // GRID_W
    lo = NA_KH - 1 - rpt
    assert lo - (rpt - 1) >= 0 and lo + 3 * rpt <= 2 * NA_KH - 1
    c = jnp.arange(GRID_W)
    col_off = c[None, :] - c[:, None] + NA_KW - 1
    onehot = (col_off[None] == jnp.arange(2 * NA_KW - 1)[:, None, None]).astype(F32)
    tcol = jnp.einsum('hrj,jck->hrck', rpb.astype(F32) * LOG2E, onehot, precision=HIGHEST)
    return pl.pallas_call(
        functools.partial(_na_bias_kernel, lo=lo),
        out_shape=jax.ShapeDtypeStruct((n_heads, TM, 3 * TM), F32),
        grid=(n_heads,),
        in_specs=[pl.BlockSpec((1, 2 * NA_KH - 1, GRID_W, GRID_W), lambda h: (h, 0, 0, 0))],
        out_specs=pl.BlockSpec((1, TM, 3 * TM), lambda h: (h, 0, 0)),
        compiler_params=_params("parallel"), name="na_bias_table",
    )(tcol)


def _na_attention(u, bias, *, n_rows):
    rws = u.shape[0]
    nt = rws // TM
    d = NA_HEADS * HEAD_DIM
    blk = lambda f, c: pl.BlockSpec((TM, d), lambda i: (f(i), c))
    prev = lambda i: jnp.maximum(i - 1, 0)
    own = lambda i: i
    nxt = lambda i: jnp.minimum(i + 1, nt - 1)
    ctx = lambda i: nt - 1
    return pl.pallas_call(
        functools.partial(_na_kernel, n_rows=n_rows),
        out_shape=jax.ShapeDtypeStruct((rws, d), BF16),
        grid=(nt,),
        in_specs=[blk(own, 0), blk(prev, 1), blk(own, 1), blk(nxt, 1), blk(ctx, 1),
                  blk(prev, 2), blk(own, 2), blk(nxt, 2), blk(ctx, 2),
                  pl.BlockSpec((NA_HEADS, TM, 3 * TM), lambda i: (0, 0, 0))],
        out_specs=pl.BlockSpec((TM, d), lambda i: (i, 0)),
        scratch_shapes=[pltpu.VMEM((4 * TM, d), BF16), pltpu.VMEM((4 * TM, d), BF16)],
        compiler_params=_params("parallel"), name="na_attention",
    )(u, u, u, u, u, u, u, u, u, bias)


def _router_kernel(x_ref, g_ref, mod_ref, wr_ref, br_ref, e_ref, w_ref, rank_ref, cnt_ref, carry, *,
                   tb, seq_len):
    @pl.when(pl.program_id(0) == 0)
    def _():
        carry[...] = jnp.zeros_like(carry)

    pick = _mod_rows(mod_ref, tb, seq_len)
    h = _norm_mod(x_ref[...], g_ref[...], pick(4), pick(3))
    h_hi, h_lo = _split2(h)
    lt = (jnp.dot(h_hi, wr_ref[0], preferred_element_type=F32) + jnp.dot(h_hi, wr_ref[1], preferred_element_type=F32)
          + jnp.dot(h_lo, wr_ref[0], preferred_element_type=F32))
    logits = jnp.transpose(lt)[:N_EXPERTS]
    scores = jax.nn.sigmoid(logits)
    biased = scores + br_ref[...]
    row = lambda a, e: a[e:e + 1, :]
    best_g = jnp.zeros((1, tb), jnp.int32)
    best_s = None
    for g in range(N_GROUPS):
        v = [row(biased, g * EXPERTS_PER_GROUP + j) for j in range(EXPERTS_PER_GROUP)]
        gs = None
        for a in range(EXPERTS_PER_GROUP):
            for b in range(a + 1, EXPERTS_PER_GROUP):
                pair = v[a] + v[b]
                gs = pair if gs is None else jnp.maximum(gs, pair)
        if best_s is None:
            best_s = gs
        else:
            better = gs > best_s
            best_g = jnp.where(better, g, best_g)
            best_s = jnp.where(better, gs, best_s)
    picks = []
    for _ in range(2):
        top_v = jnp.full((1, tb), -jnp.inf, F32)
        top_i = jnp.full((1, tb), -1, jnp.int32)
        for e in range(N_EXPERTS):
            ok = best_g == (e // EXPERTS_PER_GROUP)
            for p in picks:
                ok = jnp.logical_and(ok, p != e)
            cand = jnp.where(ok, row(biased, e), -jnp.inf)
            better = cand > top_v
            top_i = jnp.where(better, e, top_i)
            top_v = jnp.where(better, cand, top_v)
        picks.append(top_i)
    e_iota = lax.broadcasted_iota(jnp.int32, (N_EXPERTS, tb), 0)
    sel0 = e_iota == picks[0]
    sel1 = e_iota == picks[1]
    w0 = jnp.sum(jnp.where(sel0, scores, 0.0), axis=0, keepdims=True)
    w1 = jnp.sum(jnp.where(sel1, scores, 0.0), axis=0, keepdims=True)
    tot = w0 + w1
    e_ref[0:1, :] = picks[0]
    e_ref[1:2, :] = picks[1]
    w_ref[0:1, :] = w0 / tot
    w_ref[1:2, :] = w1 / tot
    member = jnp.where(jnp.logical_or(sel0, sel1), 1.0, 0.0)
    earlier = (lax.broadcasted_iota(jnp.int32, (TM, TM), 0)
               < lax.broadcasted_iota(jnp.int32, (TM, TM), 1)).astype(BF16)
    seen = carry[:, 0:1]
    before = []
    for c in range(tb // TM):
        part = member[:, c * TM:(c + 1) * TM]
        before.append(seen + jnp.dot(part.astype(BF16), earlier, preferred_element_type=F32))
        seen = seen + jnp.sum(part, axis=1, keepdims=True)
    before = jnp.concatenate(before, axis=1)
    rank_ref[0:1, :] = jnp.sum(jnp.where(sel0, before, 0.0), axis=0, keepdims=True).astype(jnp.int32)
    rank_ref[1:2, :] = jnp.sum(jnp.where(sel1, before, 0.0), axis=0, keepdims=True).astype(jnp.int32)
    carry[...] = jnp.broadcast_to(seen, carry.shape)
    cnt_ref[...] = carry[...].astype(jnp.int32)


def _router(x, g, mods, w_router_t, b_router, *, seq_len):
    r, d = x.shape
    tb = _row_tile(r, 1280)
    pair = pl.BlockSpec((2, tb), lambda i: (0, i))
    return pl.pallas_call(
        functools.partial(_router_kernel, tb=tb, seq_len=seq_len),
        out_shape=[jax.ShapeDtypeStruct((2, r), jnp.int32), jax.ShapeDtypeStruct((2, r), F32),
                   jax.ShapeDtypeStruct((2, r), jnp.int32), jax.ShapeDtypeStruct((N_EXPERTS, LANES), jnp.int32)],
        grid=(r // tb,),
        in_specs=[pl.BlockSpec((tb, d), lambda i: (i, 0)),
                  pl.BlockSpec((1, d), lambda i: (0, 0)),
                  pl.BlockSpec((2, 6, d), lambda i: (0, 0, 0)),
                  pl.BlockSpec((2, d, LANES), lambda i: (0, 0, 0)),
                  pl.BlockSpec((N_EXPERTS, 1), lambda i: (0, 0))],
        out_specs=[pair, pair, pair, pl.BlockSpec((N_EXPERTS, LANES), lambda i: (0, 0))],
        scratch_shapes=[pltpu.VMEM((N_EXPERTS, LANES), F32)],
        compiler_params=_params("arbitrary"), name="router",
    )(x, g.reshape(1, d), mods, w_router_t, b_router.reshape(N_EXPERTS, 1))


def _dispatch_plan(e01, rank01, counts, n_blocks):
    cnt = counts[:, 0]
    nblk = (cnt + MOE_BLOCK - 1) // MOE_BLOCK
    blk_end = jnp.cumsum(nblk)
    off = (blk_end - nblk) * MOE_BLOCK
    eidx = jnp.arange(N_EXPERTS, dtype=jnp.int32)[:, None, None]
    pos = rank01 + jnp.sum(jnp.where(e01[None] == eidx, off[:, None, None], 0), axis=0)
    blk_e = jnp.sum(jnp.arange(n_blocks, dtype=jnp.int32)[:, None] >= blk_end[None, :], axis=1)
    blk_e = jnp.minimum(blk_e, N_EXPERTS - 1).astype(jnp.int32)
    n_used = blk_end[-1:]
    fill = jnp.concatenate([off + cnt, nblk * MOE_BLOCK - cnt, n_used]).astype(jnp.int32)
    slab_row = (pos * CHUNKS).reshape(-1).astype(jnp.int32)
    return slab_row, blk_e, n_used.astype(jnp.int32), fill


def _zero_fill(fill_ref, zbuf, xs_ref, zsem, n_blocks, wait):
    def copy(start_row, n_rows):
        cp = pltpu.make_async_copy(zbuf.at[pl.ds(0, n_rows * CHUNKS)],
                                   xs_ref.at[pl.ds(pl.multiple_of(start_row * CHUNKS, CHUNKS), n_rows * CHUNKS)],
                                   zsem)
        cp.wait() if wait else cp.start()

    for e in range(N_EXPERTS):
        start, length = fill_ref[e], fill_ref[N_EXPERTS + e]
        piece = MOE_BLOCK // 2
        while piece >= 1:
            @pl.when((length & piece) != 0)
            def _():
                copy(start + (length & (-2 * piece)), piece)
            piece //= 2

    def body(j, carry):
        for part in range(MOE_BLOCK // TM):
            copy(j * MOE_BLOCK + part * TM, TM)
        return carry

    lax.fori_loop(fill_ref[2 * N_EXPERTS], n_blocks, body, 0)


def _to_slabs(ref, base, x):
    n = x.shape[0]
    for c in range(CHUNKS):
        ref[pl.ds(base + c, n, stride=CHUNKS), :] = x[:, c * LANES:(c + 1) * LANES]


def _from_slabs(ref, base, n):
    return jnp.concatenate([ref[pl.ds(base + c, n, stride=CHUNKS), :] for c in range(CHUNKS)], axis=1)


def _dispatch_kernel(pos_ref, fill_ref, x_ref, g_ref, mod_ref, xs_ref, h_scr, zbuf, sem, zsem, *,
                     rows, nt, n_blocks):
    i = pl.program_id(0)
    slot = i % 2
    slab_rows = TM * CHUNKS

    @pl.when(i == 0)
    def _():
        zbuf[...] = jnp.zeros_like(zbuf)
        _zero_fill(fill_ref, zbuf, xs_ref, zsem, n_blocks, wait=False)

    def wait_slot(s):
        for _ in range(2):
            pltpu.make_async_copy(h_scr.at[pl.ds(pl.multiple_of(s * slab_rows, slab_rows), slab_rows)],
                                  xs_ref.at[pl.ds(0, slab_rows)], sem.at[s]).wait()

    @pl.when(i >= 2)
    def _():
        wait_slot(slot)

    h = _norm_mod(x_ref[...], g_ref[...], mod_ref[0, 4:5, :], mod_ref[0, 3:4, :])
    _to_slabs(h_scr, slot * slab_rows, h)
    base = i * TM

    def body(grp, carry):
        for u in range(SUBLANES):
            t = grp * SUBLANES + u
            src = h_scr.at[pl.ds(pl.multiple_of(slot * slab_rows + t * CHUNKS, CHUNKS), CHUNKS)]
            for k in range(2):
                p = pl.multiple_of(pos_ref[k * rows + base + t], CHUNKS)
                pltpu.make_async_copy(src, xs_ref.at[pl.ds(p, CHUNKS)], sem.at[slot]).start(priority=k)
        return carry

    lax.fori_loop(0, TM // SUBLANES, body, 0)

    @pl.when(i == nt - 1)
    def _():
        wait_slot(slot)
        if nt > 1:
            wait_slot(1 - slot)
        _zero_fill(fill_ref, zbuf, xs_ref, zsem, n_blocks, wait=True)


def _dispatch(pos, fill, x, g, mods, n_blocks, *, n_x_tiles):
    r, d = x.shape
    nt = r // TM
    assert MOE_BLOCK % TM == 0 and MOE_BLOCK // 2 <= TM and d == CHUNKS * LANES
    n_rows = n_blocks * MOE_BLOCK
    return pl.pallas_call(
        functools.partial(_dispatch_kernel, rows=r, nt=nt, n_blocks=n_blocks),
        out_shape=jax.ShapeDtypeStruct((n_rows * CHUNKS, LANES), F32),
        grid_spec=pltpu.PrefetchScalarGridSpec(
            num_scalar_prefetch=2, grid=(nt,),
            in_specs=[pl.BlockSpec((TM, d), lambda i, p, f: (i, 0)),
                      pl.BlockSpec((1, d), lambda i, p, f: (0, 0)),
                      pl.BlockSpec((1, 6, d), lambda i, p, f: (i // n_x_tiles, 0, 0))],
            out_specs=pl.BlockSpec(memory_space=pl.ANY),
            scratch_shapes=[pltpu.VMEM((2 * TM * CHUNKS, LANES), F32),
                            pltpu.VMEM((TM * CHUNKS, LANES), F32),
                            pltpu.SemaphoreType.DMA((2,)), pltpu.SemaphoreType.DMA]),
        compiler_params=_params("arbitrary"), name="moe_dispatch",
    )(pos, fill, x, g.reshape(1, d), mods)


def _expert_kernel(be_ref, nu_ref, xs_ref, wg_ref, wu_ref, wd_ref, ys_ref, wg_s, wu_s, wd_s):
    j = pl.program_id(0)

    @pl.when(j < nu_ref[0])
    def _():
        @pl.when(jnp.logical_or(j == 0, be_ref[j] != be_ref[jnp.maximum(j - 1, 0)]))
        def _():
            wg_s[...] = wg_ref[0, 0].astype(BF16)
            wu_s[...] = wu_ref[0, 0].astype(BF16)
            wd_s[...] = wd_ref[0, 0].astype(BF16)

        h = _from_slabs(xs_ref, 0, MOE_BLOCK).astype(BF16)
        a = (_silu(jnp.dot(h, wg_s[...], preferred_element_type=F32))
             * jnp.dot(h, wu_s[...], preferred_element_type=F32))
        _to_slabs(ys_ref, 0, jnp.dot(a.astype(BF16), wd_s[...], preferred_element_type=F32))

    @pl.when(j >= nu_ref[0])
    def _():
        ys_ref[...] = jnp.zeros_like(ys_ref)


def _experts(blk_e, n_used, xs, wg, wu, wd, layer):
    d, de = wg.shape[-2:]
    blk = MOE_BLOCK * CHUNKS
    last = lambda j, nu: jnp.minimum(j, jnp.maximum(nu[0] - 1, 0))
    return pl.pallas_call(
        _expert_kernel,
        out_shape=jax.ShapeDtypeStruct(xs.shape, F32),
        grid_spec=pltpu.PrefetchScalarGridSpec(
            num_scalar_prefetch=2, grid=(xs.shape[0] // blk,),
            in_specs=[pl.BlockSpec((blk, LANES), lambda j, be, nu: (last(j, nu), 0)),
                      pl.BlockSpec((1, 1, d, de), lambda j, be, nu: (layer, be[last(j, nu)], 0, 0)),
                      pl.BlockSpec((1, 1, d, de), lambda j, be, nu: (layer, be[last(j, nu)], 0, 0)),
                      pl.BlockSpec((1, 1, de, d), lambda j, be, nu: (layer, be[last(j, nu)], 0, 0))],
            out_specs=pl.BlockSpec((blk, LANES), lambda j, be, nu: (j, 0)),
            scratch_shapes=[pltpu.VMEM((d, de), BF16), pltpu.VMEM((d, de), BF16), pltpu.VMEM((de, d), BF16)]),
        compiler_params=_params("arbitrary"), name="moe_experts",
    )(blk_e, n_used, xs, wg, wu, wd)


def _combine_kernel(pos_ref, x_ref, mod_ref, w_ref, ys_ref, *rest, rows, nt, final):
    if final:
        gf_ref, o_ref, buf, sem = rest
    else:
        o_ref, buf, sem = rest
    i = pl.program_id(0)
    slot = i % 2
    slab_rows = TM * CHUNKS
    where = lambda s, k: (s * 2 + k) * slab_rows

    def issue(row_tile, s):
        base = row_tile * TM

        def body(grp, carry):
            for u in range(SUBLANES):
                t = grp * SUBLANES + u
                for k in range(2):
                    p = pl.multiple_of(pos_ref[k * rows + base + t], CHUNKS)
                    dst = buf.at[pl.ds(pl.multiple_of(where(s, k) + t * CHUNKS, CHUNKS), CHUNKS)]
                    pltpu.make_async_copy(ys_ref.at[pl.ds(p, CHUNKS)], dst, sem.at[s]).start(priority=k)
            return carry

        lax.fori_loop(0, TM // SUBLANES, body, 0)

    @pl.when(i == 0)
    def _():
        issue(0, 0)

    @pl.when(i + 1 < nt)
    def _():
        issue(i + 1, 1 - slot)

    for k in range(2):
        pltpu.make_async_copy(ys_ref.at[pl.ds(0, slab_rows)],
                              buf.at[pl.ds(pl.multiple_of(where(slot, k), slab_rows), slab_rows)],
                              sem.at[slot]).wait()
    w = w_ref[...]
    y = w[:, 0:1] * _from_slabs(buf, where(slot, 0), TM) + w[:, 1:2] * _from_slabs(buf, where(slot, 1), TM)
    x = x_ref[...] + mod_ref[0, 5:6, :] * y
    if final:
        x = x * lax.rsqrt(jnp.mean(x * x, axis=-1, keepdims=True) + EPS) * gf_ref[...]
    o_ref[...] = x


def _combine(pos, x, mods, w_cols, ys, *, n_x_tiles, g_final=None):
    r, d = x.shape
    final = g_final is not None
    nt = n_x_tiles if final else r // TM
    in_specs = [pl.BlockSpec((TM, d), lambda i, p: (i, 0)),
                pl.BlockSpec((1, 6, d), lambda i, p: (i // n_x_tiles, 0, 0)),
                pl.BlockSpec((TM, 2), lambda i, p: (i, 0)),
                pl.BlockSpec(memory_space=pl.ANY)]
    args = [pos, x, mods, w_cols, ys]
    if final:
        in_specs.append(pl.BlockSpec((1, d), lambda i, p: (0, 0)))
        args.append(g_final.reshape(1, d))
    return pl.pallas_call(
        functools.partial(_combine_kernel, rows=r, nt=nt, final=final),
        out_shape=jax.ShapeDtypeStruct((nt * TM, d), F32),
        grid_spec=pltpu.PrefetchScalarGridSpec(
            num_scalar_prefetch=1, grid=(nt,), in_specs=in_specs,
            out_specs=pl.BlockSpec((TM, d), lambda i, p: (i, 0)),
            scratch_shapes=[pltpu.VMEM((2 * 2 * TM * CHUNKS, LANES), F32), pltpu.SemaphoreType.DMA((2,))]),
        compiler_params=_params("arbitrary"), name="moe_combine",
    )(*args)


def _moe(x, g, mods, w_router_t, b_router, wg, wu, wd, layer, *, n_x_tiles, g_final=None, defer=False):
    r = x.shape[0]
    n_blocks = -(-2 * r // MOE_BLOCK) + N_EXPERTS
    e01, w01, rank01, counts = _router(x, g, mods, w_router_t, b_router, seq_len=n_x_tiles * TM)
    pos, blk_e, n_used, fill = _dispatch_plan(e01, rank01, counts, n_blocks)
    xs = _dispatch(pos, fill, x, g, mods, n_blocks, n_x_tiles=n_x_tiles)
    ys = _experts(blk_e, n_used, xs, wg, wu, wd, layer)
    if defer:
        return pos, mods, w01.T, ys
    return _combine(pos, x, mods, w01.T, ys, n_x_tiles=n_x_tiles, g_final=g_final)


def _rope_tables(seq_len, n_rows_total):
    half = HEAD_DIM // 2
    inv_freq = ROPE_BASE ** (-jnp.arange(0, half, 2, dtype=F32) / half)
    first = jnp.arange(half) < (half // 2)

    def tab(n):
        ang = jnp.arange(n, dtype=F32)[:, None] * inv_freq
        ang = jnp.concatenate([ang, ang], axis=-1)
        c, s = jnp.cos(ang), jnp.sin(ang)
        return jnp.stack([c, jnp.where(first, -s, 0.0), jnp.where(first, 0.0, s)], axis=1)

    ident = jnp.stack([jnp.ones((half,), F32), jnp.zeros((half,), F32), jnp.zeros((half,), F32)])
    reps = LANES // HEAD_DIM
    n_ctx_rows = (n_rows_total - seq_len) // GRID_W
    row = jnp.concatenate([tab(seq_len // GRID_W), jnp.broadcast_to(ident, (n_ctx_rows, 3, half))], axis=0)
    row = jnp.tile(jnp.concatenate([row, jnp.zeros_like(row)], axis=-1), (1, 1, reps))
    col = jnp.stack([tab(GRID_W), jnp.broadcast_to(ident, (GRID_W, 3, half))])
    col = jnp.tile(jnp.concatenate([jnp.zeros_like(col), col], axis=-1), (1, TM // GRID_W, 1, reps))
    return row, col.transpose(0, 2, 1, 3)


def kernel(x, c, ctx, c_ctx, w_mod, b_mod, g_mix, g_ffn, g_final, conv_w_in, conv_w, conv_w_out,
           gla_w_proj, gla_w_gk1, gla_w_gk2, gla_b_gk, gla_g_norm, gla_w_out, swa_w_qkv, swa_sinks,
           swa_w_out, na_w_qkv, na_rpb, na_w_out, router_w, router_b, moe_w_gate, moe_w_up, moe_w_down):
    seq_len, d = x.shape[1], x.shape[2]
    ctx_len = ctx.shape[1]
    assert x.shape[0] == 1 and ctx_len == TM and seq_len % TM == 0 and d == D_MODEL
    assert seq_len % GRID_W == 0 and seq_len // GRID_W >= NA_KH
    n_x_tiles = seq_len // TM
    rows = seq_len + ctx_len
    xs = jnp.concatenate([x[0], ctx[0]], axis=0)
    mods_all = _ada_params(c, c_ctx, w_mod, b_mod)
    w_router = jnp.pad(router_w.astype(F32), ((0, 0), (0, LANES - N_EXPERTS)))
    w_router_t = jnp.stack(_split2(w_router))
    n_mixers = 4

    pending = None
    for i in range(DEPTH):
        kind, j = i % n_mixers, i // n_mixers
        mods = mods_all[i]
        if kind == 0:
            if pending is not None:
                xs = _combine(pending[0], xs, *pending[1:], n_x_tiles=n_x_tiles)
            u = _project(xs, g_mix[i], mods, conv_w_in[j].astype(BF16), seq_len=seq_len)
            xs = _conv_mixer(xs, u, conv_w[j], mods, conv_w_out[j].astype(BF16), seq_len=seq_len)
        elif kind == 1:
            n_extra = LANES
            w_low = jnp.concatenate([gla_w_gk1[j, 0], gla_w_gk1[j, 1],
                                     jnp.zeros((d, n_extra - 2 * GLA_RANK), F32)], axis=1).astype(BF16)
            out = _project(xs, g_mix[i], mods, gla_w_proj[j].astype(BF16), seq_len=seq_len, w_extra=w_low,
                           pending=pending)
            if pending is not None:
                xs, out = out[0], out[1:]
            u, r_low = out
            w2pad = jnp.stack(
                [jnp.zeros((n_extra, GLA_DK), F32).at[k * GLA_RANK:(k + 1) * GLA_RANK].set(gla_w_gk2[j, k])
                 for k in range(2)]).astype(BF16)
            o_f, o_b = _gla_scan(u, r_low, w2pad, gla_b_gk[j])
            xs = _gla_out(xs, o_f, o_b, u, gla_g_norm[j], mods, gla_w_out[j].astype(BF16), seq_len=seq_len)
        elif kind == 2:
            if pending is not None:
                xs = _combine(pending[0], xs, *pending[1:], n_x_tiles=n_x_tiles)
            rope = _rope_tables(seq_len, rows)
            u = _project(xs, g_mix[i], mods, swa_w_qkv[j].astype(BF16), seq_len=seq_len,
                         q_cols=SWA_HEADS * HEAD_DIM, q_scale=HEAD_DIM ** -0.5 * LOG2E, rope=rope,
                         rope_cols=(SWA_HEADS + SWA_KV_HEADS) * HEAD_DIM)
            xs = _swa_attention(u, swa_sinks[j], xs, mods, swa_w_out[j].astype(BF16), seq_len=seq_len)
        else:
            out = _project(xs, g_mix[i], mods, na_w_qkv[j].astype(BF16), seq_len=seq_len,
                           q_cols=NA_HEADS * HEAD_DIM, q_scale=HEAD_DIM ** -0.5 * LOG2E, pending=pending)
            xs, u = out if pending is not None else (xs, out)
            ypre = _na_attention(u, _na_bias_table(na_rpb[j]), n_rows=seq_len // GRID_W)
            xs = _out_proj(xs, ypre, mods, na_w_out[j].astype(BF16), seq_len=seq_len)
        last = i == DEPTH - 1
        res = _moe(xs, g_ffn[i], mods, w_router_t, router_b, moe_w_gate, moe_w_up, moe_w_down, i,
                   n_x_tiles=n_x_tiles, g_final=g_final if last else None, defer=not last)
        xs, pending = (res, None) if last else (xs, res)
    return xs[None]
```
